```python
import jax, jax.numpy as jnp
from jax import lax
import numpy as np

D_MODEL = 1024
BATCH = 8
SEQ = 2048
DEPTH = 2

ATT_HEADS = 8
ATT_HEAD_DIM = 64
ATT_WIDTH = ATT_HEADS * ATT_HEAD_DIM
MOBA_BLOCK = 256
MOBA_TOPK = 3
MOBA_Q_CHUNK = 32
RET_HEADS = 4
RET_HEAD_DIM = 128
RET_WIDTH = RET_HEADS * RET_HEAD_DIM
RET_CHUNK = 128
ROPE_BASE = 10000.0
IN_PROJ_WIDTH = 3 * ATT_WIDTH + 4 * RET_WIDTH
D_FF = 2816
N_EXPERTS = 8
TOP_K = 2
EXPERT_D_FF = 2816
MOE_GROUP = 128
EPS = 1e-6

kernel_name = "hymba_moba_retnet_moe_adaln"


def rms_norm(x, g):
    xf = x.astype(jnp.float32)
    y = xf * lax.rsqrt(jnp.mean(xf * xf, axis=-1, keepdims=True) + EPS)
    return (y * g.astype(jnp.float32)).astype(x.dtype)


def rotary(x):
    s, d = x.shape[1], x.shape[-1]
    half = d // 2
    inv_freq = ROPE_BASE ** (-jnp.arange(half, dtype=jnp.float32) / half)
    ang = jnp.arange(s, dtype=jnp.float32)[:, None] * inv_freq[None, :]
    cos = jnp.cos(ang)[None, :, None, :]
    sin = jnp.sin(ang)[None, :, None, :]
    xf = x.astype(jnp.float32)
    x1, x2 = xf[..., :half], xf[..., half:]
    return jnp.concatenate([x1 * cos - x2 * sin, x1 * sin + x2 * cos], axis=-1).astype(x.dtype)


def moba_attention(q, k, v):
    b, h, s, dh = q.shape
    nb = -(-s // MOBA_BLOCK)
    pad = nb * MOBA_BLOCK - s
    kp = jnp.pad(k, ((0, 0), (0, 0), (0, pad), (0, 0)))
    vp = jnp.pad(v, ((0, 0), (0, 0), (0, pad), (0, 0)))
    kb = kp.reshape(b, h, nb, MOBA_BLOCK, dh)
    vb = vp.reshape(b, h, nb, MOBA_BLOCK, dh)
    scale = dh ** -0.5
    n_sel = min(MOBA_TOPK, nb - 1)
    q_block = jnp.arange(s) // MOBA_BLOCK
    if n_sel > 0:
        k_mean = jnp.mean(kb.astype(jnp.float32), axis=3)
        gate = jnp.einsum('bhsd,bhnd->bhsn', q.astype(jnp.float32), k_mean)
        past = jnp.arange(nb)[None, :] < q_block[:, None]
        gate = jnp.where(past[None, None], gate, -1e30)
        _, sel = lax.top_k(gate, n_sel)
        sel_valid = sel < q_block[None, None, :, None]
    bi = jnp.arange(b)[:, None, None, None]
    hi = jnp.arange(h)[None, :, None, None]

    def chunk_fn(ci):
        start = ci * MOBA_Q_CHUNK
        qc = lax.dynamic_slice_in_dim(q, start, MOBA_Q_CHUNK, axis=2)
        pos = start + jnp.arange(MOBA_Q_CHUNK)
        own = start // MOBA_BLOCK
        k_own = lax.dynamic_index_in_dim(kb, own, axis=2, keepdims=False)
        v_own = lax.dynamic_index_in_dim(vb, own, axis=2, keepdims=False)
        key_pos = own * MOBA_BLOCK + jnp.arange(MOBA_BLOCK)
        s_own = jnp.einsum('bhqd,bhkd->bhqk', qc, k_own).astype(jnp.float32) * scale
        s_own = jnp.where(key_pos[None, :] <= pos[:, None], s_own, -jnp.inf)
        if n_sel > 0:
            sel_c = lax.dynamic_slice_in_dim(sel, start, MOBA_Q_CHUNK, axis=2)
            val_c = lax.dynamic_slice_in_dim(sel_valid, start, MOBA_Q_CHUNK, axis=2)
            k_sel = kb[bi, hi, sel_c]
            v_sel = vb[bi, hi, sel_c]
            s_sel = jnp.einsum('bhqd,bhqnkd->bhqnk', qc, k_sel).astype(jnp.float32) * scale
            s_sel = jnp.where(val_c[..., None], s_sel, -jnp.inf)
            s_sel = s_sel.reshape(b, h, MOBA_Q_CHUNK, n_sel * MOBA_BLOCK)
            p = jax.nn.softmax(jnp.concatenate([s_own, s_sel], axis=-1), axis=-1)
            p_own = p[..., :MOBA_BLOCK].astype(v.dtype)
            p_sel = p[..., MOBA_BLOCK:].reshape(b, h, MOBA_Q_CHUNK, n_sel, MOBA_BLOCK).astype(v.dtype)
            o = (jnp.einsum('bhqk,bhkd->bhqd', p_own, v_own)
                 + jnp.einsum('bhqnk,bhqnkd->bhqd', p_sel, v_sel))
        else:
            p_own = jax.nn.softmax(s_own, axis=-1).astype(v.dtype)
            o = jnp.einsum('bhqk,bhkd->bhqd', p_own, v_own)
        return o

    out = lax.map(chunk_fn, jnp.arange(s // MOBA_Q_CHUNK))
    return jnp.transpose(out, (1, 2, 0, 3, 4)).reshape(b, h, s, dh)


def retention(q, k, v):
    b, h, s, d = q.shape
    c = RET_CHUNK
    nc = s // c
    dt = v.dtype
    lg = jnp.log(1.0 - 2.0 ** (-5.0 - jnp.arange(h, dtype=jnp.float32)))
    idx = jnp.arange(c, dtype=jnp.float32)
    qf = q.astype(jnp.float32).reshape(b, h, nc, c, d)
    kf = (k.astype(jnp.float32) * d ** -0.5).reshape(b, h, nc, c, d)
    vf = v.astype(jnp.float32).reshape(b, h, nc, c, d)
    diff = idx[:, None] - idx[None, :]
    dmask = jnp.where(diff >= 0, jnp.exp(jnp.maximum(diff, 0.0)[None] * lg[:, None, None]), 0.0)
    scores = jnp.einsum('bhncd,bhnmd->bhncm', qf, kf) * dmask[None, :, None]
    inner = jnp.einsum('bhncm,bhnme->bhnce', scores, vf)
    k_dec = kf * jnp.exp((c - 1 - idx)[None, :] * lg[:, None])[None, :, None, :, None]
    kv = jnp.einsum('bhnmd,bhnme->bhnde', k_dec, vf)
    chunk_decay = jnp.exp(c * lg)[None, :, None, None]

    def step(state, kv_n):
        return state * chunk_decay + kv_n, state

    _, r_prev = lax.scan(step, jnp.zeros((b, h, d, d), jnp.float32), jnp.moveaxis(kv, 2, 0))
    r_prev = jnp.moveaxis(r_prev, 0, 2)
    q_dec = qf * jnp.exp((idx + 1.0)[None, :] * lg[:, None])[None, :, None, :, None]
    cross = jnp.einsum('bhncd,bhnde->bhnce', q_dec, r_prev)
    return (inner + cross).reshape(b, h, s, d).astype(dt)


def hybrid_mixer(h, w_in, w_out, att_out_g, ret_out_g):
    b, s, _ = h.shape
    proj = h @ w_in
    A, R = ATT_WIDTH, RET_WIDTH
    q_a, k_a, v_a, q_r, k_r, v_r, g_r = jnp.split(
        proj, [A, 2 * A, 3 * A, 3 * A + R, 3 * A + 2 * R, 3 * A + 3 * R], axis=-1)

    def att_heads(t):
        return t.reshape(b, s, ATT_HEADS, ATT_HEAD_DIM).transpose(0, 2, 1, 3)

    o_a = moba_attention(att_heads(q_a), att_heads(k_a), att_heads(v_a))
    o_a = o_a.transpose(0, 2, 1, 3).reshape(b, s, A)
    o_a = rms_norm(o_a, att_out_g)

    def ret_heads(t):
        return t.reshape(b, s, RET_HEADS, RET_HEAD_DIM)

    qr = rotary(ret_heads(q_r)).transpose(0, 2, 1, 3)
    kr = rotary(ret_heads(k_r)).transpose(0, 2, 1, 3)
    vr = ret_heads(v_r).transpose(0, 2, 1, 3)
    o_r = retention(qr, kr, vr).transpose(0, 2, 1, 3)
    o_r = rms_norm(o_r, ret_out_g.reshape(RET_HEADS, RET_HEAD_DIM)).reshape(b, s, R)
    o_r = jax.nn.silu(g_r) * o_r
    return jnp.concatenate([o_a, o_r], axis=-1) @ w_out


def swiglu(x, w_gate, w_up, w_down):
    return (jax.nn.silu(x @ w_gate) * (x @ w_up)) @ w_down


def moe_ffn(h, w_router, w_gate, w_up, w_down):
    b, s, d = h.shape
    xf = h.reshape(-1, d)
    n = xf.shape[0]
    e_count = w_gate.shape[0]
    g = MOE_GROUP
    logits = (xf @ w_router).astype(jnp.float32)
    top_logits, top_idx = lax.top_k(logits, TOP_K)
    top_w = jax.nn.softmax(top_logits, axis=-1)
    a = n * TOP_K
    e_flat = top_idx.reshape(-1)
    t_flat = jnp.repeat(jnp.arange(n, dtype=jnp.int32), TOP_K)
    w_flat = top_w.reshape(-1)
    order = jnp.argsort(e_flat, stable=True)
    e_sorted = e_flat[order]
    counts = jnp.bincount(e_flat, length=e_count)
    starts = jnp.cumsum(counts) - counts
    padded = (counts + g - 1) // g * g
    pad_ends = jnp.cumsum(padded)
    pad_starts = pad_ends - padded
    dest = pad_starts[e_sorted] + (jnp.arange(a) - starts[e_sorted])
    p_total = (-(-a // g) + e_count) * g
    tok_buf = jnp.full((p_total,), n, jnp.int32).at[dest].set(t_flat[order])
    w_buf = jnp.zeros((p_total,), jnp.float32).at[dest].set(w_flat[order])
    nblk = p_total // g
    blk_e = jnp.minimum(jnp.searchsorted(pad_ends, jnp.arange(nblk) * g, side='right'), e_count - 1)
    x_pad = jnp.concatenate([xf, jnp.zeros((1, d), xf.dtype)], axis=0)
    xb = x_pad[tok_buf].reshape(nblk, g, d)

    def expert_block(args):
        xg, e = args
        return swiglu(xg, w_gate[e], w_up[e], w_down[e])

    yb = lax.map(expert_block, (xb, blk_e)).reshape(p_total, d)
    yb = yb * w_buf[:, None].astype(yb.dtype)
    y = jax.ops.segment_sum(yb, tok_buf, num_segments=n + 1)[:n]
    return y.reshape(b, s, d)


def setup_inputs(seed: int = 0) -> dict:
    key = jax.random.key(seed)
    ks = jax.random.split(key, 20)
    n_dense = (DEPTH + 1) // 2
    n_moe = DEPTH // 2
    D = D_MODEL

    def nrm(k, shape, scale):
        return jax.random.normal(k, shape, jnp.float32) * scale

    return {
        "x": nrm(ks[0], (BATCH, SEQ, D), 1.0),
        "c": nrm(ks[1], (BATCH, D), 1.0),
        "norm_mix_g": 1.0 + nrm(ks[2], (DEPTH, D), 0.05),
        "norm_ffn_g": 1.0 + nrm(ks[3], (DEPTH, D), 0.05),
        "ada_w": nrm(ks[4], (DEPTH, D, 6 * D), 0.5 * D ** -0.5),
        "ada_b": nrm(ks[5], (DEPTH, 6 * D), 0.02),
        "w_in": nrm(ks[6], (DEPTH, D, IN_PROJ_WIDTH), D ** -0.5),
        "w_out": nrm(ks[7], (DEPTH, ATT_WIDTH + RET_WIDTH, D), (ATT_WIDTH + RET_WIDTH) ** -0.5),
        "att_out_g": 1.0 + nrm(ks[8], (DEPTH, ATT_WIDTH), 0.05),
        "ret_out_g": 1.0 + nrm(ks[9], (DEPTH, RET_WIDTH), 0.05),
        "ffn_w_gate": nrm(ks[10], (n_dense, D, D_FF), D ** -0.5),
        "ffn_w_up": nrm(ks[11], (n_dense, D, D_FF), D ** -0.5),
        "ffn_w_down": nrm(ks[12], (n_dense, D_FF, D), D_FF ** -0.5),
        "router_w": nrm(ks[13], (n_moe, D, N_EXPERTS), D ** -0.5),
        "moe_w_gate": nrm(ks[14], (n_moe, N_EXPERTS, D, EXPERT_D_FF), D ** -0.5),
        "moe_w_up": nrm(ks[15], (n_moe, N_EXPERTS, D, EXPERT_D_FF), D ** -0.5),
        "moe_w_down": nrm(ks[16], (n_moe, N_EXPERTS, EXPERT_D_FF, D), EXPERT_D_FF ** -0.5),
        "final_norm_g": 1.0 + nrm(ks[17], (D,), 0.05),
    }


def reference(x, c, norm_mix_g, norm_ffn_g, ada_w, ada_b, w_in, w_out, att_out_g, ret_out_g,
              ffn_w_gate, ffn_w_up, ffn_w_down, router_w, moe_w_gate, moe_w_up, moe_w_down,
              final_norm_g):
    cs = jax.nn.silu(c)
    for l in range(DEPTH):
        mod = cs @ ada_w[l] + ada_b[l]
        sh1, sc1, g1, sh2, sc2, g2 = jnp.split(mod, 6, axis=-1)
        h = rms_norm(x, norm_mix_g[l]) * (1.0 + sc1[:, None]) + sh1[:, None]
        x = x + g1[:, None] * hybrid_mixer(h, w_in[l], w_out[l], att_out_g[l], ret_out_g[l])
        h = rms_norm(x, norm_ffn_g[l]) * (1.0 + sc2[:, None]) + sh2[:, None]
        if l % 2 == 0:
            f = swiglu(h, ffn_w_gate[l // 2], ffn_w_up[l // 2], ffn_w_down[l // 2])
        else:
            f = moe_ffn(h, router_w[l // 2], moe_w_gate[l // 2], moe_w_up[l // 2], moe_w_down[l // 2])
        x = x + g2[:, None] * f
    return rms_norm(x, final_norm_g)
```

```python
import functools

import jax
import jax.numpy as jnp
from jax import lax
from jax.experimental import pallas as pl
from jax.experimental.pallas import tpu as pltpu

F32 = jnp.float32
BF16 = jnp.bfloat16

LANES = 128
V7X_VMEM_LIMIT_BYTES = 56 * 1024 * 1024

ATT_HEADS = 8
ATT_HEAD_DIM = 64
ATT_WIDTH = ATT_HEADS * ATT_HEAD_DIM
MOBA_BLOCK = 256
MOBA_TOPK = 3
RET_HEADS = 4
RET_HEAD_DIM = 128
RET_WIDTH = RET_HEADS * RET_HEAD_DIM
RET_CHUNK = 128
ROPE_BASE = 10000.0
N_EXPERTS = 8
TOP_K = 2
EPS = 1e-6
GROUP_WIDTH = 512
N_GROUPS = 7
MOE_ROWS = 512
DMA_ROWS_PER_STEP = 2048


def _tile(total, target, mult):
    best = None
    t = mult
    while t <= min(total, target):
        if total % t == 0:
            best = t
        t += mult
    return best if best is not None else total


def _params(*sem):
    return pltpu.CompilerParams(dimension_semantics=sem, vmem_limit_bytes=V7X_VMEM_LIMIT_BYTES)


def _rms(x):
    return x * lax.rsqrt(jnp.mean(x * x, axis=-1, keepdims=True) + EPS)


def _silu(x):
    return x * jax.nn.sigmoid(x)


def _adaln_kernel(c_ref, w_ref, b_ref, o_ref):
    cs = _silu(c_ref[...]).astype(BF16)
    o_ref[0] = jnp.dot(cs, w_ref[0].astype(BF16), preferred_element_type=F32) + b_ref[0]


def _adaln(c, ada_w, ada_b):
    depth, d, w = ada_w.shape
    b = c.shape[0]
    tn = _tile(w, 1536, LANES)
    return pl.pallas_call(
        _adaln_kernel,
        grid=(depth, w // tn),
        in_specs=[
            pl.BlockSpec((b, d), lambda l, j: (0, 0)),
            pl.BlockSpec((1, d, tn), lambda l, j: (l, 0, j)),
            pl.BlockSpec((1, 1, tn), lambda l, j: (l, 0, j)),
        ],
        out_specs=pl.BlockSpec((1, b, tn), lambda l, j: (l, 0, j)),
        out_shape=jax.ShapeDtypeStruct((depth, b, w), F32),
        compiler_params=_params("arbitrary", "arbitrary"),
    )(c, ada_w, ada_b.reshape(depth, 1, w))


def _inproj_kernel(x_ref, mod_ref, g_ref, w_ref, cq_ref, sq_ref, ck_ref, sk_ref, o_ref, h_scr):
    j = pl.program_id(1)

    @pl.when(j == 0)
    def _():
        m = mod_ref[0]
        h = _rms(x_ref[...]) * g_ref[...] * (1.0 + m[1:2]) + m[0:1]
        h_scr[...] = h.astype(BF16)

    acc = jnp.dot(h_scr[...], w_ref[...], preferred_element_type=F32)

    def rotate(cos_ref, sin_ref):
        cos = cos_ref[...]
        sin = sin_ref[...]
        for hd in range(RET_HEADS):
            a = acc[:, hd * RET_HEAD_DIM:(hd + 1) * RET_HEAD_DIM]
            r = a * cos + pltpu.roll(a, RET_HEAD_DIM // 2, 1) * sin
            o_ref[:, hd * RET_HEAD_DIM:(hd + 1) * RET_HEAD_DIM] = r.astype(o_ref.dtype)

    @pl.when(j == 3)
    def _():
        rotate(cq_ref, sq_ref)

    @pl.when(j == 4)
    def _():
        rotate(ck_ref, sk_ref)

    @pl.when(jnp.logical_and(j != 3, j != 4))
    def _():
        o_ref[...] = acc.astype(o_ref.dtype)


def _rotary_tables(s):
    half = RET_HEAD_DIM // 2
    inv_freq = ROPE_BASE ** (-jnp.arange(half, dtype=F32) / half)
    ang = jnp.arange(s, dtype=F32)[:, None] * inv_freq[None, :]
    cos = jnp.cos(ang)
    sin = jnp.sin(ang)
    cos2 = jnp.concatenate([cos, cos], axis=-1)
    sin2 = jnp.concatenate([-sin, sin], axis=-1)
    k_scale = RET_HEAD_DIM ** -0.5
    return cos2, sin2, cos2 * k_scale, sin2 * k_scale


def _inproj(x2d, mod, g, w_bf16, tables, seq):
    n, d = x2d.shape
    tm = _tile(seq, 1024, 16)
    per_seq = seq // tm
    return pl.pallas_call(
        _inproj_kernel,
        grid=(n // tm, N_GROUPS),
        in_specs=[
            pl.BlockSpec((tm, d), lambda i, j: (i, 0)),
            pl.BlockSpec((1, 6, d), lambda i, j: (i // per_seq, 0, 0)),
            pl.BlockSpec((1, d), lambda i, j: (0, 0)),
            pl.BlockSpec((d, GROUP_WIDTH), lambda i, j: (0, j)),
        ] + [pl.BlockSpec((tm, RET_HEAD_DIM), lambda i, j: (i % per_seq, 0))] * 4,
        out_specs=pl.BlockSpec((tm, GROUP_WIDTH), lambda i, j: (i, j)),
        out_shape=jax.ShapeDtypeStruct((n, N_GROUPS * GROUP_WIDTH), BF16),
        scratch_shapes=[pltpu.VMEM((tm, d), BF16)],
        compiler_params=_params("arbitrary", "arbitrary"),
    )(x2d, mod, g.reshape(1, d), w_bf16, *tables)


def _moba_kernel(q_ref, k_ref, v_ref, o_ref):
    s = q_ref.shape[0]
    blk = MOBA_BLOCK
    nb = s // blk
    n_sel = min(MOBA_TOPK, nb - 1)
    shift = blk.bit_length() - 1
    scale = ATT_HEAD_DIM ** -0.5
    lane = lax.broadcasted_iota(jnp.int32, (1, LANES), 1)
    k_all = k_ref[...]
    v_all = v_ref[...]
    q_all = q_ref[...]

    r_id = lax.broadcasted_iota(jnp.int32, (LANES, s), 0)
    c_id = lax.broadcasted_iota(jnp.int32, (LANES, s), 1)
    in_blk = jnp.right_shift(c_id, shift) == r_id
    expand = jnp.where(in_blk, 1.0, 0.0).astype(BF16)
    k_mean = jnp.dot(jnp.where(in_blk, 1.0 / blk, 0.0).astype(BF16), k_all,
                     preferred_element_type=F32).astype(BF16)

    heads = []
    for hh in range(2):
        hm = jnp.right_shift(lane, ATT_HEAD_DIM.bit_length() - 1) == hh
        qh = jnp.where(hm, q_all, jnp.zeros_like(q_all))
        gate = lax.dot_general(qh, k_mean, (((1,), (1,)), ((), ())), preferred_element_type=F32)
        heads.append((hm, qh, gate))

    for i in range(nb):
        nk = (i + 1) * blk
        row = lax.broadcasted_iota(jnp.int32, (blk, nk), 0)
        col = lax.broadcasted_iota(jnp.int32, (blk, nk), 1)
        causal = (col - i * blk) <= row
        outs = []
        for hm, qh, gate in heads:
            qi = qh[i * blk:(i + 1) * blk]
            sc = lax.dot_general(qi, k_all[:nk], (((1,), (1,)), ((), ())),
                                 preferred_element_type=F32) * scale
            if i == 0 or n_sel == 0:
                allowed = jnp.logical_and(causal, col >= i * blk)
            else:
                past = col < i * blk
                gm = jnp.where(lane < i, gate[i * blk:(i + 1) * blk], -1e30)
                rank = jnp.zeros((blk, LANES), jnp.int32)
                for jp in range(nb):
                    cv = gm[:, jp:jp + 1]
                    beats = jnp.logical_or(cv > gm, jnp.logical_and(cv == gm, jp < lane))
                    rank = rank + jnp.where(beats, 1, 0)
                sel = jnp.logical_and(rank < n_sel, lane < i)
                selx = jnp.dot(jnp.where(sel, 1.0, 0.0).astype(BF16), expand[:, :nk],
                               preferred_element_type=F32)
                allowed = jnp.logical_or(jnp.logical_and(past, selx > 0.5),
                                         jnp.logical_and(jnp.logical_not(past), causal))
            sc = jnp.where(allowed, sc, -jnp.inf)
            m = jnp.max(sc, axis=-1, keepdims=True)
            p = jnp.exp(sc - m)
            l = jnp.sum(p, axis=-1, keepdims=True)
            o = jnp.dot(p.astype(BF16), v_all[:nk], preferred_element_type=F32) / l
            outs.append(o)
        o_ref[i * blk:(i + 1) * blk, :] = jnp.where(heads[0][0], outs[0], outs[1]).astype(o_ref.dtype)


def _moba(proj, batch, seq):
    n = proj.shape[0]
    pairs = ATT_WIDTH // LANES
    spec = lambda off: pl.BlockSpec((seq, LANES), lambda b, p: (b, off + p))
    return pl.pallas_call(
        _moba_kernel,
        grid=(batch, pairs),
        in_specs=[spec(0), spec(pairs), spec(2 * pairs)],
        out_specs=pl.BlockSpec((seq, LANES), lambda b, p: (b, p)),
        out_shape=jax.ShapeDtypeStruct((n, ATT_WIDTH), BF16),
        compiler_params=_params("arbitrary", "arbitrary"),
    )(proj, proj, proj)


def _retention_kernel(q_ref, k_ref, v_ref, g_ref, dm_ref, qd_ref, kd_ref, cd_ref, og_ref, o_ref):
    s = q_ref.shape[0]
    c = RET_CHUNK
    dmask = dm_ref[0]
    qdec = qd_ref[0]
    kdec = kd_ref[0]
    cdec = cd_ref[0]
    out_g = og_ref[0]
    state = jnp.zeros((RET_HEAD_DIM, RET_HEAD_DIM), F32)
    for n in range(s // c):
        rows = slice(n * c, (n + 1) * c)
        qc = q_ref[rows, :]
        kc = k_ref[rows, :]
        vc = v_ref[rows, :]
        scores = lax.dot_general(qc, kc, (((1,), (1,)), ((), ())), preferred_element_type=F32) * dmask
        inner = jnp.dot(scores.astype(BF16), vc, preferred_element_type=F32)
        cross = jnp.dot(qc, state.astype(BF16), preferred_element_type=F32) * qdec
        o = _rms(inner + cross) * out_g
        gate = g_ref[rows, :].astype(F32)
        o_ref[rows, :] = (_silu(gate) * o).astype(o_ref.dtype)
        k_dec = (kc.astype(F32) * kdec).astype(BF16)
        kv = lax.dot_general(k_dec, vc, (((0,), (0,)), ((), ())), preferred_element_type=F32)
        state = state * cdec + kv


def _retention_tables():
    h, c = RET_HEADS, RET_CHUNK
    lg = jnp.log(1.0 - 2.0 ** (-5.0 - jnp.arange(h, dtype=F32)))
    idx = jnp.arange(c, dtype=F32)
    diff = idx[:, None] - idx[None, :]
    dmask = jnp.where(diff >= 0, jnp.exp(jnp.maximum(diff, 0.0)[None] * lg[:, None, None]), 0.0)
    rows = lambda v: jnp.broadcast_to(v[:, :, None], (h, c, RET_HEAD_DIM))
    qdec = rows(jnp.exp((idx + 1.0)[None, :] * lg[:, None]))
    kdec = rows(jnp.exp((c - 1 - idx)[None, :] * lg[:, None]))
    cdec = jnp.broadcast_to(jnp.exp(c * lg)[:, None, None], (h, RET_HEAD_DIM, RET_HEAD_DIM))
    return dmask, qdec, kdec, cdec


def _retention(proj, ret_out_g, batch, seq):
    n = proj.shape[0]
    base = 3 * ATT_WIDTH // LANES
    spec = lambda off: pl.BlockSpec((seq, LANES), lambda b, h: (b, base + off * RET_HEADS + h))
    table = pl.BlockSpec((1, RET_CHUNK, RET_HEAD_DIM), lambda b, h: (h, 0, 0))
    return pl.pallas_call(
        _retention_kernel,
        grid=(batch, RET_HEADS),
        in_specs=[spec(0), spec(1), spec(2), spec(3), table, table, table, table,
                  pl.BlockSpec((1, 1, RET_HEAD_DIM), lambda b, h: (h, 0, 0))],
        out_specs=pl.BlockSpec((seq, LANES), lambda b, h: (b, h)),
        out_shape=jax.ShapeDtypeStruct((n, RET_WIDTH), BF16),
        compiler_params=_params("arbitrary", "arbitrary"),
    )(proj, proj, proj, proj, *_retention_tables(), ret_out_g.reshape(RET_HEADS, 1, RET_HEAD_DIM))


def _top2_route(logits):
    lane = lax.broadcasted_iota(jnp.int32, logits.shape, 1)
    lg = jnp.where(lane < N_EXPERTS, logits, -jnp.inf)
    m1 = jnp.max(lg, axis=-1, keepdims=True)
    i1 = jnp.min(jnp.where(lg == m1, lane, LANES), axis=-1, keepdims=True)
    lg2 = jnp.where(lane == i1, -jnp.inf, lg)
    m2 = jnp.max(lg2, axis=-1, keepdims=True)
    i2 = jnp.min(jnp.where(lg2 == m2, lane, LANES), axis=-1, keepdims=True)
    e = jnp.exp(m2 - m1)
    w1 = 1.0 / (1.0 + e)
    w2 = e / (1.0 + e)
    out = jnp.where(lane == 0, i1.astype(F32), 0.0)
    out = jnp.where(lane == 1, i2.astype(F32), out)
    out = jnp.where(lane == 2, w1, out)
    return jnp.where(lane == 3, w2, out)


def _outproj_kernel(oa_ref, or_ref, x_ref, w_ref, ag_ref, mod_ref, ng_ref, *rest, route):
    if route:
        rw_ref, x1_ref, h_ref, rt_ref = rest
    else:
        x1_ref, h_ref = rest
    m = mod_ref[0]
    oa = (_rms(oa_ref[...].astype(F32)) * ag_ref[...]).astype(BF16)
    mix = (jnp.dot(oa, w_ref[:ATT_WIDTH, :], preferred_element_type=F32)
           + jnp.dot(or_ref[...], w_ref[ATT_WIDTH:, :], preferred_element_type=F32))
    x1 = x_ref[...] + m[2:3] * mix
    x1_ref[...] = x1
    h = _rms(x1) * ng_ref[...] * (1.0 + m[4:5]) + m[3:4]
    h_ref[...] = h.astype(h_ref.dtype)
    if route:
        logits = jnp.dot(h.astype(BF16), rw_ref[...], preferred_element_type=F32)
        rt_ref[...] = _top2_route(logits)


def _outproj(o_a, o_r, x2d, w_bf16, att_g, mod, norm_g, seq, router_w=None):
    n, d = x2d.shape
    tm = _tile(seq, 512, 16)
    per_seq = seq // tm
    route = router_w is not None
    row = lambda w: pl.BlockSpec((tm, w), lambda i: (i, 0))
    full = lambda a, b: pl.BlockSpec((a, b), lambda i: (0, 0))
    in_specs = [row(ATT_WIDTH), row(RET_WIDTH), row(d), full(ATT_WIDTH + RET_WIDTH, d),
                full(1, ATT_WIDTH), pl.BlockSpec((1, 6, d), lambda i: (i // per_seq, 0, 0)), full(1, d)]
    args = [o_a, o_r, x2d, w_bf16, att_g.reshape(1, ATT_WIDTH), mod, norm_g.reshape(1, d)]
    out_specs = [row(d), row(d)]
    out_shape = [jax.ShapeDtypeStruct((n, d), F32), jax.ShapeDtypeStruct((n, d), F32 if route else BF16)]
    if route:
        rw = jnp.zeros((d, LANES), BF16).at[:, :N_EXPERTS].set(router_w.astype(BF16))
        in_specs.append(full(d, LANES))
        args.append(rw)
        out_specs.append(row(LANES))
        out_shape.append(jax.ShapeDtypeStruct((n, LANES), F32))
    return pl.pallas_call(
        functools.partial(_outproj_kernel, route=route),
        grid=(n // tm,),
        in_specs=in_specs,
        out_specs=out_specs,
        out_shape=out_shape,
        compiler_params=_params("arbitrary"),
    )(*args)


def _swiglu_step(h, wg, wu, wd):
    g = jnp.dot(h, wg, preferred_element_type=F32)
    u = jnp.dot(h, wu, preferred_element_type=F32)
    return jnp.dot((_silu(g) * u).astype(BF16), wd, preferred_element_type=F32)


def _dense_ffn_kernel(h_ref, x1_ref, mod_ref, wg_ref, wu_ref, wd_ref, fg_ref, o_ref, acc_ref, *, final):
    f = pl.program_id(1)
    part = _swiglu_step(h_ref[...], wg_ref[...], wu_ref[...], wd_ref[...])

    @pl.when(f == 0)
    def _():
        acc_ref[...] = part

    @pl.when(f != 0)
    def _():
        acc_ref[...] += part

    @pl.when(f == pl.num_programs(1) - 1)
    def _():
        x2 = x1_ref[...] + mod_ref[0][5:6] * acc_ref[...]
        o_ref[...] = _rms(x2) * fg_ref[...] if final else x2


def _dense_ffn(h, x1, mod, wg, wu, wd, final_g, seq, final):
    n, d = x1.shape
    ff = wg.shape[1]
    tm = _tile(seq, 1024, 16)
    tf = _tile(ff, 256, LANES)
    per_seq = seq // tm
    return pl.pallas_call(
        functools.partial(_dense_ffn_kernel, final=final),
        grid=(n // tm, ff // tf),
        in_specs=[
            pl.BlockSpec((tm, d), lambda i, f: (i, 0)),
            pl.BlockSpec((tm, d), lambda i, f: (i, 0)),
            pl.BlockSpec((1, 6, d), lambda i, f: (i // per_seq, 0, 0)),
            pl.BlockSpec((d, tf), lambda i, f: (0, f)),
            pl.BlockSpec((d, tf), lambda i, f: (0, f)),
            pl.BlockSpec((tf, d), lambda i, f: (f, 0)),
            pl.BlockSpec((1, d), lambda i, f: (0, 0)),
        ],
        out_specs=pl.BlockSpec((tm, d), lambda i, f: (i, 0)),
        out_shape=jax.ShapeDtypeStruct((n, d), F32),
        scratch_shapes=[pltpu.VMEM((tm, d), F32)],
        compiler_params=_params("arbitrary", "arbitrary"),
    )(h, x1, mod, wg, wu, wd, final_g.reshape(1, d))


def _moe_ffn_kernel(blk_e_ref, x_ref, wg_ref, wu_ref, wd_ref, o_ref, xs_ref, acc_ref):
    f = pl.program_id(1)

    @pl.when(f == 0)
    def _():
        xs_ref[...] = x_ref[...].astype(BF16)

    part = _swiglu_step(xs_ref[...], wg_ref[0], wu_ref[0], wd_ref[0])

    @pl.when(f == 0)
    def _():
        acc_ref[...] = part

    @pl.when(f != 0)
    def _():
        acc_ref[...] += part

    @pl.when(f == pl.num_programs(1) - 1)
    def _():
        o_ref[...] = acc_ref[...]


def _moe_grouped(xb, blk_e, wg, wu, wd):
    p, d = xb.shape
    ff = wg.shape[2]
    g = MOE_ROWS
    tf = _tile(ff, 256, LANES)
    grid_spec = pltpu.PrefetchScalarGridSpec(
        num_scalar_prefetch=1,
        grid=(p // g, ff // tf),
        in_specs=[
            pl.BlockSpec((g, d), lambda b, f, be: (b, 0)),
            pl.BlockSpec((1, d, tf), lambda b, f, be: (be[b], 0, f)),
            pl.BlockSpec((1, d, tf), lambda b, f, be: (be[b], 0, f)),
            pl.BlockSpec((1, tf, d), lambda b, f, be: (be[b], f, 0)),
        ],
        out_specs=pl.BlockSpec((g, d), lambda b, f, be: (b, 0)),
        scratch_shapes=[pltpu.VMEM((g, d), BF16), pltpu.VMEM((g, d), F32)],
    )
    return pl.pallas_call(
        _moe_ffn_kernel,
        grid_spec=grid_spec,
        out_shape=jax.ShapeDtypeStruct((p, d), F32),
        compiler_params=_params("arbitrary", "arbitrary"),
    )(blk_e, xb, wg, wu, wd)


def _permute_rows_kernel(sidx_ref, didx_ref, src_ref, *rest, rows):
    dst_ref, sem = rest[-2], rest[-1]
    base = pl.program_id(0) * rows

    def row_copy(s, t):
        return pltpu.make_async_copy(src_ref.at[pl.ds(s, 1)], dst_ref.at[pl.ds(t, 1)], sem)

    def issue(a, carry):
        row_copy(sidx_ref[base + a], didx_ref[base + a]).start()
        return carry

    lax.fori_loop(0, rows, issue, 0)

    def drain(a, carry):
        row_copy(0, 0).wait()
        return carry

    lax.fori_loop(0, rows, drain, 0)


def _permute_rows(src, sidx, didx, out_rows, init=None):
    d = src.shape[1]
    a = sidx.shape[0]
    rows = _tile(a, DMA_ROWS_PER_STEP, 1)
    any_spec = pl.BlockSpec(memory_space=pl.ANY)
    args = [sidx, didx, src]
    in_specs = [any_spec]
    aliases = {}
    if init is not None:
        args.append(init)
        in_specs.append(any_spec)
        aliases = {3: 0}
    grid_spec = pltpu.PrefetchScalarGridSpec(
        num_scalar_prefetch=2,
        grid=(a // rows,),
        in_specs=in_specs,
        out_specs=any_spec,
        scratch_shapes=[pltpu.SemaphoreType.DMA(())],
    )
    return pl.pallas_call(
        functools.partial(_permute_rows_kernel, rows=rows),
        grid_spec=grid_spec,
        out_shape=jax.ShapeDtypeStruct((out_rows, d), src.dtype),
        input_output_aliases=aliases,
        compiler_params=_params("arbitrary"),
    )(*args)


def _combine_kernel(x1_ref, y_ref, rt_ref, mod_ref, fg_ref, o_ref, *, final):
    d = x1_ref.shape[1]
    rt = rt_ref[...]
    y = rt[:, 2:3] * y_ref[:, :d] + rt[:, 3:4] * y_ref[:, d:]
    x2 = x1_ref[...] + mod_ref[0][5:6] * y
    o_ref[...] = _rms(x2) * fg_ref[...] if final else x2


def _combine(x1, y2, route, mod, final_g, seq, final):
    n, d = x1.shape
    tm = _tile(seq, 512, 16)
    per_seq = seq // tm
    return pl.pallas_call(
        functools.partial(_combine_kernel, final=final),
        grid=(n // tm,),
        in_specs=[
            pl.BlockSpec((tm, d), lambda i: (i, 0)),
            pl.BlockSpec((tm, TOP_K * d), lambda i: (i, 0)),
            pl.BlockSpec((tm, LANES), lambda i: (i, 0)),
            pl.BlockSpec((1, 6, d), lambda i: (i // per_seq, 0, 0)),
            pl.BlockSpec((1, d), lambda i: (0, 0)),
        ],
        out_specs=pl.BlockSpec((tm, d), lambda i: (i, 0)),
        out_shape=jax.ShapeDtypeStruct((n, d), F32),
        compiler_params=_params("arbitrary"),
    )(x1, y2, route, mod, final_g.reshape(1, d))


def _moe_layout(route, n):
    g = MOE_ROWS
    a = n * TOP_K
    e_flat = route[:, :TOP_K].astype(jnp.int32).reshape(-1)
    onehot = (e_flat[:, None] == jnp.arange(N_EXPERTS, dtype=jnp.int32)[None, :]).astype(jnp.int32)
    csum = jnp.cumsum(onehot, axis=0)
    rank = jnp.take_along_axis(csum, e_flat[:, None], axis=1)[:, 0] - 1
    counts = csum[-1]
    padded = (counts + g - 1) // g * g
    pad_ends = jnp.cumsum(padded)
    pad_starts = pad_ends - padded
    dest = (pad_starts[e_flat] + rank).astype(jnp.int32)
    p_total = (-(-a // g) + N_EXPERTS) * g
    blk_e = jnp.minimum(jnp.searchsorted(pad_ends, jnp.arange(p_total // g, dtype=jnp.int32) * g, side='right'),
                        N_EXPERTS - 1).astype(jnp.int32)
    return dest, blk_e, p_total


def _moe_ffn(h, x1, route, mod, wg, wu, wd, final_g, seq, final):
    n, d = x1.shape
    dest, blk_e, p_total = _moe_layout(route, n)
    token = jnp.arange(n * TOP_K, dtype=jnp.int32) // TOP_K
    slot = jnp.arange(n * TOP_K, dtype=jnp.int32)
    xb = _permute_rows(h, token, dest, p_total, init=jnp.zeros((p_total, d), h.dtype))
    yb = _moe_grouped(xb, blk_e, wg, wu, wd)
    y2 = _permute_rows(yb, dest, slot, n * TOP_K)
    return _combine(x1, y2.reshape(n, TOP_K * d), route, mod, final_g, seq, final)


def kernel(x, c, norm_mix_g, norm_ffn_g, ada_w, ada_b, w_in, w_out, att_out_g, ret_out_g, ffn_w_gate,
           ffn_w_up, ffn_w_down, router_w, moe_w_gate, moe_w_up, moe_w_down, final_norm_g):
    batch, seq, d = x.shape
    depth = ada_w.shape[0]
    assert seq % MOBA_BLOCK == 0 and seq % RET_CHUNK == 0
    assert w_in.shape[2] == N_GROUPS * GROUP_WIDTH
    mods = _adaln(c, ada_w, ada_b).reshape(depth, batch, 6, d)
    tables = _rotary_tables(seq)
    xf = x.reshape(batch * seq, d)
    for l in range(depth):
        mod = mods[l]
        final = l == depth - 1
        proj = _inproj(xf, mod, norm_mix_g[l], w_in[l].astype(BF16), tables, seq)
        o_a = _moba(proj, batch, seq)
        o_r = _retention(proj, ret_out_g[l], batch, seq)
        if l % 2 == 0:
            x1, h = _outproj(o_a, o_r, xf, w_out[l].astype(BF16), att_out_g[l], mod, norm_ffn_g[l], seq)
            xf = _dense_ffn(h, x1, mod, ffn_w_gate[l // 2].astype(BF16), ffn_w_up[l // 2].astype(BF16),
                            ffn_w_down[l // 2].astype(BF16), final_norm_g, seq, final)
        else:
            x1, h, route = _outproj(o_a, o_r, xf, w_out[l].astype(BF16), att_out_g[l], mod, norm_ffn_g[l],
                                    seq, router_w=router_w[l // 2])
            xf = _moe_ffn(h, x1, route, mod, moe_w_gate[l // 2].astype(BF16), moe_w_up[l // 2].astype(BF16),
                          moe_w_down[l // 2].astype(BF16), final_norm_g, seq, final)
    return xf.reshape(batch, seq, d)
```

```python
import functools

import jax
import jax.numpy as jnp
from jax import lax
from jax.experimental import pallas as pl
from jax.experimental.pallas import tpu as pltpu

F32 = jnp.float32
BF16 = jnp.bfloat16

LANES = 128
V7X_VMEM_LIMIT_BYTES = 56 * 1024 * 1024

ATT_HEADS = 8
ATT_HEAD_DIM = 64
ATT_WIDTH = ATT_HEADS * ATT_HEAD_DIM
MOBA_BLOCK = 256
MOBA_TOPK = 3
RET_HEADS = 4
RET_HEAD_DIM = 128
RET_WIDTH = RET_HEADS * RET_HEAD_DIM
RET_CHUNK = 128
ROPE_BASE = 10000.0
N_EXPERTS = 8
TOP_K = 2
EPS = 1e-6
GROUP_WIDTH = 512
N_GROUPS = 7
MOE_ROWS = 512


def _tile(total, target, mult):
    best = None
    t = mult
    while t <= min(total, target):
        if total % t == 0:
            best = t
        t += mult
    return best if best is not None else total


def _params(*sem):
    return pltpu.CompilerParams(dimension_semantics=sem, vmem_limit_bytes=V7X_VMEM_LIMIT_BYTES)


def _rms(x):
    return x * lax.rsqrt(jnp.mean(x * x, axis=-1, keepdims=True) + EPS)


def _silu(x):
    return x * jax.nn.sigmoid(x)


def _adaln_kernel(c_ref, w_ref, b_ref, o_ref):
    cs = _silu(c_ref[...]).astype(BF16)
    o_ref[0] = jnp.dot(cs, w_ref[0].astype(BF16), preferred_element_type=F32) + b_ref[0]


def _adaln(c, ada_w, ada_b):
    depth, d, w = ada_w.shape
    b = c.shape[0]
    tn = _tile(w, 1536, LANES)
    return pl.pallas_call(
        _adaln_kernel,
        grid=(depth, w // tn),
        in_specs=[
            pl.BlockSpec((b, d), lambda l, j: (0, 0)),
            pl.BlockSpec((1, d, tn), lambda l, j: (l, 0, j)),
            pl.BlockSpec((1, 1, tn), lambda l, j: (l, 0, j)),
        ],
        out_specs=pl.BlockSpec((1, b, tn), lambda l, j: (l, 0, j)),
        out_shape=jax.ShapeDtypeStruct((depth, b, w), F32),
        compiler_params=_params("arbitrary", "arbitrary"),
    )(c, ada_w, ada_b.reshape(depth, 1, w))


def _inproj_kernel(x_ref, mod_ref, g_ref, w_ref, cq_ref, sq_ref, ck_ref, sk_ref, o_ref, h_scr):
    j = pl.program_id(1)

    @pl.when(j == 0)
    def _():
        m = mod_ref[0]
        h = _rms(x_ref[...]) * g_ref[...] * (1.0 + m[1:2]) + m[0:1]
        h_scr[...] = h.astype(BF16)

    acc = jnp.dot(h_scr[...], w_ref[...], preferred_element_type=F32)

    def rotate(cos_ref, sin_ref):
        cos = cos_ref[...]
        sin = sin_ref[...]
        for hd in range(RET_HEADS):
            a = acc[:, hd * RET_HEAD_DIM:(hd + 1) * RET_HEAD_DIM]
            r = a * cos + pltpu.roll(a, RET_HEAD_DIM // 2, 1) * sin
            o_ref[:, hd * RET_HEAD_DIM:(hd + 1) * RET_HEAD_DIM] = r.astype(o_ref.dtype)

    @pl.when(j == 3)
    def _():
        rotate(cq_ref, sq_ref)

    @pl.when(j == 4)
    def _():
        rotate(ck_ref, sk_ref)

    @pl.when(jnp.logical_and(j != 3, j != 4))
    def _():
        o_ref[...] = acc.astype(o_ref.dtype)


def _rotary_tables(s):
    half = RET_HEAD_DIM // 2
    inv_freq = ROPE_BASE ** (-jnp.arange(half, dtype=F32) / half)
    ang = jnp.arange(s, dtype=F32)[:, None] * inv_freq[None, :]
    cos = jnp.cos(ang)
    sin = jnp.sin(ang)
    cos2 = jnp.concatenate([cos, cos], axis=-1)
    sin2 = jnp.concatenate([-sin, sin], axis=-1)
    k_scale = RET_HEAD_DIM ** -0.5
    return cos2, sin2, cos2 * k_scale, sin2 * k_scale


def _inproj(x2d, mod, g, w_bf16, tables, seq):
    n, d = x2d.shape
    tm = _tile(seq, 1024, 16)
    per_seq = seq // tm
    return pl.pallas_call(
        _inproj_kernel,
        grid=(n // tm, N_GROUPS),
        in_specs=[
            pl.BlockSpec((tm, d), lambda i, j: (i, 0)),
            pl.BlockSpec((1, 6, d), lambda i, j: (i // per_seq, 0, 0)),
            pl.BlockSpec((1, d), lambda i, j: (0, 0)),
            pl.BlockSpec((d, GROUP_WIDTH), lambda i, j: (0, j)),
        ] + [pl.BlockSpec((tm, RET_HEAD_DIM), lambda i, j: (i % per_seq, 0))] * 4,
        out_specs=pl.BlockSpec((tm, GROUP_WIDTH), lambda i, j: (i, j)),
        out_shape=jax.ShapeDtypeStruct((n, N_GROUPS * GROUP_WIDTH), BF16),
        scratch_shapes=[pltpu.VMEM((tm, d), BF16)],
        compiler_params=_params("arbitrary", "arbitrary"),
    )(x2d, mod, g.reshape(1, d), w_bf16, *tables)


def _moba_kernel(q_ref, k_ref, v_ref, o_ref):
    s = q_ref.shape[0]
    blk = MOBA_BLOCK
    nb = s // blk
    n_sel = min(MOBA_TOPK, nb - 1)
    shift = blk.bit_length() - 1
    scale = ATT_HEAD_DIM ** -0.5
    lane = lax.broadcasted_iota(jnp.int32, (1, LANES), 1)
    k_all = k_ref[...]
    v_all = v_ref[...]
    q_all = q_ref[...]

    r_id = lax.broadcasted_iota(jnp.int32, (LANES, s), 0)
    c_id = lax.broadcasted_iota(jnp.int32, (LANES, s), 1)
    in_blk = jnp.right_shift(c_id, shift) == r_id
    expand = jnp.where(in_blk, 1.0, 0.0).astype(BF16)
    k_mean = jnp.dot(jnp.where(in_blk, 1.0 / blk, 0.0).astype(BF16), k_all,
                     preferred_element_type=F32).astype(BF16)

    heads = []
    for hh in range(2):
        hm = jnp.right_shift(lane, ATT_HEAD_DIM.bit_length() - 1) == hh
        qh = jnp.where(hm, q_all, jnp.zeros_like(q_all))
        gate = lax.dot_general(qh, k_mean, (((1,), (1,)), ((), ())), preferred_element_type=F32)
        heads.append((hm, qh, gate))

    for i in range(nb):
        nk = (i + 1) * blk
        row = lax.broadcasted_iota(jnp.int32, (blk, nk), 0)
        col = lax.broadcasted_iota(jnp.int32, (blk, nk), 1)
        causal = (col - i * blk) <= row
        outs = []
        for hm, qh, gate in heads:
            qi = qh[i * blk:(i + 1) * blk]
            sc = lax.dot_general(qi, k_all[:nk], (((1,), (1,)), ((), ())),
                                 preferred_element_type=F32) * scale
            if i == 0 or n_sel == 0:
                allowed = jnp.logical_and(causal, col >= i * blk)
            else:
                past = col < i * blk
                gm = jnp.where(lane < i, gate[i * blk:(i + 1) * blk], -1e30)
                rank = jnp.zeros((blk, LANES), jnp.int32)
                for jp in range(nb):
                    cv = gm[:, jp:jp + 1]
                    beats = jnp.logical_or(cv > gm, jnp.logical_and(cv == gm, jp < lane))
                    rank = rank + jnp.where(beats, 1, 0)
                sel = jnp.logical_and(rank < n_sel, lane < i)
                selx = jnp.dot(jnp.where(sel, 1.0, 0.0).astype(BF16), expand[:, :nk],
                               preferred_element_type=F32)
                allowed = jnp.logical_or(jnp.logical_and(past, selx > 0.5),
                                         jnp.logical_and(jnp.logical_not(past), causal))
            sc = jnp.where(allowed, sc, -jnp.inf)
            m = jnp.max(sc, axis=-1, keepdims=True)
            p = jnp.exp(sc - m)
            l = jnp.sum(p, axis=-1, keepdims=True)
            o = jnp.dot(p.astype(BF16), v_all[:nk], preferred_element_type=F32) / l
            outs.append(o)
        o_ref[i * blk:(i + 1) * blk, :] = jnp.where(heads[0][0], outs[0], outs[1]).astype(o_ref.dtype)


def _moba(proj, batch, seq):
    n = proj.shape[0]
    pairs = ATT_WIDTH // LANES
    spec = lambda off: pl.BlockSpec((seq, LANES), lambda b, p: (b, off + p))
    return pl.pallas_call(
        _moba_kernel,
        grid=(batch, pairs),
        in_specs=[spec(0), spec(pairs), spec(2 * pairs)],
        out_specs=pl.BlockSpec((seq, LANES), lambda b, p: (b, p)),
        out_shape=jax.ShapeDtypeStruct((n, ATT_WIDTH), BF16),
        compiler_params=_params("arbitrary", "arbitrary"),
    )(proj, proj, proj)


def _retention_kernel(q_ref, k_ref, v_ref, g_ref, dm_ref, qd_ref, kd_ref, cd_ref, og_ref, o_ref):
    s = q_ref.shape[0]
    c = RET_CHUNK
    dmask = dm_ref[0]
    qdec = qd_ref[0]
    kdec = kd_ref[0]
    cdec = cd_ref[0]
    out_g = og_ref[0]
    state = jnp.zeros((RET_HEAD_DIM, RET_HEAD_DIM), F32)
    for n in range(s // c):
        rows = slice(n * c, (n + 1) * c)
        qc = q_ref[rows, :]
        kc = k_ref[rows, :]
        vc = v_ref[rows, :]
        scores = lax.dot_general(qc, kc, (((1,), (1,)), ((), ())), preferred_element_type=F32) * dmask
        inner = jnp.dot(scores.astype(BF16), vc, preferred_element_type=F32)
        cross = jnp.dot(qc, state.astype(BF16), preferred_element_type=F32) * qdec
        o = _rms(inner + cross) * out_g
        gate = g_ref[rows, :].astype(F32)
        o_ref[rows, :] = (_silu(gate) * o).astype(o_ref.dtype)
        k_dec = (kc.astype(F32) * kdec).astype(BF16)
        kv = lax.dot_general(k_dec, vc, (((0,), (0,)), ((), ())), preferred_element_type=F32)
        state = state * cdec + kv


def _retention_tables():
    h, c = RET_HEADS, RET_CHUNK
    lg = jnp.log(1.0 - 2.0 ** (-5.0 - jnp.arange(h, dtype=F32)))
    idx = jnp.arange(c, dtype=F32)
    diff = idx[:, None] - idx[None, :]
    dmask = jnp.where(diff >= 0, jnp.exp(jnp.maximum(diff, 0.0)[None] * lg[:, None, None]), 0.0)
    rows = lambda v: jnp.broadcast_to(v[:, :, None], (h, c, RET_HEAD_DIM))
    qdec = rows(jnp.exp((idx + 1.0)[None, :] * lg[:, None]))
    kdec = rows(jnp.exp((c - 1 - idx)[None, :] * lg[:, None]))
    cdec = jnp.broadcast_to(jnp.exp(c * lg)[:, None, None], (h, RET_HEAD_DIM, RET_HEAD_DIM))
    return dmask, qdec, kdec, cdec


def _retention(proj, ret_out_g, batch, seq):
    n = proj.shape[0]
    base = 3 * ATT_WIDTH // LANES
    spec = lambda off: pl.BlockSpec((seq, LANES), lambda b, h: (b, base + off * RET_HEADS + h))
    table = pl.BlockSpec((1, RET_CHUNK, RET_HEAD_DIM), lambda b, h: (h, 0, 0))
    return pl.pallas_call(
        _retention_kernel,
        grid=(batch, RET_HEADS),
        in_specs=[spec(0), spec(1), spec(2), spec(3), table, table, table, table,
                  pl.BlockSpec((1, 1, RET_HEAD_DIM), lambda b, h: (h, 0, 0))],
        out_specs=pl.BlockSpec((seq, LANES), lambda b, h: (b, h)),
        out_shape=jax.ShapeDtypeStruct((n, RET_WIDTH), BF16),
        compiler_params=_params("arbitrary", "arbitrary"),
    )(proj, proj, proj, proj, *_retention_tables(), ret_out_g.reshape(RET_HEADS, 1, RET_HEAD_DIM))


def _top2_route(logits):
    lane = lax.broadcasted_iota(jnp.int32, logits.shape, 1)
    lg = jnp.where(lane < N_EXPERTS, logits, -jnp.inf)
    m1 = jnp.max(lg, axis=-1, keepdims=True)
    i1 = jnp.min(jnp.where(lg == m1, lane, LANES), axis=-1, keepdims=True)
    lg2 = jnp.where(lane == i1, -jnp.inf, lg)
    m2 = jnp.max(lg2, axis=-1, keepdims=True)
    i2 = jnp.min(jnp.where(lg2 == m2, lane, LANES), axis=-1, keepdims=True)
    e = jnp.exp(m2 - m1)
    w1 = 1.0 / (1.0 + e)
    w2 = e / (1.0 + e)
    out = jnp.where(lane == 0, i1.astype(F32), 0.0)
    out = jnp.where(lane == 1, i2.astype(F32), out)
    out = jnp.where(lane == 2, w1, out)
    return jnp.where(lane == 3, w2, out)


def _outproj_kernel(oa_ref, or_ref, x_ref, w_ref, ag_ref, mod_ref, ng_ref, *rest, route):
    if route:
        rw_ref, x1_ref, h_ref, rt_ref = rest
    else:
        x1_ref, h_ref = rest
    m = mod_ref[0]
    oa = (_rms(oa_ref[...].astype(F32)) * ag_ref[...]).astype(BF16)
    mix = (jnp.dot(oa, w_ref[:ATT_WIDTH, :], preferred_element_type=F32)
           + jnp.dot(or_ref[...], w_ref[ATT_WIDTH:, :], preferred_element_type=F32))
    x1 = x_ref[...] + m[2:3] * mix
    x1_ref[...] = x1
    h = _rms(x1) * ng_ref[...] * (1.0 + m[4:5]) + m[3:4]
    h_ref[...] = h.astype(h_ref.dtype)
    if route:
        logits = jnp.dot(h.astype(BF16), rw_ref[...], preferred_element_type=F32)
        rt_ref[...] = _top2_route(logits)


def _outproj(o_a, o_r, x2d, w_bf16, att_g, mod, norm_g, seq, router_w=None):
    n, d = x2d.shape
    tm = _tile(seq, 512, 16)
    per_seq = seq // tm
    route = router_w is not None
    row = lambda w: pl.BlockSpec((tm, w), lambda i: (i, 0))
    full = lambda a, b: pl.BlockSpec((a, b), lambda i: (0, 0))
    in_specs = [row(ATT_WIDTH), row(RET_WIDTH), row(d), full(ATT_WIDTH + RET_WIDTH, d),
                full(1, ATT_WIDTH), pl.BlockSpec((1, 6, d), lambda i: (i // per_seq, 0, 0)), full(1, d)]
    args = [o_a, o_r, x2d, w_bf16, att_g.reshape(1, ATT_WIDTH), mod, norm_g.reshape(1, d)]
    out_specs = [row(d), row(d)]
    out_shape = [jax.ShapeDtypeStruct((n, d), F32), jax.ShapeDtypeStruct((n, d), F32 if route else BF16)]
    if route:
        rw = jnp.zeros((d, LANES), BF16).at[:, :N_EXPERTS].set(router_w.astype(BF16))
        in_specs.append(full(d, LANES))
        args.append(rw)
        out_specs.append(row(LANES))
        out_shape.append(jax.ShapeDtypeStruct((n, LANES), F32))
    return pl.pallas_call(
        functools.partial(_outproj_kernel, route=route),
        grid=(n // tm,),
        in_specs=in_specs,
        out_specs=out_specs,
        out_shape=out_shape,
        compiler_params=_params("arbitrary"),
    )(*args)


def _swiglu_step(h, wg, wu, wd):
    g = jnp.dot(h, wg, preferred_element_type=F32)
    u = jnp.dot(h, wu, preferred_element_type=F32)
    return jnp.dot((_silu(g) * u).astype(BF16), wd, preferred_element_type=F32)


def _dense_ffn_kernel(h_ref, x1_ref, mod_ref, wg_ref, wu_ref, wd_ref, fg_ref, o_ref, acc_ref, *, final):
    f = pl.program_id(1)
    part = _swiglu_step(h_ref[...], wg_ref[...], wu_ref[...], wd_ref[...])

    @pl.when(f == 0)
    def _():
        acc_ref[...] = part

    @pl.when(f != 0)
    def _():
        acc_ref[...] += part

    @pl.when(f == pl.num_programs(1) - 1)
    def _():
        x2 = x1_ref[...] + mod_ref[0][5:6] * acc_ref[...]
        o_ref[...] = _rms(x2) * fg_ref[...] if final else x2


def _dense_ffn(h, x1, mod, wg, wu, wd, final_g, seq, final):
    n, d = x1.shape
    ff = wg.shape[1]
    tm = _tile(seq, 1024, 16)
    tf = _tile(ff, 256, LANES)
    per_seq = seq // tm
    return pl.pallas_call(
        functools.partial(_dense_ffn_kernel, final=final),
        grid=(n // tm, ff // tf),
        in_specs=[
            pl.BlockSpec((tm, d), lambda i, f: (i, 0)),
            pl.BlockSpec((tm, d), lambda i, f: (i, 0)),
            pl.BlockSpec((1, 6, d), lambda i, f: (i // per_seq, 0, 0)),
            pl.BlockSpec((d, tf), lambda i, f: (0, f)),
            pl.BlockSpec((d, tf), lambda i, f: (0, f)),
            pl.BlockSpec((tf, d), lambda i, f: (f, 0)),
            pl.BlockSpec((1, d), lambda i, f: (0, 0)),
        ],
        out_specs=pl.BlockSpec((tm, d), lambda i, f: (i, 0)),
        out_shape=jax.ShapeDtypeStruct((n, d), F32),
        scratch_shapes=[pltpu.VMEM((tm, d), F32)],
        compiler_params=_params("arbitrary", "arbitrary"),
    )(h, x1, mod, wg, wu, wd, final_g.reshape(1, d))


def _moe_ffn_kernel(blk_e_ref, x_ref, wg_ref, wu_ref, wd_ref, o_ref, xs_ref, acc_ref):
    f = pl.program_id(1)

    @pl.when(f == 0)
    def _():
        xs_ref[...] = x_ref[...].astype(BF16)

    part = _swiglu_step(xs_ref[...], wg_ref[0], wu_ref[0], wd_ref[0])

    @pl.when(f == 0)
    def _():
        acc_ref[...] = part

    @pl.when(f != 0)
    def _():
        acc_ref[...] += part

    @pl.when(f == pl.num_programs(1) - 1)
    def _():
        o_ref[...] = acc_ref[...]


def _moe_grouped(xb, blk_e, wg, wu, wd):
    p, d = xb.shape
    ff = wg.shape[2]
    g = MOE_ROWS
    tf = _tile(ff, 256, LANES)
    grid_spec = pltpu.PrefetchScalarGridSpec(
        num_scalar_prefetch=1,
        grid=(p // g, ff // tf),
        in_specs=[
            pl.BlockSpec((g, d), lambda b, f, be: (b, 0)),
            pl.BlockSpec((1, d, tf), lambda b, f, be: (be[b], 0, f)),
            pl.BlockSpec((1, d, tf), lambda b, f, be: (be[b], 0, f)),
            pl.BlockSpec((1, tf, d), lambda b, f, be: (be[b], f, 0)),
        ],
        out_specs=pl.BlockSpec((g, d), lambda b, f, be: (b, 0)),
        scratch_shapes=[pltpu.VMEM((g, d), BF16), pltpu.VMEM((g, d), F32)],
    )
    return pl.pallas_call(
        _moe_ffn_kernel,
        grid_spec=grid_spec,
        out_shape=jax.ShapeDtypeStruct((p, d), F32),
        compiler_params=_params("arbitrary", "arbitrary"),
    )(blk_e, xb, wg, wu, wd)


def _dispatch_kernel(dest_ref, h_ref, init_ref, xb_ref, sem):
    del init_ref
    tm = h_ref.shape[0]
    base = pl.program_id(0) * (tm * TOP_K)

    def row_copy(t, r):
        return pltpu.make_async_copy(h_ref.at[pl.ds(t, 1)], xb_ref.at[pl.ds(r, 1)], sem)

    def issue(t, carry):
        for k in range(TOP_K):
            row_copy(t, dest_ref[base + TOP_K * t + k]).start()
        return carry

    lax.fori_loop(0, tm, issue, 0, unroll=8)

    def drain(t, carry):
        for _ in range(TOP_K):
            row_copy(0, 0).wait()
        return carry

    lax.fori_loop(0, tm, drain, 0, unroll=8)


def _dispatch(h, dest, p_total, seq):
    n, d = h.shape
    tm = _tile(seq, 512, 8)
    grid_spec = pltpu.PrefetchScalarGridSpec(
        num_scalar_prefetch=1,
        grid=(n // tm,),
        in_specs=[pl.BlockSpec((tm, d), lambda i, dest: (i, 0)), pl.BlockSpec(memory_space=pl.ANY)],
        out_specs=pl.BlockSpec(memory_space=pl.ANY),
        scratch_shapes=[pltpu.SemaphoreType.DMA(())],
    )
    return pl.pallas_call(
        _dispatch_kernel,
        grid_spec=grid_spec,
        out_shape=jax.ShapeDtypeStruct((p_total, d), h.dtype),
        input_output_aliases={2: 0},
        compiler_params=_params("arbitrary"),
    )(dest, h, jnp.zeros((p_total, d), h.dtype))


def _combine_kernel(dest_ref, x1_ref, rt_ref, mod_ref, fg_ref, yb_ref, o_ref, ybuf, sem, *, final):
    tm = x1_ref.shape[0]
    base = pl.program_id(0) * (tm * TOP_K)

    def row_copy(r, j):
        return pltpu.make_async_copy(yb_ref.at[pl.ds(r, 1)], ybuf.at[pl.ds(j, 1)], sem)

    def issue(t, carry):
        for k in range(TOP_K):
            row_copy(dest_ref[base + TOP_K * t + k], k * tm + t).start()
        return carry

    lax.fori_loop(0, tm, issue, 0, unroll=8)

    def drain(t, carry):
        for _ in range(TOP_K):
            row_copy(0, 0).wait()
        return carry

    lax.fori_loop(0, tm, drain, 0, unroll=8)

    rt = rt_ref[...]
    y = rt[:, 2:3] * ybuf[0:tm, :] + rt[:, 3:4] * ybuf[tm:TOP_K * tm, :]
    x2 = x1_ref[...] + mod_ref[0][5:6] * y
    o_ref[...] = _rms(x2) * fg_ref[...] if final else x2


def _combine(x1, yb, dest, route, mod, final_g, seq, final):
    n, d = x1.shape
    tm = _tile(seq, 512, 8)
    per_seq = seq // tm
    grid_spec = pltpu.PrefetchScalarGridSpec(
        num_scalar_prefetch=1,
        grid=(n // tm,),
        in_specs=[
            pl.BlockSpec((tm, d), lambda i, dest: (i, 0)),
            pl.BlockSpec((tm, LANES), lambda i, dest: (i, 0)),
            pl.BlockSpec((1, 6, d), lambda i, dest: (i // per_seq, 0, 0)),
            pl.BlockSpec((1, d), lambda i, dest: (0, 0)),
            pl.BlockSpec(memory_space=pl.ANY),
        ],
        out_specs=pl.BlockSpec((tm, d), lambda i, dest: (i, 0)),
        scratch_shapes=[pltpu.VMEM((TOP_K * tm, d), F32), pltpu.SemaphoreType.DMA(())],
    )
    return pl.pallas_call(
        functools.partial(_combine_kernel, final=final),
        grid_spec=grid_spec,
        out_shape=jax.ShapeDtypeStruct((n, d), F32),
        compiler_params=_params("arbitrary"),
    )(dest, x1, route, mod, final_g.reshape(1, d), yb)


def _moe_layout(route, n):
    g = MOE_ROWS
    a = n * TOP_K
    e_flat = route[:, :TOP_K].astype(jnp.int32).reshape(-1)
    onehot = (e_flat[:, None] == jnp.arange(N_EXPERTS, dtype=jnp.int32)[None, :]).astype(jnp.int32)
    csum = jnp.cumsum(onehot, axis=0)
    rank = jnp.sum(csum * onehot, axis=1) - 1
    counts = csum[-1]
    padded = (counts + g - 1) // g * g
    pad_ends = jnp.cumsum(padded)
    pad_starts = pad_ends - padded
    dest = (jnp.sum(pad_starts[None, :] * onehot, axis=1) + rank).astype(jnp.int32)
    p_total = (-(-a // g) + N_EXPERTS) * g
    blk_start = jnp.arange(p_total // g, dtype=jnp.int32) * g
    blk_e = jnp.minimum(jnp.sum((pad_ends[None, :] <= blk_start[:, None]).astype(jnp.int32), axis=1),
                        N_EXPERTS - 1).astype(jnp.int32)
    return dest, blk_e, p_total


def _moe_ffn(h, x1, route, mod, wg, wu, wd, final_g, seq, final):
    n = x1.shape[0]
    dest, blk_e, p_total = _moe_layout(route, n)
    xb = _dispatch(h, dest, p_total, seq)
    yb = _moe_grouped(xb, blk_e, wg, wu, wd)
    return _combine(x1, yb, dest, route, mod, final_g, seq, final)


def kernel(x, c, norm_mix_g, norm_ffn_g, ada_w, ada_b, w_in, w_out, att_out_g, ret_out_g, ffn_w_gate,
           ffn_w_up, ffn_w_down, router_w, moe_w_gate, moe_w_up, moe_w_down, final_norm_g):
    batch, seq, d = x.shape
    depth = ada_w.shape[0]
    assert seq % MOBA_BLOCK == 0 and seq % RET_CHUNK == 0
    assert w_in.shape[2] == N_GROUPS * GROUP_WIDTH
    mods = _adaln(c, ada_w, ada_b).reshape(depth, batch, 6, d)
    tables = _rotary_tables(seq)
    xf = x.reshape(batch * seq, d)
    for l in range(depth):
        mod = mods[l]
        final = l == depth - 1
        proj = _inproj(xf, mod, norm_mix_g[l], w_in[l].astype(BF16), tables, seq)
        o_a = _moba(proj, batch, seq)
        o_r = _retention(proj, ret_out_g[l], batch, seq)
        if l % 2 == 0:
            x1, h = _outproj(o_a, o_r, xf, w_out[l].astype(BF16), att_out_g[l], mod, norm_ffn_g[l], seq)
            xf = _dense_ffn(h, x1, mod, ffn_w_gate[l // 2].astype(BF16), ffn_w_up[l // 2].astype(BF16),
                            ffn_w_down[l // 2].astype(BF16), final_norm_g, seq, final)
        else:
            x1, h, route = _outproj(o_a, o_r, xf, w_out[l].astype(BF16), att_out_g[l], mod, norm_ffn_g[l],
                                    seq, router_w=router_w[l // 2])
            xf = _moe_ffn(h, x1, route, mod, moe_w_gate[l // 2].astype(BF16), moe_w_up[l // 2].astype(BF16),
                          moe_w_down[l // 2].astype(BF16), final_norm_g, seq, final)
    return xf.reshape(batch, seq, d)
```

```python
import functools

import jax
import jax.numpy as jnp
from jax import lax
from jax.experimental import pallas as pl
from jax.experimental.pallas import tpu as pltpu

F32 = jnp.float32
BF16 = jnp.bfloat16

LANES = 128
V7X_MXU_WIDTH = 256
V7X_VMEM_LIMIT_BYTES = 56 * 1024 * 1024

ATT_HEADS = 8
ATT_HEAD_DIM = 64
ATT_WIDTH = ATT_HEADS * ATT_HEAD_DIM
MOBA_BLOCK = 256
MOBA_TOPK = 3
MOBA_GATE_ROWS = 16
MASK_VALUE = -1e30
RET_HEADS = 4
RET_HEAD_DIM = 128
RET_WIDTH = RET_HEADS * RET_HEAD_DIM
RET_CHUNK = 128
ROPE_BASE = 10000.0
N_EXPERTS = 8
TOP_K = 2
EPS = 1e-6
GROUP_WIDTH = 512
N_GROUPS = 7
FFN_ROWS = 512


def _tile(total, target, mult):
    best = None
    t = mult
    while t <= min(total, target):
        if total % t == 0:
            best = t
        t += mult
    return best if best is not None else total


def _params(*sem):
    return pltpu.CompilerParams(dimension_semantics=sem, vmem_limit_bytes=V7X_VMEM_LIMIT_BYTES)


def _rms(x):
    return x * lax.rsqrt(jnp.mean(x * x, axis=-1, keepdims=True) + EPS)


def _silu(x):
    return x * jax.nn.sigmoid(x)


def _adaln_kernel(c_ref, w_ref, b_ref, o_ref):
    cs = _silu(c_ref[...]).astype(BF16)
    o_ref[0] = jnp.dot(cs, w_ref[0].astype(BF16), preferred_element_type=F32) + b_ref[0]


def _adaln(c, ada_w, ada_b):
    depth, d, w = ada_w.shape
    b = c.shape[0]
    tn = _tile(w, 1536, LANES)
    return pl.pallas_call(
        _adaln_kernel,
        grid=(depth, w // tn),
        in_specs=[
            pl.BlockSpec((b, d), lambda l, j: (0, 0)),
            pl.BlockSpec((1, d, tn), lambda l, j: (l, 0, j)),
            pl.BlockSpec((1, 1, tn), lambda l, j: (l, 0, j)),
        ],
        out_specs=pl.BlockSpec((1, b, tn), lambda l, j: (l, 0, j)),
        out_shape=jax.ShapeDtypeStruct((depth, b, w), F32),
        compiler_params=_params("arbitrary", "arbitrary"),
    )(c, ada_w, ada_b.reshape(depth, 1, w))


def _inproj_kernel(x_ref, mod_ref, g_ref, w_ref, cq_ref, sq_ref, ck_ref, sk_ref, o_ref):
    m = mod_ref[0]
    h = (_rms(x_ref[...]) * g_ref[...] * (1.0 + m[1:2]) + m[0:1]).astype(BF16)
    rotary = {3: (cq_ref, sq_ref), 4: (ck_ref, sk_ref)}
    for j in range(N_GROUPS):
        c0 = j * GROUP_WIDTH
        acc = jnp.dot(h, w_ref[:, c0:c0 + GROUP_WIDTH], preferred_element_type=F32)
        if j in rotary:
            cos = rotary[j][0][...]
            sin = rotary[j][1][...]
            for hd in range(RET_HEADS):
                a = acc[:, hd * RET_HEAD_DIM:(hd + 1) * RET_HEAD_DIM]
                r = a * cos + pltpu.roll(a, RET_HEAD_DIM // 2, 1) * sin
                o_ref[:, c0 + hd * RET_HEAD_DIM:c0 + (hd + 1) * RET_HEAD_DIM] = r.astype(o_ref.dtype)
        else:
            o_ref[:, c0:c0 + GROUP_WIDTH] = acc.astype(o_ref.dtype)


def _rotary_tables(s):
    half = RET_HEAD_DIM // 2
    inv_freq = ROPE_BASE ** (-jnp.arange(half, dtype=F32) / half)
    ang = jnp.arange(s, dtype=F32)[:, None] * inv_freq[None, :]
    cos = jnp.cos(ang)
    sin = jnp.sin(ang)
    cos2 = jnp.concatenate([cos, cos], axis=-1)
    sin2 = jnp.concatenate([-sin, sin], axis=-1)
    k_scale = RET_HEAD_DIM ** -0.5
    return cos2, sin2, cos2 * k_scale, sin2 * k_scale


def _inproj(x2d, mod, g, w_bf16, tables, seq):
    n, d = x2d.shape
    tm = _tile(seq, 1024, 16)
    per_seq = seq // tm
    width = N_GROUPS * GROUP_WIDTH
    return pl.pallas_call(
        _inproj_kernel,
        grid=(n // tm,),
        in_specs=[
            pl.BlockSpec((tm, d), lambda i: (i, 0)),
            pl.BlockSpec((1, 6, d), lambda i: (i // per_seq, 0, 0)),
            pl.BlockSpec((1, d), lambda i: (0, 0)),
            pl.BlockSpec((d, width), lambda i: (0, 0)),
        ] + [pl.BlockSpec((tm, RET_HEAD_DIM), lambda i: (i % per_seq, 0))] * 4,
        out_specs=pl.BlockSpec((tm, width), lambda i: (i, 0)),
        out_shape=jax.ShapeDtypeStruct((n, width), BF16),
        compiler_params=_params("arbitrary"),
    )(x2d, mod, g.reshape(1, d), w_bf16, *tables)


def _moba_kernel(q_ref, k_ref, v_ref, o_ref):
    s = q_ref.shape[0]
    blk = MOBA_BLOCK
    nb = s // blk
    assert nb <= MOBA_GATE_ROWS
    n_sel = min(MOBA_TOPK, nb - 1)
    shift = blk.bit_length() - 1
    scale = ATT_HEAD_DIM ** -0.5
    contract_lanes = (((1,), (1,)), ((), ()))
    lane = lax.broadcasted_iota(jnp.int32, (1, LANES), 1)
    k_all = k_ref[...]
    v_all = v_ref[...]
    q_all = q_ref[...]

    key_pos = lax.broadcasted_iota(jnp.int32, (s, LANES), 0)
    key_lane = lax.broadcasted_iota(jnp.int32, (s, LANES), 1)
    k_aug = jnp.concatenate(
        [k_all, jnp.where(jnp.right_shift(key_pos, shift) == key_lane, 1.0, 0.0).astype(BF16)], axis=1)

    blk_id = lax.broadcasted_iota(jnp.int32, (MOBA_GATE_ROWS, s), 0)
    q_blk = jnp.right_shift(lax.broadcasted_iota(jnp.int32, (MOBA_GATE_ROWS, s), 1), shift)
    past = blk_id < q_blk
    k_mean = jnp.dot(jnp.where(q_blk == blk_id, 1.0 / blk, 0.0).astype(BF16), k_all,
                     preferred_element_type=F32).astype(BF16)
    eye = jnp.where(lax.broadcasted_iota(jnp.int32, (MOBA_GATE_ROWS, LANES), 0)
                    == lax.broadcasted_iota(jnp.int32, (MOBA_GATE_ROWS, LANES), 1), 1.0, 0.0).astype(BF16)
    row = lax.broadcasted_iota(jnp.int32, (blk, blk), 0)
    col = lax.broadcasted_iota(jnp.int32, (blk, blk), 1)
    causal_bias = jnp.where(col <= row, 0.0, MASK_VALUE)

    heads = []
    for hh in range(2):
        hm = jnp.right_shift(lane, ATT_HEAD_DIM.bit_length() - 1) == hh
        qh = jnp.where(hm, q_all, jnp.zeros_like(q_all))
        gate = lax.dot_general(k_mean, qh, contract_lanes, preferred_element_type=F32)
        gm = jnp.where(past, gate, MASK_VALUE)
        rank = jnp.zeros(gm.shape, jnp.int32)
        for jp in range(nb):
            cv = gm[jp:jp + 1, :]
            beats = jnp.logical_or(cv > gm, jnp.logical_and(cv == gm, jp < blk_id))
            rank = rank + jnp.where(beats, 1, 0)
        visible = jnp.logical_or(jnp.logical_and(rank < n_sel, past), blk_id >= q_blk)
        bias_t = jnp.where(visible, 0.0, MASK_VALUE).astype(BF16)
        bias = lax.dot_general(bias_t, eye, (((0,), (0,)), ((), ())), preferred_element_type=F32)
        q_aug = jnp.concatenate([qh * scale, bias.astype(BF16)], axis=1)
        heads.append(q_aug)

    for i in range(nb):
        nk = (i + 1) * blk
        outs = []
        for q_aug in heads:
            sc = lax.dot_general(q_aug[i * blk:(i + 1) * blk], k_aug[:nk], contract_lanes,
                                 preferred_element_type=F32)
            sc_own = sc[:, i * blk:] + causal_bias
            m = jnp.max(sc_own, axis=-1, keepdims=True)
            if i > 0:
                sc_past = sc[:, :i * blk]
                m = jnp.maximum(m, jnp.max(sc_past, axis=-1, keepdims=True))
                p = jnp.concatenate([jnp.exp(sc_past - m), jnp.exp(sc_own - m)], axis=1)
            else:
                p = jnp.exp(sc_own - m)
            l = jnp.sum(p, axis=-1, keepdims=True)
            outs.append(jnp.dot(p.astype(BF16), v_all[:nk], preferred_element_type=F32) / l)
        first_head = jnp.right_shift(lane, ATT_HEAD_DIM.bit_length() - 1) == 0
        o_ref[i * blk:(i + 1) * blk, :] = jnp.where(first_head, outs[0], outs[1]).astype(o_ref.dtype)


def _moba(proj, batch, seq):
    n = proj.shape[0]
    pairs = ATT_WIDTH // LANES
    spec = lambda off: pl.BlockSpec((seq, LANES), lambda b, p: (b, off + p))
    return pl.pallas_call(
        _moba_kernel,
        grid=(batch, pairs),
        in_specs=[spec(0), spec(pairs), spec(2 * pairs)],
        out_specs=pl.BlockSpec((seq, LANES), lambda b, p: (b, p)),
        out_shape=jax.ShapeDtypeStruct((n, ATT_WIDTH), BF16),
        compiler_params=_params("arbitrary", "arbitrary"),
    )(proj, proj, proj)


def _retention_kernel(q_ref, k_ref, v_ref, g_ref, dm_ref, qd_ref, kd_ref, cd_ref, og_ref, o_ref):
    s = q_ref.shape[0]
    c = RET_CHUNK
    dmask = dm_ref[0]
    qdec = qd_ref[0]
    kdec = kd_ref[0]
    cdec = cd_ref[0]
    out_g = og_ref[0]
    state = jnp.zeros((RET_HEAD_DIM, RET_HEAD_DIM), F32)
    for n in range(s // c):
        rows = slice(n * c, (n + 1) * c)
        qc = q_ref[rows, :]
        kc = k_ref[rows, :]
        vc = v_ref[rows, :]
        scores = lax.dot_general(qc, kc, (((1,), (1,)), ((), ())), preferred_element_type=F32) * dmask
        inner = jnp.dot(scores.astype(BF16), vc, preferred_element_type=F32)
        cross = jnp.dot(qc, state.astype(BF16), preferred_element_type=F32) * qdec
        o = _rms(inner + cross) * out_g
        gate = g_ref[rows, :].astype(F32)
        o_ref[rows, :] = (_silu(gate) * o).astype(o_ref.dtype)
        k_dec = (kc.astype(F32) * kdec).astype(BF16)
        kv = lax.dot_general(k_dec, vc, (((0,), (0,)), ((), ())), preferred_element_type=F32)
        state = state * cdec + kv


def _retention_tables():
    h, c = RET_HEADS, RET_CHUNK
    lg = jnp.log(1.0 - 2.0 ** (-5.0 - jnp.arange(h, dtype=F32)))
    idx = jnp.arange(c, dtype=F32)
    diff = idx[:, None] - idx[None, :]
    dmask = jnp.where(diff >= 0, jnp.exp(jnp.maximum(diff, 0.0)[None] * lg[:, None, None]), 0.0)
    rows = lambda v: jnp.broadcast_to(v[:, :, None], (h, c, RET_HEAD_DIM))
    qdec = rows(jnp.exp((idx + 1.0)[None, :] * lg[:, None]))
    kdec = rows(jnp.exp((c - 1 - idx)[None, :] * lg[:, None]))
    cdec = jnp.broadcast_to(jnp.exp(c * lg)[:, None, None], (h, RET_HEAD_DIM, RET_HEAD_DIM))
    return dmask, qdec, kdec, cdec


def _retention(proj, ret_out_g, batch, seq):
    n = proj.shape[0]
    base = 3 * ATT_WIDTH // LANES
    spec = lambda off: pl.BlockSpec((seq, LANES), lambda b, h: (b, base + off * RET_HEADS + h))
    table = pl.BlockSpec((1, RET_CHUNK, RET_HEAD_DIM), lambda b, h: (h, 0, 0))
    return pl.pallas_call(
        _retention_kernel,
        grid=(batch, RET_HEADS),
        in_specs=[spec(0), spec(1), spec(2), spec(3), table, table, table, table,
                  pl.BlockSpec((1, 1, RET_HEAD_DIM), lambda b, h: (h, 0, 0))],
        out_specs=pl.BlockSpec((seq, LANES), lambda b, h: (b, h)),
        out_shape=jax.ShapeDtypeStruct((n, RET_WIDTH), BF16),
        compiler_params=_params("arbitrary", "arbitrary"),
    )(proj, proj, proj, proj, *_retention_tables(), ret_out_g.reshape(RET_HEADS, 1, RET_HEAD_DIM))


def _top2_route(logits):
    lane = lax.broadcasted_iota(jnp.int32, logits.shape, 1)
    lg = jnp.where(lane < N_EXPERTS, logits, -jnp.inf)
    m1 = jnp.max(lg, axis=-1, keepdims=True)
    i1 = jnp.min(jnp.where(lg == m1, lane, LANES), axis=-1, keepdims=True)
    lg2 = jnp.where(lane == i1, -jnp.inf, lg)
    m2 = jnp.max(lg2, axis=-1, keepdims=True)
    i2 = jnp.min(jnp.where(lg2 == m2, lane, LANES), axis=-1, keepdims=True)
    e = jnp.exp(m2 - m1)
    w1 = 1.0 / (1.0 + e)
    w2 = e / (1.0 + e)
    out = jnp.where(lane == 0, i1.astype(F32), 0.0)
    out = jnp.where(lane == 1, i2.astype(F32), out)
    out = jnp.where(lane == 2, w1, out)
    return jnp.where(lane == 3, w2, out)


def _outproj_kernel(oa_ref, or_ref, x_ref, w_ref, ag_ref, mod_ref, ng_ref, *rest, route):
    if route:
        rw_ref, x1_ref, h_ref, rt_ref = rest
    else:
        x1_ref, h_ref = rest
    m = mod_ref[0]
    oa = (_rms(oa_ref[...].astype(F32)) * ag_ref[...]).astype(BF16)
    mix = (jnp.dot(oa, w_ref[:ATT_WIDTH, :], preferred_element_type=F32)
           + jnp.dot(or_ref[...], w_ref[ATT_WIDTH:, :], preferred_element_type=F32))
    x1 = x_ref[...] + m[2:3] * mix
    x1_ref[...] = x1
    h = _rms(x1) * ng_ref[...] * (1.0 + m[4:5]) + m[3:4]
    h_ref[...] = h.astype(h_ref.dtype)
    if route:
        logits = jnp.dot(h.astype(BF16), rw_ref[...], preferred_element_type=F32)
        rt_ref[...] = _top2_route(logits)


def _outproj(o_a, o_r, x2d, w_bf16, att_g, mod, norm_g, seq, router_w=None):
    n, d = x2d.shape
    tm = _tile(seq, 512, 16)
    per_seq = seq // tm
    route = router_w is not None
    row = lambda w: pl.BlockSpec((tm, w), lambda i: (i, 0))
    full = lambda a, b: pl.BlockSpec((a, b), lambda i: (0, 0))
    in_specs = [row(ATT_WIDTH), row(RET_WIDTH), row(d), full(ATT_WIDTH + RET_WIDTH, d),
                full(1, ATT_WIDTH), pl.BlockSpec((1, 6, d), lambda i: (i // per_seq, 0, 0)), full(1, d)]
    args = [o_a, o_r, x2d, w_bf16, att_g.reshape(1, ATT_WIDTH), mod, norm_g.reshape(1, d)]
    out_specs = [row(d), row(d)]
    out_shape = [jax.ShapeDtypeStruct((n, d), F32), jax.ShapeDtypeStruct((n, d), F32 if route else BF16)]
    if route:
        rw = jnp.zeros((d, LANES), BF16).at[:, :N_EXPERTS].set(router_w.astype(BF16))
        in_specs.append(full(d, LANES))
        args.append(rw)
        out_specs.append(row(LANES))
        out_shape.append(jax.ShapeDtypeStruct((n, LANES), F32))
    return pl.pallas_call(
        functools.partial(_outproj_kernel, route=route),
        grid=(n // tm,),
        in_specs=in_specs,
        out_specs=out_specs,
        out_shape=out_shape,
        compiler_params=_params("arbitrary"),
    )(*args)


def _swiglu(h, wg_ref, wu_ref, wd_ref, act_ref):
    ff = act_ref.shape[1]
    tf = _tile(ff, V7X_MXU_WIDTH, LANES)
    for c0 in range(0, ff, tf):
        g = jnp.dot(h, wg_ref[:, c0:c0 + tf], preferred_element_type=F32)
        u = jnp.dot(h, wu_ref[:, c0:c0 + tf], preferred_element_type=F32)
        act_ref[:, c0:c0 + tf] = (_silu(g) * u).astype(act_ref.dtype)
    return jnp.dot(act_ref[...], wd_ref[...], preferred_element_type=F32)


def _dense_ffn_kernel(h_ref, x1_ref, mod_ref, wg_ref, wu_ref, wd_ref, fg_ref, o_ref, act_ref, *, final):
    y = _swiglu(h_ref[...], wg_ref, wu_ref, wd_ref, act_ref)
    x2 = x1_ref[...] + mod_ref[0][5:6] * y
    o_ref[...] = _rms(x2) * fg_ref[...] if final else x2


def _dense_ffn(h, x1, mod, wg, wu, wd, final_g, seq, final):
    n, d = x1.shape
    ff = wg.shape[1]
    tm = _tile(seq, FFN_ROWS, 16)
    per_seq = seq // tm
    row = pl.BlockSpec((tm, d), lambda i: (i, 0))
    return pl.pallas_call(
        functools.partial(_dense_ffn_kernel, final=final),
        grid=(n // tm,),
        in_specs=[
            row, row,
            pl.BlockSpec((1, 6, d), lambda i: (i // per_seq, 0, 0)),
            pl.BlockSpec((d, ff), lambda i: (0, 0)),
            pl.BlockSpec((d, ff), lambda i: (0, 0)),
            pl.BlockSpec((ff, d), lambda i: (0, 0)),
            pl.BlockSpec((1, d), lambda i: (0, 0)),
        ],
        out_specs=row,
        out_shape=jax.ShapeDtypeStruct((n, d), F32),
        scratch_shapes=[pltpu.VMEM((tm, ff), BF16)],
        compiler_params=_params("arbitrary"),
    )(h, x1, mod, wg, wu, wd, final_g.reshape(1, d))


def _moe_ffn_kernel(blk_e_ref, x_ref, wg_ref, wu_ref, wd_ref, o_ref, act_ref):
    del blk_e_ref
    o_ref[...] = _swiglu(x_ref[...].astype(BF16), wg_ref.at[0], wu_ref.at[0], wd_ref.at[0], act_ref)


def _moe_grouped(xb, blk_e, wg, wu, wd):
    p, d = xb.shape
    ff = wg.shape[2]
    g = FFN_ROWS
    grid_spec = pltpu.PrefetchScalarGridSpec(
        num_scalar_prefetch=1,
        grid=(p // g,),
        in_specs=[
            pl.BlockSpec((g, d), lambda b, be: (b, 0)),
            pl.BlockSpec((1, d, ff), lambda b, be: (be[b], 0, 0)),
            pl.BlockSpec((1, d, ff), lambda b, be: (be[b], 0, 0)),
            pl.BlockSpec((1, ff, d), lambda b, be: (be[b], 0, 0)),
        ],
        out_specs=pl.BlockSpec((g, d), lambda b, be: (b, 0)),
        scratch_shapes=[pltpu.VMEM((g, ff), BF16)],
    )
    return pl.pallas_call(
        _moe_ffn_kernel,
        grid_spec=grid_spec,
        out_shape=jax.ShapeDtypeStruct((p, d), F32),
        compiler_params=_params("arbitrary"),
    )(blk_e, xb, wg, wu, wd)


def _dispatch_kernel(dest_ref, h_ref, init_ref, xb_ref, sem):
    del init_ref
    tm = h_ref.shape[0]
    base = pl.program_id(0) * (tm * TOP_K)

    def row_copy(t, r):
        return pltpu.make_async_copy(h_ref.at[pl.ds(t, 1)], xb_ref.at[pl.ds(r, 1)], sem)

    def issue(t, carry):
        for k in range(TOP_K):
            row_copy(t, dest_ref[base + TOP_K * t + k]).start()
        return carry

    lax.fori_loop(0, tm, issue, 0, unroll=8)

    def drain(t, carry):
        for _ in range(TOP_K):
            row_copy(0, 0).wait()
        return carry

    lax.fori_loop(0, tm, drain, 0, unroll=8)


def _dispatch(h, dest, p_total, seq):
    n, d = h.shape
    tm = _tile(seq, 512, 8)
    grid_spec = pltpu.PrefetchScalarGridSpec(
        num_scalar_prefetch=1,
        grid=(n // tm,),
        in_specs=[pl.BlockSpec((tm, d), lambda i, dest: (i, 0)), pl.BlockSpec(memory_space=pl.ANY)],
        out_specs=pl.BlockSpec(memory_space=pl.ANY),
        scratch_shapes=[pltpu.SemaphoreType.DMA(())],
    )
    return pl.pallas_call(
        _dispatch_kernel,
        grid_spec=grid_spec,
        out_shape=jax.ShapeDtypeStruct((p_total, d), h.dtype),
        input_output_aliases={2: 0},
        compiler_params=_params("arbitrary"),
    )(dest, h, jnp.zeros((p_total, d), h.dtype))


def _combine_kernel(dest_ref, x1_ref, rt_ref, mod_ref, fg_ref, yb_ref, o_ref, ybuf, sem, *, final):
    tm = x1_ref.shape[0]
    base = pl.program_id(0) * (tm * TOP_K)

    def row_copy(r, j):
        return pltpu.make_async_copy(yb_ref.at[pl.ds(r, 1)], ybuf.at[pl.ds(j, 1)], sem)

    def issue(t, carry):
        for k in range(TOP_K):
            row_copy(dest_ref[base + TOP_K * t + k], k * tm + t).start()
        return carry

    lax.fori_loop(0, tm, issue, 0, unroll=8)

    def drain(t, carry):
        for _ in range(TOP_K):
            row_copy(0, 0).wait()
        return carry

    lax.fori_loop(0, tm, drain, 0, unroll=8)

    rt = rt_ref[...]
    y = rt[:, 2:3] * ybuf[0:tm, :] + rt[:, 3:4] * ybuf[tm:TOP_K * tm, :]
    x2 = x1_ref[...] + mod_ref[0][5:6] * y
    o_ref[...] = _rms(x2) * fg_ref[...] if final else x2


def _combine(x1, yb, dest, route, mod, final_g, seq, final):
    n, d = x1.shape
    tm = _tile(seq, 512, 8)
    per_seq = seq // tm
    grid_spec = pltpu.PrefetchScalarGridSpec(
        num_scalar_prefetch=1,
        grid=(n // tm,),
        in_specs=[
            pl.BlockSpec((tm, d), lambda i, dest: (i, 0)),
            pl.BlockSpec((tm, LANES), lambda i, dest: (i, 0)),
            pl.BlockSpec((1, 6, d), lambda i, dest: (i // per_seq, 0, 0)),
            pl.BlockSpec((1, d), lambda i, dest: (0, 0)),
            pl.BlockSpec(memory_space=pl.ANY),
        ],
        out_specs=pl.BlockSpec((tm, d), lambda i, dest: (i, 0)),
        scratch_shapes=[pltpu.VMEM((TOP_K * tm, d), F32), pltpu.SemaphoreType.DMA(())],
    )
    return pl.pallas_call(
        functools.partial(_combine_kernel, final=final),
        grid_spec=grid_spec,
        out_shape=jax.ShapeDtypeStruct((n, d), F32),
        compiler_params=_params("arbitrary"),
    )(dest, x1, route, mod, final_g.reshape(1, d), yb)


def _moe_layout(route, n):
    g = FFN_ROWS
    a = n * TOP_K
    e_flat = route[:, :TOP_K].astype(jnp.int32).reshape(-1)
    onehot = (e_flat[:, None] == jnp.arange(N_EXPERTS, dtype=jnp.int32)[None, :]).astype(jnp.int32)
    csum = jnp.cumsum(onehot, axis=0)
    rank = jnp.sum(csum * onehot, axis=1) - 1
    counts = csum[-1]
    padded = (counts + g - 1) // g * g
    pad_ends = jnp.cumsum(padded)
    pad_starts = pad_ends - padded
    dest = (jnp.sum(pad_starts[None, :] * onehot, axis=1) + rank).astype(jnp.int32)
    p_total = (-(-a // g) + N_EXPERTS) * g
    blk_start = jnp.arange(p_total // g, dtype=jnp.int32) * g
    blk_e = jnp.minimum(jnp.sum((pad_ends[None, :] <= blk_start[:, None]).astype(jnp.int32), axis=1),
                        N_EXPERTS - 1).astype(jnp.int32)
    return dest, blk_e, p_total


def _moe_ffn(h, x1, route, mod, wg, wu, wd, final_g, seq, final):
    n = x1.shape[0]
    dest, blk_e, p_total = _moe_layout(route, n)
    xb = _dispatch(h, dest, p_total, seq)
    yb = _moe_grouped(xb, blk_e, wg, wu, wd)
    return _combine(x1, yb, dest, route, mod, final_g, seq, final)


def kernel(x, c, norm_mix_g, norm_ffn_g, ada_w, ada_b, w_in, w_out, att_out_g, ret_out_g, ffn_w_gate,
           ffn_w_up, ffn_w_down, router_w, moe_w_gate, moe_w_up, moe_w_down, final_norm_g):
    batch, seq, d = x.shape
    depth = ada_w.shape[0]
    assert seq % MOBA_BLOCK == 0 and seq % RET_CHUNK == 0
    assert w_in.shape[2] == N_GROUPS * GROUP_WIDTH
    mods = _adaln(c, ada_w, ada_b).reshape(depth, batch, 6, d)
    tables = _rotary_tables(seq)
    xf = x.reshape(batch * seq, d)
    for l in range(depth):
        mod = mods[l]
        final = l == depth - 1
        proj = _inproj(xf, mod, norm_mix_g[l], w_in[l].astype(BF16), tables, seq)
        o_a = _moba(proj, batch, seq)
        o_r = _retention(proj, ret_out_g[l], batch, seq)
        if l % 2 == 0:
            x1, h = _outproj(o_a, o_r, xf, w_out[l].astype(BF16), att_out_g[l], mod, norm_ffn_g[l], seq)
            xf = _dense_ffn(h, x1, mod, ffn_w_gate[l // 2].astype(BF16), ffn_w_up[l // 2].astype(BF16),
                            ffn_w_down[l // 2].astype(BF16), final_norm_g, seq, final)
        else:
            x1, h, route = _outproj(o_a, o_r, xf, w_out[l].astype(BF16), att_out_g[l], mod, norm_ffn_g[l],
                                    seq, router_w=router_w[l // 2])
            xf = _moe_ffn(h, x1, route, mod, moe_w_gate[l // 2].astype(BF16), moe_w_up[l // 2].astype(BF16),
                          moe_w_down[l // 2].astype(BF16), final_norm_g, seq, final)
    return xf.reshape(batch, seq, d)
```

```python
import functools

import jax
import jax.numpy as jnp
from jax import lax
from jax.experimental import pallas as pl
from jax.experimental.pallas import tpu as pltpu

F32 = jnp.float32
BF16 = jnp.bfloat16

LANES = 128
V7X_MXU_WIDTH = 256
V7X_VMEM_LIMIT_BYTES = 56 * 1024 * 1024

ATT_HEADS = 8
ATT_HEAD_DIM = 64
ATT_WIDTH = ATT_HEADS * ATT_HEAD_DIM
MOBA_BLOCK = 256
MOBA_TOPK = 3
MOBA_PAIRS_PER_STEP = 2
MOBA_GATE_ROWS = 16
MASK_VALUE = -1e30
MOBA_Q_SCALE = ATT_HEAD_DIM ** -0.5 * 1.4426950408889634
RET_HEADS = 4
RET_HEAD_DIM = 128
RET_WIDTH = RET_HEADS * RET_HEAD_DIM
RET_CHUNK = 128
ROPE_BASE = 10000.0
N_EXPERTS = 8
TOP_K = 2
EPS = 1e-6
GROUP_WIDTH = 512
N_GROUPS = 7
FFN_ROWS = 512


def _tile(total, target, mult):
    best = None
    t = mult
    while t <= min(total, target):
        if total % t == 0:
            best = t
        t += mult
    return best if best is not None else total


def _params(*sem):
    return pltpu.CompilerParams(dimension_semantics=sem, vmem_limit_bytes=V7X_VMEM_LIMIT_BYTES)


def _rms(x):
    return x * lax.rsqrt(jnp.mean(x * x, axis=-1, keepdims=True) + EPS)


def _silu(x):
    return x * jax.nn.sigmoid(x)


def _store_token_tiles(ref, x, row0=0):
    rows, d = x.shape
    sub = d // LANES
    for s in range(sub):
        ref[pl.ds(row0 * sub + s, rows, stride=sub), :] = x[:, s * LANES:(s + 1) * LANES].astype(ref.dtype)


def _load_token_tiles(ref, rows, d, row0=0):
    sub = d // LANES
    return jnp.concatenate([ref[pl.ds(row0 * sub + s, rows, stride=sub), :] for s in range(sub)], axis=1)


def _token_tile(ref, t, sub):
    return ref.at[pl.ds(pl.multiple_of(t * sub, sub), sub)]


def _adaln_kernel(c_ref, w_ref, b_ref, o_ref):
    cs = _silu(c_ref[...]).astype(BF16)
    o_ref[0] = jnp.dot(cs, w_ref[0].astype(BF16), preferred_element_type=F32) + b_ref[0]


def _adaln(c, ada_w, ada_b):
    depth, d, w = ada_w.shape
    b = c.shape[0]
    tn = _tile(w, 1536, LANES)
    return pl.pallas_call(
        _adaln_kernel,
        grid=(depth, w // tn),
        in_specs=[
            pl.BlockSpec((b, d), lambda l, j: (0, 0)),
            pl.BlockSpec((1, d, tn), lambda l, j: (l, 0, j)),
            pl.BlockSpec((1, 1, tn), lambda l, j: (l, 0, j)),
        ],
        out_specs=pl.BlockSpec((1, b, tn), lambda l, j: (l, 0, j)),
        out_shape=jax.ShapeDtypeStruct((depth, b, w), F32),
        compiler_params=_params("arbitrary", "arbitrary"),
    )(c, ada_w, ada_b.reshape(depth, 1, w))


def _inproj_kernel(x_ref, mod_ref, g_ref, w_ref, cq_ref, sq_ref, ck_ref, sk_ref, o_ref):
    m = mod_ref[0]
    h = (_rms(x_ref[...]) * g_ref[...] * (1.0 + m[1:2]) + m[0:1]).astype(BF16)
    rotary = {3: (cq_ref, sq_ref), 4: (ck_ref, sk_ref)}
    for j in range(N_GROUPS):
        c0 = j * GROUP_WIDTH
        acc = jnp.dot(h, w_ref[:, c0:c0 + GROUP_WIDTH], preferred_element_type=F32)
        if j in rotary:
            cos = rotary[j][0][...]
            sin = rotary[j][1][...]
            for hd in range(RET_HEADS):
                a = acc[:, hd * RET_HEAD_DIM:(hd + 1) * RET_HEAD_DIM]
                r = a * cos + pltpu.roll(a, RET_HEAD_DIM // 2, 1) * sin
                o_ref[:, c0 + hd * RET_HEAD_DIM:c0 + (hd + 1) * RET_HEAD_DIM] = r.astype(o_ref.dtype)
        elif j == 0:
            o_ref[:, c0:c0 + GROUP_WIDTH] = (acc * MOBA_Q_SCALE).astype(o_ref.dtype)
        else:
            o_ref[:, c0:c0 + GROUP_WIDTH] = acc.astype(o_ref.dtype)


def _rotary_tables(s):
    half = RET_HEAD_DIM // 2
    inv_freq = ROPE_BASE ** (-jnp.arange(half, dtype=F32) / half)
    ang = jnp.arange(s, dtype=F32)[:, None] * inv_freq[None, :]
    cos = jnp.cos(ang)
    sin = jnp.sin(ang)
    cos2 = jnp.concatenate([cos, cos], axis=-1)
    sin2 = jnp.concatenate([-sin, sin], axis=-1)
    k_scale = RET_HEAD_DIM ** -0.5
    return cos2, sin2, cos2 * k_scale, sin2 * k_scale


def _inproj(x2d, mod, g, w_bf16, tables, seq):
    n, d = x2d.shape
    tm = _tile(seq, 1024, 16)
    per_seq = seq // tm
    width = N_GROUPS * GROUP_WIDTH
    return pl.pallas_call(
        _inproj_kernel,
        grid=(n // tm,),
        in_specs=[
            pl.BlockSpec((tm, d), lambda i: (i, 0)),
            pl.BlockSpec((1, 6, d), lambda i: (i // per_seq, 0, 0)),
            pl.BlockSpec((1, d), lambda i: (0, 0)),
            pl.BlockSpec((d, width), lambda i: (0, 0)),
        ] + [pl.BlockSpec((tm, RET_HEAD_DIM), lambda i: (i % per_seq, 0))] * 4,
        out_specs=pl.BlockSpec((tm, width), lambda i: (i, 0)),
        out_shape=jax.ShapeDtypeStruct((n, width), BF16),
        compiler_params=_params("arbitrary"),
    )(x2d, mod, g.reshape(1, d), w_bf16, *tables)


def _moba_kernel(q_ref, k_ref, v_ref, o_ref):
    s = q_ref.shape[0]
    blk = MOBA_BLOCK
    nb = s // blk
    assert nb <= MOBA_GATE_ROWS
    n_sel = min(MOBA_TOPK, nb - 1)
    shift = blk.bit_length() - 1
    contract_lanes = (((1,), (1,)), ((), ()))
    lane = lax.broadcasted_iota(jnp.int32, (1, LANES), 1)
    head_of_lane = jnp.right_shift(lane, ATT_HEAD_DIM.bit_length() - 1)

    key_pos = lax.broadcasted_iota(jnp.int32, (s, LANES), 0)
    key_lane = lax.broadcasted_iota(jnp.int32, (s, LANES), 1)
    key_block_onehot = jnp.where(jnp.right_shift(key_pos, shift) == key_lane, 1.0, 0.0).astype(BF16)
    ones = jnp.ones((s, LANES), BF16)

    blk_id = lax.broadcasted_iota(jnp.int32, (MOBA_GATE_ROWS, s), 0)
    q_blk = jnp.right_shift(lax.broadcasted_iota(jnp.int32, (MOBA_GATE_ROWS, s), 1), shift)
    past = blk_id < q_blk
    block_mean = jnp.where(q_blk == blk_id, 1.0 / blk, 0.0).astype(BF16)
    eye = jnp.where(lax.broadcasted_iota(jnp.int32, (MOBA_GATE_ROWS, LANES), 0)
                    == lax.broadcasted_iota(jnp.int32, (MOBA_GATE_ROWS, LANES), 1), 1.0, 0.0).astype(BF16)
    row = lax.broadcasted_iota(jnp.int32, (2 * blk, blk), 0)
    col = lax.broadcasted_iota(jnp.int32, (2 * blk, blk), 1)
    causal_bias = jnp.where(col <= jnp.bitwise_and(row, blk - 1), 0.0, MASK_VALUE)

    for pair in range(q_ref.shape[1] // LANES):
        cols = slice(pair * LANES, (pair + 1) * LANES)
        q_all = q_ref[:, cols]
        k_all = k_ref[:, cols]
        k_aug = jnp.concatenate([k_all, key_block_onehot], axis=1)
        v_aug = jnp.concatenate([v_ref[:, cols], ones], axis=1)
        k_mean = jnp.dot(block_mean, k_all, preferred_element_type=F32).astype(BF16)

        heads = []
        for hh in range(2):
            qh = jnp.where(head_of_lane == hh, q_all, jnp.zeros_like(q_all))
            gate = lax.dot_general(k_mean, qh, contract_lanes, preferred_element_type=F32)
            gm = jnp.where(past, gate, MASK_VALUE)
            rank = jnp.zeros(gm.shape, jnp.int32)
            for jp in range(nb):
                cv = gm[jp:jp + 1, :]
                beats = jnp.logical_or(cv > gm, jnp.logical_and(cv == gm, jp < blk_id))
                rank = rank + jnp.where(beats, 1, 0)
            visible = jnp.logical_or(jnp.logical_and(rank < n_sel, past), blk_id >= q_blk)
            bias_t = jnp.where(visible, 0.0, MASK_VALUE).astype(BF16)
            bias = lax.dot_general(bias_t, eye, (((0,), (0,)), ((), ())), preferred_element_type=F32)
            heads.append(jnp.concatenate([qh, bias.astype(BF16)], axis=1))

        for i in range(nb):
            nk = (i + 1) * blk
            rows = slice(i * blk, (i + 1) * blk)
            q_both = jnp.concatenate([heads[0][rows], heads[1][rows]], axis=0)
            sc = lax.dot_general(q_both, k_aug[:nk], contract_lanes, preferred_element_type=F32)
            sc_own = sc[:, i * blk:] + causal_bias
            m = jnp.max(sc_own, axis=-1, keepdims=True)
            if i > 0:
                sc_past = sc[:, :i * blk]
                m = jnp.maximum(m, jnp.max(sc_past, axis=-1, keepdims=True))
                p = jnp.concatenate([jnp.exp2(sc_past - m), jnp.exp2(sc_own - m)], axis=1)
            else:
                p = jnp.exp2(sc_own - m)
            o = jnp.dot(p.astype(BF16), v_aug[:nk], preferred_element_type=F32)
            o = o[:, :LANES] / o[:, LANES:LANES + 1]
            o_ref[rows, cols] = jnp.where(head_of_lane == 0, o[:blk], o[blk:]).astype(o_ref.dtype)


def _moba(proj, batch, seq):
    n = proj.shape[0]
    width = MOBA_PAIRS_PER_STEP * LANES
    steps = ATT_WIDTH // width
    spec = lambda off: pl.BlockSpec((seq, width), lambda b, p: (b, off + p))
    return pl.pallas_call(
        _moba_kernel,
        grid=(batch, steps),
        in_specs=[spec(0), spec(steps), spec(2 * steps)],
        out_specs=pl.BlockSpec((seq, width), lambda b, p: (b, p)),
        out_shape=jax.ShapeDtypeStruct((n, ATT_WIDTH), BF16),
        compiler_params=_params("arbitrary", "arbitrary"),
    )(proj, proj, proj)


def _retention_kernel(q_ref, k_ref, v_ref, g_ref, dm_ref, qd_ref, kd_ref, cd_ref, og_ref, o_ref):
    s = q_ref.shape[0]
    c = RET_CHUNK
    states = [jnp.zeros((RET_HEAD_DIM, RET_HEAD_DIM), F32) for _ in range(RET_HEADS)]
    for n in range(s // c):
        rows = slice(n * c, (n + 1) * c)
        for hd in range(RET_HEADS):
            cols = slice(hd * RET_HEAD_DIM, (hd + 1) * RET_HEAD_DIM)
            qc = q_ref[rows, cols]
            kc = k_ref[rows, cols]
            vc = v_ref[rows, cols]
            scores = lax.dot_general(qc, kc, (((1,), (1,)), ((), ())), preferred_element_type=F32) * dm_ref[hd]
            inner = jnp.dot(scores.astype(BF16), vc, preferred_element_type=F32)
            cross = jnp.dot(qc, states[hd].astype(BF16), preferred_element_type=F32) * qd_ref[hd]
            o = _rms(inner + cross) * og_ref[hd]
            gate = g_ref[rows, cols].astype(F32)
            o_ref[rows, cols] = (_silu(gate) * o).astype(o_ref.dtype)
            k_dec = (kc.astype(F32) * kd_ref[hd]).astype(BF16)
            kv = lax.dot_general(k_dec, vc, (((0,), (0,)), ((), ())), preferred_element_type=F32)
            states[hd] = states[hd] * cd_ref[hd] + kv


def _retention_tables():
    h, c = RET_HEADS, RET_CHUNK
    lg = jnp.log(1.0 - 2.0 ** (-5.0 - jnp.arange(h, dtype=F32)))
    idx = jnp.arange(c, dtype=F32)
    diff = idx[:, None] - idx[None, :]
    dmask = jnp.where(diff >= 0, jnp.exp(jnp.maximum(diff, 0.0)[None] * lg[:, None, None]), 0.0)
    rows = lambda v: jnp.broadcast_to(v[:, :, None], (h, c, RET_HEAD_DIM))
    qdec = rows(jnp.exp((idx + 1.0)[None, :] * lg[:, None]))
    kdec = rows(jnp.exp((c - 1 - idx)[None, :] * lg[:, None]))
    cdec = jnp.broadcast_to(jnp.exp(c * lg)[:, None, None], (h, RET_HEAD_DIM, RET_HEAD_DIM))
    return dmask, qdec, kdec, cdec


def _retention(proj, ret_out_g, batch, seq):
    n = proj.shape[0]
    base = 3 * ATT_WIDTH // RET_WIDTH
    spec = lambda off: pl.BlockSpec((seq, RET_WIDTH), lambda b: (b, base + off))
    table = pl.BlockSpec((RET_HEADS, RET_CHUNK, RET_HEAD_DIM), lambda b: (0, 0, 0))
    return pl.pallas_call(
        _retention_kernel,
        grid=(batch,),
        in_specs=[spec(0), spec(1), spec(2), spec(3), table, table, table, table,
                  pl.BlockSpec((RET_HEADS, 1, RET_HEAD_DIM), lambda b: (0, 0, 0))],
        out_specs=pl.BlockSpec((seq, RET_WIDTH), lambda b: (b, 0)),
        out_shape=jax.ShapeDtypeStruct((n, RET_WIDTH), BF16),
        compiler_params=_params("arbitrary"),
    )(proj, proj, proj, proj, *_retention_tables(), ret_out_g.reshape(RET_HEADS, 1, RET_HEAD_DIM))


def _top2_route(logits):
    lane = lax.broadcasted_iota(jnp.int32, logits.shape, 1)
    lg = jnp.where(lane < N_EXPERTS, logits, -jnp.inf)
    m1 = jnp.max(lg, axis=-1, keepdims=True)
    i1 = jnp.min(jnp.where(lg == m1, lane, LANES), axis=-1, keepdims=True)
    lg2 = jnp.where(lane == i1, -jnp.inf, lg)
    m2 = jnp.max(lg2, axis=-1, keepdims=True)
    i2 = jnp.min(jnp.where(lg2 == m2, lane, LANES), axis=-1, keepdims=True)
    e = jnp.exp(m2 - m1)
    w1 = 1.0 / (1.0 + e)
    w2 = e / (1.0 + e)
    out = jnp.where(lane == 0, i1.astype(F32), 0.0)
    out = jnp.where(lane == 1, i2.astype(F32), out)
    out = jnp.where(lane == 2, w1, out)
    return jnp.where(lane == 3, w2, out)


def _outproj_kernel(oa_ref, or_ref, x_ref, w_ref, ag_ref, mod_ref, ng_ref, *rest, route):
    if route:
        rw_ref, x1_ref, h_ref, rt_ref = rest
    else:
        x1_ref, h_ref = rest
    m = mod_ref[0]
    oa = (_rms(oa_ref[...].astype(F32)) * ag_ref[...]).astype(BF16)
    mix = (jnp.dot(oa, w_ref[:ATT_WIDTH, :], preferred_element_type=F32)
           + jnp.dot(or_ref[...], w_ref[ATT_WIDTH:, :], preferred_element_type=F32))
    x1 = x_ref[...] + m[2:3] * mix
    x1_ref[...] = x1
    h = _rms(x1) * ng_ref[...] * (1.0 + m[4:5]) + m[3:4]
    if route:
        _store_token_tiles(h_ref, h)
        logits = jnp.dot(h.astype(BF16), rw_ref[...], preferred_element_type=F32)
        rt_ref[...] = _top2_route(logits)
    else:
        h_ref[...] = h.astype(h_ref.dtype)


def _outproj(o_a, o_r, x2d, w_bf16, att_g, mod, norm_g, seq, router_w=None):
    n, d = x2d.shape
    tm = _tile(seq, 512, 16)
    per_seq = seq // tm
    route = router_w is not None
    row = lambda w: pl.BlockSpec((tm, w), lambda i: (i, 0))
    full = lambda a, b: pl.BlockSpec((a, b), lambda i: (0, 0))
    in_specs = [row(ATT_WIDTH), row(RET_WIDTH), row(d), full(ATT_WIDTH + RET_WIDTH, d),
                full(1, ATT_WIDTH), pl.BlockSpec((1, 6, d), lambda i: (i // per_seq, 0, 0)), full(1, d)]
    args = [o_a, o_r, x2d, w_bf16, att_g.reshape(1, ATT_WIDTH), mod, norm_g.reshape(1, d)]
    out_specs = [row(d), row(d)]
    out_shape = [jax.ShapeDtypeStruct((n, d), F32), jax.ShapeDtypeStruct((n, d), BF16)]
    if route:
        sub = d // LANES
        out_specs[1] = pl.BlockSpec((tm * sub, LANES), lambda i: (i, 0))
        out_shape[1] = jax.ShapeDtypeStruct((n * sub, LANES), F32)
        rw = jnp.zeros((d, LANES), BF16).at[:, :N_EXPERTS].set(router_w.astype(BF16))
        in_specs.append(full(d, LANES))
        args.append(rw)
        out_specs.append(row(LANES))
        out_shape.append(jax.ShapeDtypeStruct((n, LANES), F32))
    return pl.pallas_call(
        functools.partial(_outproj_kernel, route=route),
        grid=(n // tm,),
        in_specs=in_specs,
        out_specs=out_specs,
        out_shape=out_shape,
        compiler_params=_params("arbitrary"),
    )(*args)


def _swiglu(h, wg_ref, wu_ref, wd_ref, act_ref):
    ff = act_ref.shape[1]
    tf = _tile(ff, V7X_MXU_WIDTH, LANES)
    for c0 in range(0, ff, tf):
        g = jnp.dot(h, wg_ref[:, c0:c0 + tf], preferred_element_type=F32)
        u = jnp.dot(h, wu_ref[:, c0:c0 + tf], preferred_element_type=F32)
        act_ref[:, c0:c0 + tf] = (_silu(g) * u).astype(act_ref.dtype)
    return jnp.dot(act_ref[...], wd_ref[...], preferred_element_type=F32)


def _dense_ffn_kernel(h_ref, x1_ref, mod_ref, wg_ref, wu_ref, wd_ref, fg_ref, o_ref, act_ref, *, final):
    y = _swiglu(h_ref[...], wg_ref, wu_ref, wd_ref, act_ref)
    x2 = x1_ref[...] + mod_ref[0][5:6] * y
    o_ref[...] = _rms(x2) * fg_ref[...] if final else x2


def _dense_ffn(h, x1, mod, wg, wu, wd, final_g, seq, final):
    n, d = x1.shape
    ff = wg.shape[1]
    tm = _tile(seq, FFN_ROWS, 16)
    per_seq = seq // tm
    row = pl.BlockSpec((tm, d), lambda i: (i, 0))
    return pl.pallas_call(
        functools.partial(_dense_ffn_kernel, final=final),
        grid=(n // tm,),
        in_specs=[
            row, row,
            pl.BlockSpec((1, 6, d), lambda i: (i // per_seq, 0, 0)),
            pl.BlockSpec((d, ff), lambda i: (0, 0)),
            pl.BlockSpec((d, ff), lambda i: (0, 0)),
            pl.BlockSpec((ff, d), lambda i: (0, 0)),
            pl.BlockSpec((1, d), lambda i: (0, 0)),
        ],
        out_specs=row,
        out_shape=jax.ShapeDtypeStruct((n, d), F32),
        scratch_shapes=[pltpu.VMEM((tm, ff), BF16)],
        compiler_params=_params("arbitrary"),
    )(h, x1, mod, wg, wu, wd, final_g.reshape(1, d))


def _moe_ffn_kernel(blk_e_ref, n_used_ref, x_ref, wg_ref, wu_ref, wd_ref, o_ref, xs_ref, act_ref):
    del blk_e_ref
    g, d = xs_ref.shape

    @pl.when(pl.program_id(0) < n_used_ref[0])
    def _():
        xs_ref[...] = _load_token_tiles(x_ref, g, d).astype(BF16)
        _store_token_tiles(o_ref, _swiglu(xs_ref[...], wg_ref.at[0], wu_ref.at[0], wd_ref.at[0], act_ref))

    @pl.when(pl.program_id(0) >= n_used_ref[0])
    def _():
        o_ref[...] = jnp.zeros_like(o_ref)


def _moe_grouped(xb, blk_e, n_used, wg, wu, wd):
    _, d, ff = wg.shape
    sub = d // LANES
    g = FFN_ROWS
    grid_spec = pltpu.PrefetchScalarGridSpec(
        num_scalar_prefetch=2,
        grid=(xb.shape[0] // (g * sub),),
        in_specs=[
            pl.BlockSpec((g * sub, LANES), lambda b, be, nu: (b, 0)),
            pl.BlockSpec((1, d, ff), lambda b, be, nu: (be[b], 0, 0)),
            pl.BlockSpec((1, d, ff), lambda b, be, nu: (be[b], 0, 0)),
            pl.BlockSpec((1, ff, d), lambda b, be, nu: (be[b], 0, 0)),
        ],
        out_specs=pl.BlockSpec((g * sub, LANES), lambda b, be, nu: (b, 0)),
        scratch_shapes=[pltpu.VMEM((g, d), BF16), pltpu.VMEM((g, ff), BF16)],
    )
    return pl.pallas_call(
        _moe_ffn_kernel,
        grid_spec=grid_spec,
        out_shape=jax.ShapeDtypeStruct(xb.shape, F32),
        compiler_params=_params("arbitrary"),
    )(blk_e, n_used, xb, wg, wu, wd)


def _dispatch_kernel(dest_ref, h_ref, init_ref, xb_ref, sem, *, sub):
    del init_ref
    tm = h_ref.shape[0] // sub
    base = pl.program_id(0) * (tm * TOP_K)

    def row_copy(t, r):
        return pltpu.make_async_copy(_token_tile(h_ref, t, sub), _token_tile(xb_ref, r, sub), sem)

    def issue(t, carry):
        for k in range(TOP_K):
            row_copy(t, dest_ref[base + TOP_K * t + k]).start()
        return carry

    lax.fori_loop(0, tm, issue, 0, unroll=8)

    def drain(t, carry):
        for _ in range(TOP_K):
            row_copy(0, 0).wait()
        return carry

    lax.fori_loop(0, tm, drain, 0, unroll=8)


def _dispatch(h_tiles, dest, p_total, seq, d):
    sub = d // LANES
    n = h_tiles.shape[0] // sub
    tm = _tile(seq, 512, 8)
    grid_spec = pltpu.PrefetchScalarGridSpec(
        num_scalar_prefetch=1,
        grid=(n // tm,),
        in_specs=[pl.BlockSpec((tm * sub, LANES), lambda i, dest: (i, 0)), pl.BlockSpec(memory_space=pl.ANY)],
        out_specs=pl.BlockSpec(memory_space=pl.ANY),
        scratch_shapes=[pltpu.SemaphoreType.DMA(())],
    )
    return pl.pallas_call(
        functools.partial(_dispatch_kernel, sub=sub),
        grid_spec=grid_spec,
        out_shape=jax.ShapeDtypeStruct((p_total * sub, LANES), h_tiles.dtype),
        input_output_aliases={2: 0},
        compiler_params=_params("arbitrary"),
    )(dest, h_tiles, jnp.zeros((p_total * sub, LANES), h_tiles.dtype))


def _combine_kernel(dest_ref, x1_ref, rt_ref, mod_ref, fg_ref, yb_ref, o_ref, ybuf, sem, *, final):
    tm, d = x1_ref.shape
    sub = d // LANES
    base = pl.program_id(0) * (tm * TOP_K)

    def row_copy(r, j):
        return pltpu.make_async_copy(_token_tile(yb_ref, r, sub), _token_tile(ybuf, j, sub), sem)

    def issue(t, carry):
        for k in range(TOP_K):
            row_copy(dest_ref[base + TOP_K * t + k], k * tm + t).start()
        return carry

    lax.fori_loop(0, tm, issue, 0, unroll=8)

    def drain(t, carry):
        for _ in range(TOP_K):
            row_copy(0, 0).wait()
        return carry

    lax.fori_loop(0, tm, drain, 0, unroll=8)

    rt = rt_ref[...]
    y = (rt[:, 2:3] * _load_token_tiles(ybuf, tm, d) + rt[:, 3:4] * _load_token_tiles(ybuf, tm, d, row0=tm))
    x2 = x1_ref[...] + mod_ref[0][5:6] * y
    o_ref[...] = _rms(x2) * fg_ref[...] if final else x2


def _combine(x1, yb, dest, route, mod, final_g, seq, final):
    n, d = x1.shape
    tm = _tile(seq, 512, 8)
    per_seq = seq // tm
    grid_spec = pltpu.PrefetchScalarGridSpec(
        num_scalar_prefetch=1,
        grid=(n // tm,),
        in_specs=[
            pl.BlockSpec((tm, d), lambda i, dest: (i, 0)),
            pl.BlockSpec((tm, LANES), lambda i, dest: (i, 0)),
            pl.BlockSpec((1, 6, d), lambda i, dest: (i // per_seq, 0, 0)),
            pl.BlockSpec((1, d), lambda i, dest: (0, 0)),
            pl.BlockSpec(memory_space=pl.ANY),
        ],
        out_specs=pl.BlockSpec((tm, d), lambda i, dest: (i, 0)),
        scratch_shapes=[pltpu.VMEM((TOP_K * tm * (d // LANES), LANES), F32), pltpu.SemaphoreType.DMA(())],
    )
    return pl.pallas_call(
        functools.partial(_combine_kernel, final=final),
        grid_spec=grid_spec,
        out_shape=jax.ShapeDtypeStruct((n, d), F32),
        compiler_params=_params("arbitrary"),
    )(dest, x1, route, mod, final_g.reshape(1, d), yb)


def _moe_layout(route, n):
    g = FFN_ROWS
    a = n * TOP_K
    e_flat = route[:, :TOP_K].astype(jnp.int32).reshape(-1)
    onehot = (e_flat[:, None] == jnp.arange(N_EXPERTS, dtype=jnp.int32)[None, :]).astype(jnp.int32)
    csum = jnp.cumsum(onehot, axis=0)
    rank = jnp.sum(csum * onehot, axis=1) - 1
    counts = csum[-1]
    padded = (counts + g - 1) // g * g
    pad_ends = jnp.cumsum(padded)
    pad_starts = pad_ends - padded
    dest = (jnp.sum(pad_starts[None, :] * onehot, axis=1) + rank).astype(jnp.int32)
    p_total = (-(-a // g) + N_EXPERTS) * g
    blk_start = jnp.arange(p_total // g, dtype=jnp.int32) * g
    blk_e = jnp.minimum(jnp.sum((pad_ends[None, :] <= blk_start[:, None]).astype(jnp.int32), axis=1),
                        N_EXPERTS - 1).astype(jnp.int32)
    n_used = (pad_ends[-1:] // g).astype(jnp.int32)
    return dest, blk_e, n_used, p_total


def _moe_ffn(h_tiles, x1, route, mod, wg, wu, wd, final_g, seq, final):
    n, d = x1.shape
    dest, blk_e, n_used, p_total = _moe_layout(route, n)
    xb = _dispatch(h_tiles, dest, p_total, seq, d)
    yb = _moe_grouped(xb, blk_e, n_used, wg, wu, wd)
    return _combine(x1, yb, dest, route, mod, final_g, seq, final)


def kernel(x, c, norm_mix_g, norm_ffn_g, ada_w, ada_b, w_in, w_out, att_out_g, ret_out_g, ffn_w_gate,
           ffn_w_up, ffn_w_down, router_w, moe_w_gate, moe_w_up, moe_w_down, final_norm_g):
    batch, seq, d = x.shape
    depth = ada_w.shape[0]
    assert seq % MOBA_BLOCK == 0 and seq % RET_CHUNK == 0
    assert w_in.shape[2] == N_GROUPS * GROUP_WIDTH
    mods = _adaln(c, ada_w, ada_b).reshape(depth, batch, 6, d)
    tables = _rotary_tables(seq)
    xf = x.reshape(batch * seq, d)
    for l in range(depth):
        mod = mods[l]
        final = l == depth - 1
        proj = _inproj(xf, mod, norm_mix_g[l], w_in[l].astype(BF16), tables, seq)
        o_a = _moba(proj, batch, seq)
        o_r = _retention(proj, ret_out_g[l], batch, seq)
        if l % 2 == 0:
            x1, h = _outproj(o_a, o_r, xf, w_out[l].astype(BF16), att_out_g[l], mod, norm_ffn_g[l], seq)
            xf = _dense_ffn(h, x1, mod, ffn_w_gate[l // 2].astype(BF16), ffn_w_up[l // 2].astype(BF16),
                            ffn_w_down[l // 2].astype(BF16), final_norm_g, seq, final)
        else:
            x1, h, route = _outproj(o_a, o_r, xf, w_out[l].astype(BF16), att_out_g[l], mod, norm_ffn_g[l],
                                    seq, router_w=router_w[l // 2])
            xf = _moe_ffn(h, x1, route, mod, moe_w_gate[l // 2].astype(BF16), moe_w_up[l // 2].astype(BF16),
                          moe_w_down[l // 2].astype(BF16), final_norm_g, seq, final)
    return xf.reshape(batch, seq, d)
```

```python
import functools

import jax
import jax.numpy as jnp
from jax import lax
from jax.experimental import pallas as pl
from jax.experimental.pallas import tpu as pltpu

F32 = jnp.float32
BF16 = jnp.bfloat16

LANES = 128
V7X_MXU_WIDTH = 256
V7X_VMEM_LIMIT_BYTES = 56 * 1024 * 1024

ATT_HEADS = 8
ATT_HEAD_DIM = 64
ATT_WIDTH = ATT_HEADS * ATT_HEAD_DIM
MOBA_BLOCK = 256
MOBA_TOPK = 3
MOBA_PAIRS_PER_STEP = 2
MOBA_GATE_ROWS = 16
MASK_VALUE = -1e30
MOBA_Q_SCALE = ATT_HEAD_DIM ** -0.5 * 1.4426950408889634
RET_HEADS = 4
RET_HEAD_DIM = 128
RET_WIDTH = RET_HEADS * RET_HEAD_DIM
RET_CHUNK = 128
ROPE_BASE = 10000.0
N_EXPERTS = 8
TOP_K = 2
EPS = 1e-6
GROUP_WIDTH = 512
N_GROUPS = 7
MOE_TILE_TOKENS = 512
MOE_RUN_BITS = 11
FFN_ROWS = 512


def _tile(total, target, mult):
    best = None
    t = mult
    while t <= min(total, target):
        if total % t == 0:
            best = t
        t += mult
    return best if best is not None else total


def _params(*sem):
    return pltpu.CompilerParams(dimension_semantics=sem, vmem_limit_bytes=V7X_VMEM_LIMIT_BYTES)


def _rms(x):
    return x * lax.rsqrt(jnp.mean(x * x, axis=-1, keepdims=True) + EPS)


def _silu(x):
    return x * jax.nn.sigmoid(x)


def _store_token_tiles(ref, x, row0=0):
    rows, d = x.shape
    sub = d // LANES
    for s in range(sub):
        ref[pl.ds(row0 * sub + s, rows, stride=sub), :] = x[:, s * LANES:(s + 1) * LANES].astype(ref.dtype)


def _load_token_tiles(ref, rows, d, row0=0):
    sub = d // LANES
    return jnp.concatenate([ref[pl.ds(row0 * sub + s, rows, stride=sub), :] for s in range(sub)], axis=1)


def _token_tile(ref, t, sub):
    return ref.at[pl.ds(pl.multiple_of(t * sub, sub), sub)]


def _adaln_kernel(c_ref, w_ref, b_ref, o_ref):
    cs = _silu(c_ref[...]).astype(BF16)
    o_ref[0] = jnp.dot(cs, w_ref[0].astype(BF16), preferred_element_type=F32) + b_ref[0]


def _adaln(c, ada_w, ada_b):
    depth, d, w = ada_w.shape
    b = c.shape[0]
    tn = _tile(w, 1536, LANES)
    return pl.pallas_call(
        _adaln_kernel,
        grid=(depth, w // tn),
        in_specs=[
            pl.BlockSpec((b, d), lambda l, j: (0, 0)),
            pl.BlockSpec((1, d, tn), lambda l, j: (l, 0, j)),
            pl.BlockSpec((1, 1, tn), lambda l, j: (l, 0, j)),
        ],
        out_specs=pl.BlockSpec((1, b, tn), lambda l, j: (l, 0, j)),
        out_shape=jax.ShapeDtypeStruct((depth, b, w), F32),
        compiler_params=_params("arbitrary", "arbitrary"),
    )(c, ada_w, ada_b.reshape(depth, 1, w))


def _inproj_kernel(x_ref, mod_ref, g_ref, w_ref, cq_ref, sq_ref, ck_ref, sk_ref, o_ref):
    m = mod_ref[0]
    h = (_rms(x_ref[...]) * g_ref[...] * (1.0 + m[1:2]) + m[0:1]).astype(BF16)
    rotary = {3: (cq_ref, sq_ref), 4: (ck_ref, sk_ref)}
    for j in range(N_GROUPS):
        c0 = j * GROUP_WIDTH
        acc = jnp.dot(h, w_ref[:, c0:c0 + GROUP_WIDTH], preferred_element_type=F32)
        if j in rotary:
            cos = rotary[j][0][...]
            sin = rotary[j][1][...]
            for hd in range(RET_HEADS):
                a = acc[:, hd * RET_HEAD_DIM:(hd + 1) * RET_HEAD_DIM]
                r = a * cos + pltpu.roll(a, RET_HEAD_DIM // 2, 1) * sin
                o_ref[:, c0 + hd * RET_HEAD_DIM:c0 + (hd + 1) * RET_HEAD_DIM] = r.astype(o_ref.dtype)
        elif j == 0:
            o_ref[:, c0:c0 + GROUP_WIDTH] = (acc * MOBA_Q_SCALE).astype(o_ref.dtype)
        else:
            o_ref[:, c0:c0 + GROUP_WIDTH] = acc.astype(o_ref.dtype)


def _rotary_tables(s):
    half = RET_HEAD_DIM // 2
    inv_freq = ROPE_BASE ** (-jnp.arange(half, dtype=F32) / half)
    ang = jnp.arange(s, dtype=F32)[:, None] * inv_freq[None, :]
    cos = jnp.cos(ang)
    sin = jnp.sin(ang)
    cos2 = jnp.concatenate([cos, cos], axis=-1)
    sin2 = jnp.concatenate([-sin, sin], axis=-1)
    k_scale = RET_HEAD_DIM ** -0.5
    return cos2, sin2, cos2 * k_scale, sin2 * k_scale


def _inproj(x2d, mod, g, w_bf16, tables, seq):
    n, d = x2d.shape
    tm = _tile(seq, 1024, 16)
    per_seq = seq // tm
    width = N_GROUPS * GROUP_WIDTH
    return pl.pallas_call(
        _inproj_kernel,
        grid=(n // tm,),
        in_specs=[
            pl.BlockSpec((tm, d), lambda i: (i, 0)),
            pl.BlockSpec((1, 6, d), lambda i: (i // per_seq, 0, 0)),
            pl.BlockSpec((1, d), lambda i: (0, 0)),
            pl.BlockSpec((d, width), lambda i: (0, 0)),
        ] + [pl.BlockSpec((tm, RET_HEAD_DIM), lambda i: (i % per_seq, 0))] * 4,
        out_specs=pl.BlockSpec((tm, width), lambda i: (i, 0)),
        out_shape=jax.ShapeDtypeStruct((n, width), BF16),
        compiler_params=_params("arbitrary"),
    )(x2d, mod, g.reshape(1, d), w_bf16, *tables)


def _moba_kernel(q_ref, k_ref, v_ref, o_ref):
    s = q_ref.shape[0]
    blk = MOBA_BLOCK
    nb = s // blk
    assert nb <= MOBA_GATE_ROWS
    n_sel = min(MOBA_TOPK, nb - 1)
    shift = blk.bit_length() - 1
    contract_lanes = (((1,), (1,)), ((), ()))
    lane = lax.broadcasted_iota(jnp.int32, (1, LANES), 1)
    head_of_lane = jnp.right_shift(lane, ATT_HEAD_DIM.bit_length() - 1)

    key_pos = lax.broadcasted_iota(jnp.int32, (s, LANES), 0)
    key_lane = lax.broadcasted_iota(jnp.int32, (s, LANES), 1)
    key_block_onehot = jnp.where(jnp.right_shift(key_pos, shift) == key_lane, 1.0, 0.0).astype(BF16)
    ones = jnp.ones((s, LANES), BF16)

    blk_id = lax.broadcasted_iota(jnp.int32, (MOBA_GATE_ROWS, s), 0)
    q_blk = jnp.right_shift(lax.broadcasted_iota(jnp.int32, (MOBA_GATE_ROWS, s), 1), shift)
    past = blk_id < q_blk
    block_mean = jnp.where(q_blk == blk_id, 1.0 / blk, 0.0).astype(BF16)
    eye = jnp.where(lax.broadcasted_iota(jnp.int32, (MOBA_GATE_ROWS, LANES), 0)
                    == lax.broadcasted_iota(jnp.int32, (MOBA_GATE_ROWS, LANES), 1), 1.0, 0.0).astype(BF16)
    row = lax.broadcasted_iota(jnp.int32, (2 * blk, blk), 0)
    col = lax.broadcasted_iota(jnp.int32, (2 * blk, blk), 1)
    causal_bias = jnp.where(col <= jnp.bitwise_and(row, blk - 1), 0.0, MASK_VALUE)

    for pair in range(q_ref.shape[1] // LANES):
        cols = slice(pair * LANES, (pair + 1) * LANES)
        q_all = q_ref[:, cols]
        k_all = k_ref[:, cols]
        k_aug = jnp.concatenate([k_all, key_block_onehot], axis=1)
        v_aug = jnp.concatenate([v_ref[:, cols], ones], axis=1)
        k_mean = jnp.dot(block_mean, k_all, preferred_element_type=F32).astype(BF16)

        heads = []
        for hh in range(2):
            qh = jnp.where(head_of_lane == hh, q_all, jnp.zeros_like(q_all))
            gate = lax.dot_general(k_mean, qh, contract_lanes, preferred_element_type=F32)
            gm = jnp.where(past, gate, MASK_VALUE)
            rank = jnp.zeros(gm.shape, jnp.int32)
            for jp in range(nb):
                cv = gm[jp:jp + 1, :]
                beats = jnp.logical_or(cv > gm, jnp.logical_and(cv == gm, jp < blk_id))
                rank = rank + jnp.where(beats, 1, 0)
            visible = jnp.logical_or(jnp.logical_and(rank < n_sel, past), blk_id >= q_blk)
            bias_t = jnp.where(visible, 0.0, MASK_VALUE).astype(BF16)
            bias = lax.dot_general(bias_t, eye, (((0,), (0,)), ((), ())), preferred_element_type=F32)
            heads.append(jnp.concatenate([qh, bias.astype(BF16)], axis=1))

        for i in range(nb):
            nk = (i + 1) * blk
            rows = slice(i * blk, (i + 1) * blk)
            q_both = jnp.concatenate([heads[0][rows], heads[1][rows]], axis=0)
            sc = lax.dot_general(q_both, k_aug[:nk], contract_lanes, preferred_element_type=F32)
            sc_own = sc[:, i * blk:] + causal_bias
            m = jnp.max(sc_own, axis=-1, keepdims=True)
            if i > 0:
                sc_past = sc[:, :i * blk]
                m = jnp.maximum(m, jnp.max(sc_past, axis=-1, keepdims=True))
                p = jnp.concatenate([jnp.exp2(sc_past - m), jnp.exp2(sc_own - m)], axis=1)
            else:
                p = jnp.exp2(sc_own - m)
            o = jnp.dot(p.astype(BF16), v_aug[:nk], preferred_element_type=F32)
            o = o[:, :LANES] / o[:, LANES:LANES + 1]
            o_ref[rows, cols] = jnp.where(head_of_lane == 0, o[:blk], o[blk:]).astype(o_ref.dtype)


def _moba(proj, batch, seq):
    n = proj.shape[0]
    width = MOBA_PAIRS_PER_STEP * LANES
    steps = ATT_WIDTH // width
    spec = lambda off: pl.BlockSpec((seq, width), lambda b, p: (b, off + p))
    return pl.pallas_call(
        _moba_kernel,
        grid=(batch, steps),
        in_specs=[spec(0), spec(steps), spec(2 * steps)],
        out_specs=pl.BlockSpec((seq, width), lambda b, p: (b, p)),
        out_shape=jax.ShapeDtypeStruct((n, ATT_WIDTH), BF16),
        compiler_params=_params("arbitrary", "arbitrary"),
    )(proj, proj, proj)


def _retention_kernel(q_ref, k_ref, v_ref, g_ref, dm_ref, qd_ref, kd_ref, cd_ref, og_ref, o_ref):
    s = q_ref.shape[0]
    c = RET_CHUNK
    states = [jnp.zeros((RET_HEAD_DIM, RET_HEAD_DIM), F32) for _ in range(RET_HEADS)]
    for n in range(s // c):
        rows = slice(n * c, (n + 1) * c)
        for hd in range(RET_HEADS):
            cols = slice(hd * RET_HEAD_DIM, (hd + 1) * RET_HEAD_DIM)
            qc = q_ref[rows, cols]
            kc = k_ref[rows, cols]
            vc = v_ref[rows, cols]
            scores = lax.dot_general(qc, kc, (((1,), (1,)), ((), ())), preferred_element_type=F32) * dm_ref[hd]
            inner = jnp.dot(scores.astype(BF16), vc, preferred_element_type=F32)
            cross = jnp.dot(qc, states[hd].astype(BF16), preferred_element_type=F32) * qd_ref[hd]
            o = _rms(inner + cross) * og_ref[hd]
            gate = g_ref[rows, cols].astype(F32)
            o_ref[rows, cols] = (_silu(gate) * o).astype(o_ref.dtype)
            k_dec = (kc.astype(F32) * kd_ref[hd]).astype(BF16)
            kv = lax.dot_general(k_dec, vc, (((0,), (0,)), ((), ())), preferred_element_type=F32)
            states[hd] = states[hd] * cd_ref[hd] + kv


def _retention_tables():
    h, c = RET_HEADS, RET_CHUNK
    lg = jnp.log(1.0 - 2.0 ** (-5.0 - jnp.arange(h, dtype=F32)))
    idx = jnp.arange(c, dtype=F32)
    diff = idx[:, None] - idx[None, :]
    dmask = jnp.where(diff >= 0, jnp.exp(jnp.maximum(diff, 0.0)[None] * lg[:, None, None]), 0.0)
    rows = lambda v: jnp.broadcast_to(v[:, :, None], (h, c, RET_HEAD_DIM))
    qdec = rows(jnp.exp((idx + 1.0)[None, :] * lg[:, None]))
    kdec = rows(jnp.exp((c - 1 - idx)[None, :] * lg[:, None]))
    cdec = jnp.broadcast_to(jnp.exp(c * lg)[:, None, None], (h, RET_HEAD_DIM, RET_HEAD_DIM))
    return dmask, qdec, kdec, cdec


def _retention(proj, ret_out_g, batch, seq):
    n = proj.shape[0]
    base = 3 * ATT_WIDTH // RET_WIDTH
    spec = lambda off: pl.BlockSpec((seq, RET_WIDTH), lambda b: (b, base + off))
    table = pl.BlockSpec((RET_HEADS, RET_CHUNK, RET_HEAD_DIM), lambda b: (0, 0, 0))
    return pl.pallas_call(
        _retention_kernel,
        grid=(batch,),
        in_specs=[spec(0), spec(1), spec(2), spec(3), table, table, table, table,
                  pl.BlockSpec((RET_HEADS, 1, RET_HEAD_DIM), lambda b: (0, 0, 0))],
        out_specs=pl.BlockSpec((seq, RET_WIDTH), lambda b: (b, 0)),
        out_shape=jax.ShapeDtypeStruct((n, RET_WIDTH), BF16),
        compiler_params=_params("arbitrary"),
    )(proj, proj, proj, proj, *_retention_tables(), ret_out_g.reshape(RET_HEADS, 1, RET_HEAD_DIM))


def _top2_route(logits):
    lane = lax.broadcasted_iota(jnp.int32, logits.shape, 1)
    lg = jnp.where(lane < N_EXPERTS, logits, -jnp.inf)
    m1 = jnp.max(lg, axis=-1, keepdims=True)
    i1 = jnp.min(jnp.where(lg == m1, lane, LANES), axis=-1, keepdims=True)
    lg2 = jnp.where(lane == i1, -jnp.inf, lg)
    m2 = jnp.max(lg2, axis=-1, keepdims=True)
    i2 = jnp.min(jnp.where(lg2 == m2, lane, LANES), axis=-1, keepdims=True)
    e = jnp.exp(m2 - m1)
    w1 = 1.0 / (1.0 + e)
    w2 = e / (1.0 + e)
    out = jnp.where(lane == 0, i1.astype(F32), 0.0)
    out = jnp.where(lane == 1, i2.astype(F32), out)
    out = jnp.where(lane == 2, w1, out)
    return jnp.where(lane == 3, w2, out)


def _outproj_kernel(oa_ref, or_ref, x_ref, w_ref, ag_ref, mod_ref, ng_ref, *rest, route):
    if route:
        rw_ref, x1_ref, h_ref, rt_ref = rest
    else:
        x1_ref, h_ref = rest
    m = mod_ref[0]
    oa = (_rms(oa_ref[...].astype(F32)) * ag_ref[...]).astype(BF16)
    mix = (jnp.dot(oa, w_ref[:ATT_WIDTH, :], preferred_element_type=F32)
           + jnp.dot(or_ref[...], w_ref[ATT_WIDTH:, :], preferred_element_type=F32))
    x1 = x_ref[...] + m[2:3] * mix
    x1_ref[...] = x1
    h = _rms(x1) * ng_ref[...] * (1.0 + m[4:5]) + m[3:4]
    if route:
        _store_token_tiles(h_ref, h)
        logits = jnp.dot(h.astype(BF16), rw_ref[...], preferred_element_type=F32)
        rt_ref[...] = _top2_route(logits)
    else:
        h_ref[...] = h.astype(h_ref.dtype)


def _outproj(o_a, o_r, x2d, w_bf16, att_g, mod, norm_g, seq, router_w=None):
    n, d = x2d.shape
    tm = _tile(seq, 512, 16)
    per_seq = seq // tm
    route = router_w is not None
    row = lambda w: pl.BlockSpec((tm, w), lambda i: (i, 0))
    full = lambda a, b: pl.BlockSpec((a, b), lambda i: (0, 0))
    in_specs = [row(ATT_WIDTH), row(RET_WIDTH), row(d), full(ATT_WIDTH + RET_WIDTH, d),
                full(1, ATT_WIDTH), pl.BlockSpec((1, 6, d), lambda i: (i // per_seq, 0, 0)), full(1, d)]
    args = [o_a, o_r, x2d, w_bf16, att_g.reshape(1, ATT_WIDTH), mod, norm_g.reshape(1, d)]
    out_specs = [row(d), row(d)]
    out_shape = [jax.ShapeDtypeStruct((n, d), F32), jax.ShapeDtypeStruct((n, d), BF16)]
    if route:
        sub = d // LANES
        out_specs[1] = pl.BlockSpec((tm * sub, LANES), lambda i: (i, 0))
        out_shape[1] = jax.ShapeDtypeStruct((n * sub, LANES), F32)
        rw = jnp.zeros((d, LANES), BF16).at[:, :N_EXPERTS].set(router_w.astype(BF16))
        in_specs.append(full(d, LANES))
        args.append(rw)
        out_specs.append(row(LANES))
        out_shape.append(jax.ShapeDtypeStruct((n, LANES), F32))
    return pl.pallas_call(
        functools.partial(_outproj_kernel, route=route),
        grid=(n // tm,),
        in_specs=in_specs,
        out_specs=out_specs,
        out_shape=out_shape,
        compiler_params=_params("arbitrary"),
    )(*args)


def _swiglu(h, wg_ref, wu_ref, wd_ref, act_ref):
    ff = act_ref.shape[1]
    tf = _tile(ff, V7X_MXU_WIDTH, LANES)
    for c0 in range(0, ff, tf):
        g = jnp.dot(h, wg_ref[:, c0:c0 + tf], preferred_element_type=F32)
        u = jnp.dot(h, wu_ref[:, c0:c0 + tf], preferred_element_type=F32)
        act_ref[:, c0:c0 + tf] = (_silu(g) * u).astype(act_ref.dtype)
    return jnp.dot(act_ref[...], wd_ref[...], preferred_element_type=F32)


def _dense_ffn_kernel(h_ref, x1_ref, mod_ref, wg_ref, wu_ref, wd_ref, fg_ref, o_ref, act_ref, *, final):
    y = _swiglu(h_ref[...], wg_ref, wu_ref, wd_ref, act_ref)
    x2 = x1_ref[...] + mod_ref[0][5:6] * y
    o_ref[...] = _rms(x2) * fg_ref[...] if final else x2


def _dense_ffn(h, x1, mod, wg, wu, wd, final_g, seq, final):
    n, d = x1.shape
    ff = wg.shape[1]
    tm = _tile(seq, FFN_ROWS, 16)
    per_seq = seq // tm
    row = pl.BlockSpec((tm, d), lambda i: (i, 0))
    return pl.pallas_call(
        functools.partial(_dense_ffn_kernel, final=final),
        grid=(n // tm,),
        in_specs=[
            row, row,
            pl.BlockSpec((1, 6, d), lambda i: (i // per_seq, 0, 0)),
            pl.BlockSpec((d, ff), lambda i: (0, 0)),
            pl.BlockSpec((d, ff), lambda i: (0, 0)),
            pl.BlockSpec((ff, d), lambda i: (0, 0)),
            pl.BlockSpec((1, d), lambda i: (0, 0)),
        ],
        out_specs=row,
        out_shape=jax.ShapeDtypeStruct((n, d), F32),
        scratch_shapes=[pltpu.VMEM((tm, ff), BF16)],
        compiler_params=_params("arbitrary"),
    )(h, x1, mod, wg, wu, wd, final_g.reshape(1, d))


def _moe_ffn_kernel(blk_e_ref, n_used_ref, x_ref, wg_ref, wu_ref, wd_ref, o_ref, xs_ref, act_ref):
    del blk_e_ref
    g, d = xs_ref.shape

    @pl.when(pl.program_id(0) < n_used_ref[0])
    def _():
        xs_ref[...] = _load_token_tiles(x_ref, g, d).astype(BF16)
        _store_token_tiles(o_ref, _swiglu(xs_ref[...], wg_ref.at[0], wu_ref.at[0], wd_ref.at[0], act_ref))

    @pl.when(pl.program_id(0) >= n_used_ref[0])
    def _():
        o_ref[...] = jnp.zeros_like(o_ref)


def _moe_grouped(xb, blk_e, n_used, wg, wu, wd):
    _, d, ff = wg.shape
    sub = d // LANES
    g = FFN_ROWS
    grid_spec = pltpu.PrefetchScalarGridSpec(
        num_scalar_prefetch=2,
        grid=(xb.shape[0] // (g * sub),),
        in_specs=[
            pl.BlockSpec((g * sub, LANES), lambda b, be, nu: (b, 0)),
            pl.BlockSpec((1, d, ff), lambda b, be, nu: (be[b], 0, 0)),
            pl.BlockSpec((1, d, ff), lambda b, be, nu: (be[b], 0, 0)),
            pl.BlockSpec((1, ff, d), lambda b, be, nu: (be[b], 0, 0)),
        ],
        out_specs=pl.BlockSpec((g * sub, LANES), lambda b, be, nu: (b, 0)),
        scratch_shapes=[pltpu.VMEM((g, d), BF16), pltpu.VMEM((g, ff), BF16)],
    )
    return pl.pallas_call(
        _moe_ffn_kernel,
        grid_spec=grid_spec,
        out_shape=jax.ShapeDtypeStruct(xb.shape, F32),
        compiler_params=_params("arbitrary"),
    )(blk_e, n_used, xb, wg, wu, wd)


def _start_run_copies(src_ref, dst_ref, src_row, dst_row, count, sub, sem):
    for bit in range(MOE_RUN_BITS):
        size = 1 << bit
        off = jnp.left_shift(jnp.right_shift(count, bit + 1), bit + 1)

        @pl.when(jnp.bitwise_and(jnp.right_shift(count, bit), 1) == 1)
        def _():
            pltpu.make_async_copy(
                src_ref.at[pl.ds(pl.multiple_of((src_row + off) * sub, sub), size * sub)],
                dst_ref.at[pl.ds(pl.multiple_of((dst_row + off) * sub, sub), size * sub)], sem).start()


def _dispatch_kernel(pos_ref, cnt_ref, glob_ref, loc_ref, h_ref, init_ref, xb_ref, sbuf, sems, *, sub):
    del init_ref
    i = pl.program_id(0)
    last = pl.num_programs(0) - 1
    tm = h_ref.shape[0] // sub
    ta = tm * TOP_K
    slot = lax.rem(i, 2)

    def wait_slot(s):
        pltpu.make_async_copy(sbuf.at[s], xb_ref.at[pl.ds(0, ta * sub)], sems.at[s]).wait()

    @pl.when(i >= 2)
    def _():
        wait_slot(slot)

    def place(t, carry):
        v = _token_tile(h_ref, t, sub)[...]
        for k in range(TOP_K):
            _token_tile(sbuf.at[slot], pos_ref[i * ta + TOP_K * t + k], sub)[...] = v
        return carry

    lax.fori_loop(0, tm, place, 0, unroll=8)

    for e in range(N_EXPERTS):
        r = i * N_EXPERTS + e
        _start_run_copies(sbuf.at[slot], xb_ref, loc_ref[r], glob_ref[r], cnt_ref[r], sub, sems.at[slot])

    @pl.when(i == last)
    def _():
        @pl.when(i >= 1)
        def _():
            wait_slot(1 - slot)
        wait_slot(slot)


def _dispatch(h_tiles, layout, seq, d):
    sub = d // LANES
    n = h_tiles.shape[0] // sub
    tm = _tile(seq, MOE_TILE_TOKENS, 8)
    grid_spec = pltpu.PrefetchScalarGridSpec(
        num_scalar_prefetch=4,
        grid=(n // tm,),
        in_specs=[pl.BlockSpec((tm * sub, LANES), lambda i, *_: (i, 0)), pl.BlockSpec(memory_space=pl.ANY)],
        out_specs=pl.BlockSpec(memory_space=pl.ANY),
        scratch_shapes=[pltpu.VMEM((2, tm * TOP_K * sub, LANES), h_tiles.dtype), pltpu.SemaphoreType.DMA((2,))],
    )
    rows = layout["p_total"] * sub
    return pl.pallas_call(
        functools.partial(_dispatch_kernel, sub=sub),
        grid_spec=grid_spec,
        out_shape=jax.ShapeDtypeStruct((rows, LANES), h_tiles.dtype),
        input_output_aliases={5: 0},
        compiler_params=_params("arbitrary"),
    )(layout["pos"], layout["cnt"], layout["glob"], layout["loc"], h_tiles, jnp.zeros((rows, LANES), h_tiles.dtype))


def _combine_kernel(pos_ref, cnt_ref, glob_ref, loc_ref, x1_ref, rt_ref, mod_ref, fg_ref, yb_ref, o_ref,
                    ybuf, tbuf, sems, *, final):
    i = pl.program_id(0)
    tm, d = x1_ref.shape
    sub = d // LANES
    ta = tm * TOP_K
    slot = lax.rem(i, 2)

    def fetch(tile, s):
        for e in range(N_EXPERTS):
            r = tile * N_EXPERTS + e
            _start_run_copies(yb_ref, ybuf.at[s], glob_ref[r], loc_ref[r], cnt_ref[r], sub, sems.at[s])

    @pl.when(i == 0)
    def _():
        fetch(0, 0)

    @pl.when(i + 1 < pl.num_programs(0))
    def _():
        fetch(i + 1, 1 - slot)

    pltpu.make_async_copy(yb_ref.at[pl.ds(0, ta * sub)], ybuf.at[slot], sems.at[slot]).wait()

    def unpermute(t, carry):
        for k in range(TOP_K):
            _token_tile(tbuf.at[k], t, sub)[...] = _token_tile(ybuf.at[slot], pos_ref[i * ta + TOP_K * t + k], sub)[...]
        return carry

    lax.fori_loop(0, tm, unpermute, 0, unroll=8)

    rt = rt_ref[...]
    y = (rt[:, 2:3] * _load_token_tiles(tbuf.at[0], tm, d) + rt[:, 3:4] * _load_token_tiles(tbuf.at[1], tm, d))
    x2 = x1_ref[...] + mod_ref[0][5:6] * y
    o_ref[...] = _rms(x2) * fg_ref[...] if final else x2


def _combine(x1, yb, layout, route, mod, final_g, seq, final):
    n, d = x1.shape
    sub = d // LANES
    tm = _tile(seq, MOE_TILE_TOKENS, 8)
    per_seq = seq // tm
    grid_spec = pltpu.PrefetchScalarGridSpec(
        num_scalar_prefetch=4,
        grid=(n // tm,),
        in_specs=[
            pl.BlockSpec((tm, d), lambda i, *_: (i, 0)),
            pl.BlockSpec((tm, LANES), lambda i, *_: (i, 0)),
            pl.BlockSpec((1, 6, d), lambda i, *_: (i // per_seq, 0, 0)),
            pl.BlockSpec((1, d), lambda i, *_: (0, 0)),
            pl.BlockSpec(memory_space=pl.ANY),
        ],
        out_specs=pl.BlockSpec((tm, d), lambda i, *_: (i, 0)),
        scratch_shapes=[pltpu.VMEM((2, TOP_K * tm * sub, LANES), F32),
                        pltpu.VMEM((TOP_K, tm * sub, LANES), F32),
                        pltpu.SemaphoreType.DMA((2,))],
    )
    return pl.pallas_call(
        functools.partial(_combine_kernel, final=final),
        grid_spec=grid_spec,
        out_shape=jax.ShapeDtypeStruct((n, d), F32),
        compiler_params=_params("arbitrary"),
    )(layout["pos"], layout["cnt"], layout["glob"], layout["loc"], x1, route, mod, final_g.reshape(1, d), yb)


def _moe_layout(route, n, seq):
    g = FFN_ROWS
    a = n * TOP_K
    ta = _tile(seq, MOE_TILE_TOKENS, 8) * TOP_K
    tiles = a // ta
    assert ta < (1 << MOE_RUN_BITS)
    e_flat = route[:, :TOP_K].astype(jnp.int32).reshape(-1)
    onehot = (e_flat[:, None] == jnp.arange(N_EXPERTS, dtype=jnp.int32)[None, :]).astype(jnp.int32)
    csum = jnp.cumsum(onehot, axis=0)
    rank = jnp.sum(csum * onehot, axis=1) - 1
    counts = csum[-1]
    padded = (counts + g - 1) // g * g
    pad_ends = jnp.cumsum(padded)
    pad_starts = pad_ends - padded
    dest = jnp.sum(pad_starts[None, :] * onehot, axis=1) + rank
    tile_end = csum[ta - 1::ta]
    cnt = tile_end - jnp.concatenate([jnp.zeros((1, N_EXPERTS), jnp.int32), tile_end[:-1]], axis=0)
    glob = pad_starts[None, :] + tile_end - cnt
    loc = jnp.cumsum(cnt, axis=1) - cnt
    per_assignment = lambda v: jnp.sum(
        jnp.broadcast_to(v[:, None, :], (tiles, ta, N_EXPERTS)).reshape(a, N_EXPERTS) * onehot, axis=1)
    pos = per_assignment(loc) + dest - per_assignment(glob)
    p_total = (-(-a // g) + N_EXPERTS) * g
    blk_start = jnp.arange(p_total // g, dtype=jnp.int32) * g
    blk_e = jnp.minimum(jnp.sum((pad_ends[None, :] <= blk_start[:, None]).astype(jnp.int32), axis=1),
                        N_EXPERTS - 1)
    i32 = lambda v: v.astype(jnp.int32).reshape(-1)
    return dict(pos=i32(pos), cnt=i32(cnt), glob=i32(glob), loc=i32(loc), blk_e=i32(blk_e),
                n_used=i32(pad_ends[-1:] // g), p_total=p_total)


def _moe_ffn(h_tiles, x1, route, mod, wg, wu, wd, final_g, seq, final):
    n, d = x1.shape
    layout = _moe_layout(route, n, seq)
    xb = _dispatch(h_tiles, layout, seq, d)
    yb = _moe_grouped(xb, layout["blk_e"], layout["n_used"], wg, wu, wd)
    return _combine(x1, yb, layout, route, mod, final_g, seq, final)


def kernel(x, c, norm_mix_g, norm_ffn_g, ada_w, ada_b, w_in, w_out, att_out_g, ret_out_g, ffn_w_gate,
           ffn_w_up, ffn_w_down, router_w, moe_w_gate, moe_w_up, moe_w_down, final_norm_g):
    batch, seq, d = x.shape
    depth = ada_w.shape[0]
    assert seq % MOBA_BLOCK == 0 and seq % RET_CHUNK == 0
    assert w_in.shape[2] == N_GROUPS * GROUP_WIDTH
    mods = _adaln(c, ada_w, ada_b).reshape(depth, batch, 6, d)
    tables = _rotary_tables(seq)
    xf = x.reshape(batch * seq, d)
    for l in range(depth):
        mod = mods[l]
        final = l == depth - 1
        proj = _inproj(xf, mod, norm_mix_g[l], w_in[l].astype(BF16), tables, seq)
        o_a = _moba(proj, batch, seq)
        o_r = _retention(proj, ret_out_g[l], batch, seq)
        if l % 2 == 0:
            x1, h = _outproj(o_a, o_r, xf, w_out[l].astype(BF16), att_out_g[l], mod, norm_ffn_g[l], seq)
            xf = _dense_ffn(h, x1, mod, ffn_w_gate[l // 2].astype(BF16), ffn_w_up[l // 2].astype(BF16),
                            ffn_w_down[l // 2].astype(BF16), final_norm_g, seq, final)
        else:
            x1, h, route = _outproj(o_a, o_r, xf, w_out[l].astype(BF16), att_out_g[l], mod, norm_ffn_g[l],
                                    seq, router_w=router_w[l // 2])
            xf = _moe_ffn(h, x1, route, mod, moe_w_gate[l // 2].astype(BF16), moe_w_up[l // 2].astype(BF16),
                          moe_w_down[l // 2].astype(BF16), final_norm_g, seq, final)
    return xf.reshape(batch, seq, d)
```

```python
import functools

import jax
import jax.numpy as jnp
from jax import lax
from jax.experimental import pallas as pl
from jax.experimental.pallas import tpu as pltpu

F32 = jnp.float32
BF16 = jnp.bfloat16

LANES = 128
V7X_MXU_WIDTH = 256
V7X_VMEM_LIMIT_BYTES = 56 * 1024 * 1024

ATT_HEADS = 8
ATT_HEAD_DIM = 64
ATT_WIDTH = ATT_HEADS * ATT_HEAD_DIM
MOBA_BLOCK = 256
MOBA_TOPK = 3
MOBA_PAIRS_PER_STEP = 2
MOBA_GATE_ROWS = 16
MASK_VALUE = -1e30
MOBA_Q_SCALE = ATT_HEAD_DIM ** -0.5 * 1.4426950408889634
RET_HEADS = 4
RET_HEAD_DIM = 128
RET_WIDTH = RET_HEADS * RET_HEAD_DIM
RET_BLOCK = 256
ROPE_BASE = 10000.0
N_EXPERTS = 8
TOP_K = 2
EPS = 1e-6
GROUP_WIDTH = 512
N_GROUPS = 7
MOE_TILE_TOKENS = 512
MOE_RUN_BITS = 11
FFN_ROWS = 512


def _tile(total, target, mult):
    best = None
    t = mult
    while t <= min(total, target):
        if total % t == 0:
            best = t
        t += mult
    return best if best is not None else total


def _params(*sem):
    return pltpu.CompilerParams(dimension_semantics=sem, vmem_limit_bytes=V7X_VMEM_LIMIT_BYTES)


def _rms(x):
    return x * lax.rsqrt(jnp.mean(x * x, axis=-1, keepdims=True) + EPS)


def _silu(x):
    return x * jax.nn.sigmoid(x)


def _store_token_tiles(ref, x, row0=0):
    rows, d = x.shape
    sub = d // LANES
    for s in range(sub):
        ref[pl.ds(row0 * sub + s, rows, stride=sub), :] = x[:, s * LANES:(s + 1) * LANES].astype(ref.dtype)


def _load_token_tiles(ref, rows, d, row0=0):
    sub = d // LANES
    return jnp.concatenate([ref[pl.ds(row0 * sub + s, rows, stride=sub), :] for s in range(sub)], axis=1)


def _token_tile(ref, t, sub):
    return ref.at[pl.ds(pl.multiple_of(t * sub, sub), sub)]


def _adaln_kernel(c_ref, w_ref, b_ref, o_ref):
    cs = _silu(c_ref[...]).astype(BF16)
    o_ref[0] = jnp.dot(cs, w_ref[0].astype(BF16), preferred_element_type=F32) + b_ref[0]


def _adaln(c, ada_w, ada_b):
    depth, d, w = ada_w.shape
    b = c.shape[0]
    tn = _tile(w, 1536, LANES)
    return pl.pallas_call(
        _adaln_kernel,
        grid=(depth, w // tn),
        in_specs=[
            pl.BlockSpec((b, d), lambda l, j: (0, 0)),
            pl.BlockSpec((1, d, tn), lambda l, j: (l, 0, j)),
            pl.BlockSpec((1, 1, tn), lambda l, j: (l, 0, j)),
        ],
        out_specs=pl.BlockSpec((1, b, tn), lambda l, j: (l, 0, j)),
        out_shape=jax.ShapeDtypeStruct((depth, b, w), F32),
        compiler_params=_params("arbitrary", "arbitrary"),
    )(c, ada_w, ada_b.reshape(depth, 1, w))


def _inproj_kernel(x_ref, mod_ref, g_ref, w_ref, cq_ref, sq_ref, ck_ref, sk_ref, o_ref):
    m = mod_ref[0]
    h = (_rms(x_ref[...]) * g_ref[...] * (1.0 + m[1:2]) + m[0:1]).astype(BF16)
    rotary = {3: (cq_ref, sq_ref), 4: (ck_ref, sk_ref)}
    for j in range(N_GROUPS):
        c0 = j * GROUP_WIDTH
        acc = jnp.dot(h, w_ref[:, c0:c0 + GROUP_WIDTH], preferred_element_type=F32)
        if j in rotary:
            cos = rotary[j][0][...]
            sin = rotary[j][1][...]
            for hd in range(RET_HEADS):
                a = acc[:, hd * RET_HEAD_DIM:(hd + 1) * RET_HEAD_DIM]
                r = a * cos + pltpu.roll(a, RET_HEAD_DIM // 2, 1) * sin
                o_ref[:, c0 + hd * RET_HEAD_DIM:c0 + (hd + 1) * RET_HEAD_DIM] = r.astype(o_ref.dtype)
        elif j == 0:
            o_ref[:, c0:c0 + GROUP_WIDTH] = (acc * MOBA_Q_SCALE).astype(o_ref.dtype)
        else:
            o_ref[:, c0:c0 + GROUP_WIDTH] = acc.astype(o_ref.dtype)


def _rotary_tables(s):
    half = RET_HEAD_DIM // 2
    inv_freq = ROPE_BASE ** (-jnp.arange(half, dtype=F32) / half)
    ang = jnp.arange(s, dtype=F32)[:, None] * inv_freq[None, :]
    cos = jnp.cos(ang)
    sin = jnp.sin(ang)
    cos2 = jnp.concatenate([cos, cos], axis=-1)
    sin2 = jnp.concatenate([-sin, sin], axis=-1)
    k_scale = RET_HEAD_DIM ** -0.5
    return cos2, sin2, cos2 * k_scale, sin2 * k_scale


def _inproj(x2d, mod, g, w_bf16, tables, seq):
    n, d = x2d.shape
    tm = _tile(seq, 1024, 16)
    per_seq = seq // tm
    width = N_GROUPS * GROUP_WIDTH
    return pl.pallas_call(
        _inproj_kernel,
        grid=(n // tm,),
        in_specs=[
            pl.BlockSpec((tm, d), lambda i: (i, 0)),
            pl.BlockSpec((1, 6, d), lambda i: (i // per_seq, 0, 0)),
            pl.BlockSpec((1, d), lambda i: (0, 0)),
            pl.BlockSpec((d, width), lambda i: (0, 0)),
        ] + [pl.BlockSpec((tm, RET_HEAD_DIM), lambda i: (i % per_seq, 0))] * 4,
        out_specs=pl.BlockSpec((tm, width), lambda i: (i, 0)),
        out_shape=jax.ShapeDtypeStruct((n, width), BF16),
        compiler_params=_params("arbitrary"),
    )(x2d, mod, g.reshape(1, d), w_bf16, *tables)


def _moba_kernel(q_ref, k_ref, v_ref, o_ref):
    s = q_ref.shape[0]
    blk = MOBA_BLOCK
    nb = s // blk
    assert nb <= MOBA_GATE_ROWS
    n_sel = min(MOBA_TOPK, nb - 1)
    shift = blk.bit_length() - 1
    contract_lanes = (((1,), (1,)), ((), ()))
    lane = lax.broadcasted_iota(jnp.int32, (1, LANES), 1)
    head_of_lane = jnp.right_shift(lane, ATT_HEAD_DIM.bit_length() - 1)

    key_pos = lax.broadcasted_iota(jnp.int32, (s, LANES), 0)
    key_lane = lax.broadcasted_iota(jnp.int32, (s, LANES), 1)
    key_block_onehot = jnp.where(jnp.right_shift(key_pos, shift) == key_lane, 1.0, 0.0).astype(BF16)
    ones = jnp.ones((s, LANES), BF16)

    blk_id = lax.broadcasted_iota(jnp.int32, (MOBA_GATE_ROWS, s), 0)
    q_blk = jnp.right_shift(lax.broadcasted_iota(jnp.int32, (MOBA_GATE_ROWS, s), 1), shift)
    past = blk_id < q_blk
    block_mean = jnp.where(q_blk == blk_id, 1.0 / blk, 0.0).astype(BF16)
    eye = jnp.where(lax.broadcasted_iota(jnp.int32, (MOBA_GATE_ROWS, LANES), 0)
                    == lax.broadcasted_iota(jnp.int32, (MOBA_GATE_ROWS, LANES), 1), 1.0, 0.0).astype(BF16)
    row = lax.broadcasted_iota(jnp.int32, (2 * blk, blk), 0)
    col = lax.broadcasted_iota(jnp.int32, (2 * blk, blk), 1)
    causal_bias = jnp.where(col <= jnp.bitwise_and(row, blk - 1), 0.0, MASK_VALUE)

    for pair in range(q_ref.shape[1] // LANES):
        cols = slice(pair * LANES, (pair + 1) * LANES)
        q_all = q_ref[:, cols]
        k_all = k_ref[:, cols]
        k_aug = jnp.concatenate([k_all, key_block_onehot], axis=1)
        v_aug = jnp.concatenate([v_ref[:, cols], ones], axis=1)
        k_mean = jnp.dot(block_mean, k_all, preferred_element_type=F32).astype(BF16)

        heads = []
        for hh in range(2):
            qh = jnp.where(head_of_lane == hh, q_all, jnp.zeros_like(q_all))
            gate = lax.dot_general(k_mean, qh, contract_lanes, preferred_element_type=F32)
            gm = jnp.where(past, gate, MASK_VALUE)
            rank = jnp.zeros(gm.shape, jnp.int32)
            for jp in range(nb):
                cv = gm[jp:jp + 1, :]
                beats = jnp.logical_or(cv > gm, jnp.logical_and(cv == gm, jp < blk_id))
                rank = rank + jnp.where(beats, 1, 0)
            visible = jnp.logical_or(jnp.logical_and(rank < n_sel, past), blk_id >= q_blk)
            bias_t = jnp.where(visible, 0.0, MASK_VALUE).astype(BF16)
            bias = lax.dot_general(bias_t, eye, (((0,), (0,)), ((), ())), preferred_element_type=F32)
            heads.append(jnp.concatenate([qh, bias.astype(BF16)], axis=1))

        for i in range(nb):
            nk = (i + 1) * blk
            rows = slice(i * blk, (i + 1) * blk)
            q_both = jnp.concatenate([heads[0][rows], heads[1][rows]], axis=0)
            sc = lax.dot_general(q_both, k_aug[:nk], contract_lanes, preferred_element_type=F32)
            sc_own = sc[:, i * blk:] + causal_bias
            m = jnp.max(sc_own, axis=-1, keepdims=True)
            if i > 0:
                sc_past = sc[:, :i * blk]
                m = jnp.maximum(m, jnp.max(sc_past, axis=-1, keepdims=True))
                p = jnp.concatenate([jnp.exp2(sc_past - m), jnp.exp2(sc_own - m)], axis=1)
            else:
                p = jnp.exp2(sc_own - m)
            o = jnp.dot(p.astype(BF16), v_aug[:nk], preferred_element_type=F32)
            o = o[:, :LANES] / o[:, LANES:LANES + 1]
            o_ref[rows, cols] = jnp.where(head_of_lane == 0, o[:blk], o[blk:]).astype(o_ref.dtype)


def _moba(proj, batch, seq):
    n = proj.shape[0]
    width = MOBA_PAIRS_PER_STEP * LANES
    steps = ATT_WIDTH // width
    spec = lambda off: pl.BlockSpec((seq, width), lambda b, p: (b, off + p))
    return pl.pallas_call(
        _moba_kernel,
        grid=(batch, steps),
        in_specs=[spec(0), spec(steps), spec(2 * steps)],
        out_specs=pl.BlockSpec((seq, width), lambda b, p: (b, p)),
        out_shape=jax.ShapeDtypeStruct((n, ATT_WIDTH), BF16),
        compiler_params=_params("arbitrary", "arbitrary"),
    )(proj, proj, proj)


def _retention_kernel(q_ref, k_ref, v_ref, g_ref, dm_ref, qd_ref, kd_ref, cd_ref, og_ref, o_ref):
    s = q_ref.shape[0]
    c = RET_BLOCK
    states = [jnp.zeros((RET_HEAD_DIM, RET_HEAD_DIM), F32) for _ in range(RET_HEADS)]
    for n in range(s // c):
        rows = slice(n * c, (n + 1) * c)
        for hd in range(RET_HEADS):
            cols = slice(hd * RET_HEAD_DIM, (hd + 1) * RET_HEAD_DIM)
            qc = q_ref[rows, cols]
            kc = k_ref[rows, cols]
            vc = v_ref[rows, cols]
            scores = lax.dot_general(qc, kc, (((1,), (1,)), ((), ())), preferred_element_type=F32) * dm_ref[hd]
            inner = jnp.dot(scores.astype(BF16), vc, preferred_element_type=F32)
            cross = jnp.dot(qc, states[hd].astype(BF16), preferred_element_type=F32) * qd_ref[hd]
            o = _rms(inner + cross) * og_ref[hd]
            gate = g_ref[rows, cols].astype(F32)
            o_ref[rows, cols] = (_silu(gate) * o).astype(o_ref.dtype)
            k_dec = (kc.astype(F32) * kd_ref[hd]).astype(BF16)
            kv = lax.dot_general(k_dec, vc, (((0,), (0,)), ((), ())), preferred_element_type=F32)
            states[hd] = states[hd] * cd_ref[hd] + kv


def _retention_tables():
    h, c = RET_HEADS, RET_BLOCK
    lg = jnp.log(1.0 - 2.0 ** (-5.0 - jnp.arange(h, dtype=F32)))
    idx = jnp.arange(c, dtype=F32)
    diff = idx[:, None] - idx[None, :]
    dmask = jnp.where(diff >= 0, jnp.exp(jnp.maximum(diff, 0.0)[None] * lg[:, None, None]), 0.0)
    rows = lambda v: jnp.broadcast_to(v[:, :, None], (h, c, RET_HEAD_DIM))
    qdec = rows(jnp.exp((idx + 1.0)[None, :] * lg[:, None]))
    kdec = rows(jnp.exp((c - 1 - idx)[None, :] * lg[:, None]))
    cdec = jnp.broadcast_to(jnp.exp(c * lg)[:, None, None], (h, RET_HEAD_DIM, RET_HEAD_DIM))
    return dmask, qdec, kdec, cdec


def _retention(proj, ret_out_g, batch, seq):
    n = proj.shape[0]
    base = 3 * ATT_WIDTH // RET_WIDTH
    spec = lambda off: pl.BlockSpec((seq, RET_WIDTH), lambda b: (b, base + off))
    tables = _retention_tables()
    whole = lambda t: pl.BlockSpec(t.shape, lambda b: (0, 0, 0))
    return pl.pallas_call(
        _retention_kernel,
        grid=(batch,),
        in_specs=[spec(0), spec(1), spec(2), spec(3)] + [whole(t) for t in tables]
                 + [pl.BlockSpec((RET_HEADS, 1, RET_HEAD_DIM), lambda b: (0, 0, 0))],
        out_specs=pl.BlockSpec((seq, RET_WIDTH), lambda b: (b, 0)),
        out_shape=jax.ShapeDtypeStruct((n, RET_WIDTH), BF16),
        compiler_params=_params("arbitrary"),
    )(proj, proj, proj, proj, *tables, ret_out_g.reshape(RET_HEADS, 1, RET_HEAD_DIM))


def _top2_route(logits):
    lane = lax.broadcasted_iota(jnp.int32, logits.shape, 1)
    lg = jnp.where(lane < N_EXPERTS, logits, -jnp.inf)
    m1 = jnp.max(lg, axis=-1, keepdims=True)
    i1 = jnp.min(jnp.where(lg == m1, lane, LANES), axis=-1, keepdims=True)
    lg2 = jnp.where(lane == i1, -jnp.inf, lg)
    m2 = jnp.max(lg2, axis=-1, keepdims=True)
    i2 = jnp.min(jnp.where(lg2 == m2, lane, LANES), axis=-1, keepdims=True)
    e = jnp.exp(m2 - m1)
    w1 = 1.0 / (1.0 + e)
    w2 = e / (1.0 + e)
    out = jnp.where(lane == 0, i1.astype(F32), 0.0)
    out = jnp.where(lane == 1, i2.astype(F32), out)
    out = jnp.where(lane == 2, w1, out)
    return jnp.where(lane == 3, w2, out)


def _mix_and_norm(oa_ref, or_ref, x_ref, w_ref, ag_ref, m, ng_ref):
    oa = (_rms(oa_ref[...].astype(F32)) * ag_ref[...]).astype(BF16)
    mix = (jnp.dot(oa, w_ref[:ATT_WIDTH, :], preferred_element_type=F32)
           + jnp.dot(or_ref[...], w_ref[ATT_WIDTH:, :], preferred_element_type=F32))
    x1 = x_ref[...] + m[2:3] * mix
    h = _rms(x1) * ng_ref[...] * (1.0 + m[4:5]) + m[3:4]
    return x1, h


def _mixer_specs(tm, d, per_seq):
    row = lambda w: pl.BlockSpec((tm, w), lambda i: (i, 0))
    full = lambda a, b: pl.BlockSpec((a, b), lambda i: (0, 0))
    return [row(ATT_WIDTH), row(RET_WIDTH), row(d), full(ATT_WIDTH + RET_WIDTH, d), full(1, ATT_WIDTH),
            pl.BlockSpec((1, 6, d), lambda i: (i // per_seq, 0, 0)), full(1, d)]


def _outproj_route_kernel(oa_ref, or_ref, x_ref, w_ref, ag_ref, mod_ref, ng_ref, rw_ref, x1_ref, h_ref, rt_ref):
    x1, h = _mix_and_norm(oa_ref, or_ref, x_ref, w_ref, ag_ref, mod_ref[0], ng_ref)
    x1_ref[...] = x1
    _store_token_tiles(h_ref, h)
    rt_ref[...] = _top2_route(jnp.dot(h.astype(BF16), rw_ref[...], preferred_element_type=F32))


def _outproj_route(o_a, o_r, x2d, w_bf16, att_g, mod, norm_g, seq, router_w):
    n, d = x2d.shape
    tm = _tile(seq, 512, 16)
    sub = d // LANES
    rw = jnp.zeros((d, LANES), BF16).at[:, :N_EXPERTS].set(router_w.astype(BF16))
    return pl.pallas_call(
        _outproj_route_kernel,
        grid=(n // tm,),
        in_specs=_mixer_specs(tm, d, seq // tm) + [pl.BlockSpec((d, LANES), lambda i: (0, 0))],
        out_specs=[pl.BlockSpec((tm, d), lambda i: (i, 0)),
                   pl.BlockSpec((tm * sub, LANES), lambda i: (i, 0)),
                   pl.BlockSpec((tm, LANES), lambda i: (i, 0))],
        out_shape=[jax.ShapeDtypeStruct((n, d), F32), jax.ShapeDtypeStruct((n * sub, LANES), F32),
                   jax.ShapeDtypeStruct((n, LANES), F32)],
        compiler_params=_params("arbitrary"),
    )(o_a, o_r, x2d, w_bf16, att_g.reshape(1, ATT_WIDTH), mod, norm_g.reshape(1, d), rw)


def _swiglu(h, wg_ref, wu_ref, wd_ref, act_ref):
    ff = act_ref.shape[1]
    tf = _tile(ff, V7X_MXU_WIDTH, LANES)
    for c0 in range(0, ff, tf):
        g = jnp.dot(h, wg_ref[:, c0:c0 + tf], preferred_element_type=F32)
        u = jnp.dot(h, wu_ref[:, c0:c0 + tf], preferred_element_type=F32)
        act_ref[:, c0:c0 + tf] = (_silu(g) * u).astype(act_ref.dtype)
    return jnp.dot(act_ref[...], wd_ref[...], preferred_element_type=F32)


def _outproj_ffn_kernel(oa_ref, or_ref, x_ref, w_ref, ag_ref, mod_ref, ng_ref, wg_ref, wu_ref, wd_ref, fg_ref,
                        o_ref, act_ref, *, final):
    m = mod_ref[0]
    x1, h = _mix_and_norm(oa_ref, or_ref, x_ref, w_ref, ag_ref, m, ng_ref)
    y = _swiglu(h.astype(BF16), wg_ref, wu_ref, wd_ref, act_ref)
    x2 = x1 + m[5:6] * y
    o_ref[...] = _rms(x2) * fg_ref[...] if final else x2


def _outproj_ffn(o_a, o_r, x2d, w_bf16, att_g, mod, norm_g, wg, wu, wd, final_g, seq, final):
    n, d = x2d.shape
    ff = wg.shape[1]
    tm = _tile(seq, FFN_ROWS, 16)
    whole = lambda t: pl.BlockSpec(t.shape, lambda i: (0, 0))
    return pl.pallas_call(
        functools.partial(_outproj_ffn_kernel, final=final),
        grid=(n // tm,),
        in_specs=_mixer_specs(tm, d, seq // tm) + [whole(wg), whole(wu), whole(wd), pl.BlockSpec((1, d), lambda i: (0, 0))],
        out_specs=pl.BlockSpec((tm, d), lambda i: (i, 0)),
        out_shape=jax.ShapeDtypeStruct((n, d), F32),
        scratch_shapes=[pltpu.VMEM((tm, ff), BF16)],
        compiler_params=_params("arbitrary"),
    )(o_a, o_r, x2d, w_bf16, att_g.reshape(1, ATT_WIDTH), mod, norm_g.reshape(1, d), wg, wu, wd,
      final_g.reshape(1, d))


def _moe_ffn_kernel(blk_e_ref, n_used_ref, x_ref, wg_ref, wu_ref, wd_ref, o_ref, xs_ref, act_ref):
    del blk_e_ref
    g, d = xs_ref.shape

    @pl.when(pl.program_id(0) < n_used_ref[0])
    def _():
        xs_ref[...] = _load_token_tiles(x_ref, g, d).astype(BF16)
        _store_token_tiles(o_ref, _swiglu(xs_ref[...], wg_ref.at[0], wu_ref.at[0], wd_ref.at[0], act_ref))

    @pl.when(pl.program_id(0) >= n_used_ref[0])
    def _():
        o_ref[...] = jnp.zeros_like(o_ref)


def _moe_grouped(xb, blk_e, n_used, wg, wu, wd):
    _, d, ff = wg.shape
    sub = d // LANES
    g = FFN_ROWS
    grid_spec = pltpu.PrefetchScalarGridSpec(
        num_scalar_prefetch=2,
        grid=(xb.shape[0] // (g * sub),),
        in_specs=[
            pl.BlockSpec((g * sub, LANES), lambda b, be, nu: (b, 0)),
            pl.BlockSpec((1, d, ff), lambda b, be, nu: (be[b], 0, 0)),
            pl.BlockSpec((1, d, ff), lambda b, be, nu: (be[b], 0, 0)),
            pl.BlockSpec((1, ff, d), lambda b, be, nu: (be[b], 0, 0)),
        ],
        out_specs=pl.BlockSpec((g * sub, LANES), lambda b, be, nu: (b, 0)),
        scratch_shapes=[pltpu.VMEM((g, d), BF16), pltpu.VMEM((g, ff), BF16)],
    )
    return pl.pallas_call(
        _moe_ffn_kernel,
        grid_spec=grid_spec,
        out_shape=jax.ShapeDtypeStruct(xb.shape, F32),
        compiler_params=_params("arbitrary"),
    )(blk_e, n_used, xb, wg, wu, wd)


def _start_run_copies(src_ref, dst_ref, src_row, dst_row, count, sub, sem, wait=False):
    for bit in range(MOE_RUN_BITS):
        size = 1 << bit
        off = jnp.left_shift(jnp.right_shift(count, bit + 1), bit + 1)

        @pl.when(jnp.bitwise_and(jnp.right_shift(count, bit), 1) == 1)
        def _():
            copy = pltpu.make_async_copy(
                src_ref.at[pl.ds(pl.multiple_of((src_row + off) * sub, sub), size * sub)],
                dst_ref.at[pl.ds(pl.multiple_of((dst_row + off) * sub, sub), size * sub)], sem)
            copy.wait() if wait else copy.start()


def _dispatch_kernel(pos_ref, cnt_ref, glob_ref, loc_ref, pad_row_ref, pad_len_ref, h_ref, xb_ref,
                     sbuf, zbuf, sems, *, sub):
    i = pl.program_id(0)
    last = pl.num_programs(0) - 1
    tm = h_ref.shape[0] // sub
    ta = tm * TOP_K
    slot = lax.rem(i, 2)
    g = zbuf.shape[0] // sub
    blocks = xb_ref.shape[0] // (g * sub)

    def zero_fill(wait):
        for e in range(N_EXPERTS):
            _start_run_copies(zbuf, xb_ref, 0, pad_row_ref[e], pad_len_ref[e], sub, sems.at[2], wait)
            unused = pad_len_ref[N_EXPERTS] + e

            @pl.when(unused < blocks)
            def _():
                copy = pltpu.make_async_copy(
                    zbuf, xb_ref.at[pl.ds(pl.multiple_of(unused * (g * sub), g * sub), g * sub)], sems.at[2])
                copy.wait() if wait else copy.start()

    @pl.when(i == 0)
    def _():
        zbuf[...] = jnp.zeros_like(zbuf)
        zero_fill(False)

    def wait_slot(s):
        pltpu.make_async_copy(sbuf.at[s], xb_ref.at[pl.ds(0, ta * sub)], sems.at[s]).wait()

    @pl.when(i >= 2)
    def _():
        wait_slot(slot)

    def place(t, carry):
        v = _token_tile(h_ref, t, sub)[...]
        for k in range(TOP_K):
            _token_tile(sbuf.at[slot], pos_ref[i * ta + TOP_K * t + k], sub)[...] = v
        return carry

    lax.fori_loop(0, tm, place, 0, unroll=8)

    for e in range(N_EXPERTS):
        r = i * N_EXPERTS + e
        _start_run_copies(sbuf.at[slot], xb_ref, loc_ref[r], glob_ref[r], cnt_ref[r], sub, sems.at[slot])

    @pl.when(i == last)
    def _():
        @pl.when(i >= 1)
        def _():
            wait_slot(1 - slot)
        wait_slot(slot)
        zero_fill(True)


def _dispatch(h_tiles, layout, seq, d):
    sub = d // LANES
    n = h_tiles.shape[0] // sub
    tm = _tile(seq, MOE_TILE_TOKENS, 8)
    grid_spec = pltpu.PrefetchScalarGridSpec(
        num_scalar_prefetch=6,
        grid=(n // tm,),
        in_specs=[pl.BlockSpec((tm * sub, LANES), lambda i, *_: (i, 0))],
        out_specs=pl.BlockSpec(memory_space=pl.ANY),
        scratch_shapes=[pltpu.VMEM((2, tm * TOP_K * sub, LANES), h_tiles.dtype),
                        pltpu.VMEM((FFN_ROWS * sub, LANES), h_tiles.dtype),
                        pltpu.SemaphoreType.DMA((3,))],
    )
    return pl.pallas_call(
        functools.partial(_dispatch_kernel, sub=sub),
        grid_spec=grid_spec,
        out_shape=jax.ShapeDtypeStruct((layout["p_total"] * sub, LANES), h_tiles.dtype),
        compiler_params=_params("arbitrary"),
    )(layout["pos"], layout["cnt"], layout["glob"], layout["loc"], layout["pad_row"], layout["pad_len"], h_tiles)


def _combine_kernel(pos_ref, cnt_ref, glob_ref, loc_ref, x1_ref, rt_ref, mod_ref, fg_ref, yb_ref, o_ref,
                    ybuf, tbuf, sems, *, final):
    i = pl.program_id(0)
    tm, d = x1_ref.shape
    sub = d // LANES
    ta = tm * TOP_K
    slot = lax.rem(i, 2)

    def fetch(tile, s):
        for e in range(N_EXPERTS):
            r = tile * N_EXPERTS + e
            _start_run_copies(yb_ref, ybuf.at[s], glob_ref[r], loc_ref[r], cnt_ref[r], sub, sems.at[s])

    @pl.when(i == 0)
    def _():
        fetch(0, 0)

    @pl.when(i + 1 < pl.num_programs(0))
    def _():
        fetch(i + 1, 1 - slot)

    pltpu.make_async_copy(yb_ref.at[pl.ds(0, ta * sub)], ybuf.at[slot], sems.at[slot]).wait()

    def unpermute(t, carry):
        for k in range(TOP_K):
            _token_tile(tbuf.at[k], t, sub)[...] = _token_tile(ybuf.at[slot], pos_ref[i * ta + TOP_K * t + k], sub)[...]
        return carry

    lax.fori_loop(0, tm, unpermute, 0, unroll=8)

    rt = rt_ref[...]
    y = (rt[:, 2:3] * _load_token_tiles(tbuf.at[0], tm, d) + rt[:, 3:4] * _load_token_tiles(tbuf.at[1], tm, d))
    x2 = x1_ref[...] + mod_ref[0][5:6] * y
    o_ref[...] = _rms(x2) * fg_ref[...] if final else x2


def _combine(x1, yb, layout, route, mod, final_g, seq, final):
    n, d = x1.shape
    sub = d // LANES
    tm = _tile(seq, MOE_TILE_TOKENS, 8)
    per_seq = seq // tm
    grid_spec = pltpu.PrefetchScalarGridSpec(
        num_scalar_prefetch=4,
        grid=(n // tm,),
        in_specs=[
            pl.BlockSpec((tm, d), lambda i, *_: (i, 0)),
            pl.BlockSpec((tm, LANES), lambda i, *_: (i, 0)),
            pl.BlockSpec((1, 6, d), lambda i, *_: (i // per_seq, 0, 0)),
            pl.BlockSpec((1, d), lambda i, *_: (0, 0)),
            pl.BlockSpec(memory_space=pl.ANY),
        ],
        out_specs=pl.BlockSpec((tm, d), lambda i, *_: (i, 0)),
        scratch_shapes=[pltpu.VMEM((2, TOP_K * tm * sub, LANES), F32),
                        pltpu.VMEM((TOP_K, tm * sub, LANES), F32),
                        pltpu.SemaphoreType.DMA((2,))],
    )
    return pl.pallas_call(
        functools.partial(_combine_kernel, final=final),
        grid_spec=grid_spec,
        out_shape=jax.ShapeDtypeStruct((n, d), F32),
        compiler_params=_params("arbitrary"),
    )(layout["pos"], layout["cnt"], layout["glob"], layout["loc"], x1, route, mod, final_g.reshape(1, d), yb)


def _moe_layout(route, n, seq):
    g = FFN_ROWS
    a = n * TOP_K
    ta = _tile(seq, MOE_TILE_TOKENS, 8) * TOP_K
    tiles = a // ta
    assert ta < (1 << MOE_RUN_BITS)
    e_flat = route[:, :TOP_K].astype(jnp.int32).reshape(-1)
    onehot = (e_flat[:, None] == jnp.arange(N_EXPERTS, dtype=jnp.int32)[None, :]).astype(jnp.int32)
    csum = jnp.cumsum(onehot, axis=0)
    rank = jnp.sum(csum * onehot, axis=1) - 1
    counts = csum[-1]
    padded = (counts + g - 1) // g * g
    pad_ends = jnp.cumsum(padded)
    pad_starts = pad_ends - padded
    dest = jnp.sum(pad_starts[None, :] * onehot, axis=1) + rank
    tile_end = csum[ta - 1::ta]
    cnt = tile_end - jnp.concatenate([jnp.zeros((1, N_EXPERTS), jnp.int32), tile_end[:-1]], axis=0)
    glob = pad_starts[None, :] + tile_end - cnt
    loc = jnp.cumsum(cnt, axis=1) - cnt
    per_assignment = lambda v: jnp.sum(
        jnp.broadcast_to(v[:, None, :], (tiles, ta, N_EXPERTS)).reshape(a, N_EXPERTS) * onehot, axis=1)
    pos = per_assignment(loc) + dest - per_assignment(glob)
    p_total = (-(-a // g) + N_EXPERTS) * g
    blk_start = jnp.arange(p_total // g, dtype=jnp.int32) * g
    blk_e = jnp.minimum(jnp.sum((pad_ends[None, :] <= blk_start[:, None]).astype(jnp.int32), axis=1),
                        N_EXPERTS - 1)
    i32 = lambda v: v.astype(jnp.int32).reshape(-1)
    n_used = pad_ends[-1:] // g
    return dict(pos=i32(pos), cnt=i32(cnt), glob=i32(glob), loc=i32(loc), blk_e=i32(blk_e), n_used=i32(n_used),
                pad_row=i32(pad_starts + counts), pad_len=i32(jnp.concatenate([padded - counts, n_used])),
                p_total=p_total)


def _moe_ffn(h_tiles, x1, route, mod, wg, wu, wd, final_g, seq, final):
    n, d = x1.shape
    layout = _moe_layout(route, n, seq)
    xb = _dispatch(h_tiles, layout, seq, d)
    yb = _moe_grouped(xb, layout["blk_e"], layout["n_used"], wg, wu, wd)
    return _combine(x1, yb, layout, route, mod, final_g, seq, final)


def kernel(x, c, norm_mix_g, norm_ffn_g, ada_w, ada_b, w_in, w_out, att_out_g, ret_out_g, ffn_w_gate,
           ffn_w_up, ffn_w_down, router_w, moe_w_gate, moe_w_up, moe_w_down, final_norm_g):
    batch, seq, d = x.shape
    depth = ada_w.shape[0]
    assert seq % MOBA_BLOCK == 0 and seq % RET_BLOCK == 0
    assert w_in.shape[2] == N_GROUPS * GROUP_WIDTH
    mods = _adaln(c, ada_w, ada_b).reshape(depth, batch, 6, d)
    tables = _rotary_tables(seq)
    xf = x.reshape(batch * seq, d)
    for l in range(depth):
        mod = mods[l]
        final = l == depth - 1
        proj = _inproj(xf, mod, norm_mix_g[l], w_in[l].astype(BF16), tables, seq)
        o_a = _moba(proj, batch, seq)
        o_r = _retention(proj, ret_out_g[l], batch, seq)
        if l % 2 == 0:
            xf = _outproj_ffn(o_a, o_r, xf, w_out[l].astype(BF16), att_out_g[l], mod, norm_ffn_g[l],
                              ffn_w_gate[l // 2].astype(BF16), ffn_w_up[l // 2].astype(BF16),
                              ffn_w_down[l // 2].astype(BF16), final_norm_g, seq, final)
        else:
            x1, h, route = _outproj_route(o_a, o_r, xf, w_out[l].astype(BF16), att_out_g[l], mod, norm_ffn_g[l],
                                          seq, router_w[l // 2])
            xf = _moe_ffn(h, x1, route, mod, moe_w_gate[l // 2].astype(BF16), moe_w_up[l // 2].astype(BF16),
                          moe_w_down[l // 2].astype(BF16), final_norm_g, seq, final)
    return xf.reshape(batch, seq, d)
```

```python
import functools

import jax
import jax.numpy as jnp
from jax import lax
from jax.experimental import pallas as pl
from jax.experimental.pallas import tpu as pltpu

F32 = jnp.float32
BF16 = jnp.bfloat16

LANES = 128
V7X_MXU_WIDTH = 256
V7X_VMEM_LIMIT_BYTES = 56 * 1024 * 1024

ATT_HEADS = 8
ATT_HEAD_DIM = 64
ATT_WIDTH = ATT_HEADS * ATT_HEAD_DIM
MOBA_BLOCK = 256
MOBA_TOPK = 3
MOBA_PAIRS_PER_STEP = 2
MOBA_GATE_ROWS = 16
MASK_VALUE = -1e30
MOBA_Q_SCALE = ATT_HEAD_DIM ** -0.5 * 1.4426950408889634
RET_HEADS = 4
RET_HEAD_DIM = 128
RET_WIDTH = RET_HEADS * RET_HEAD_DIM
RET_BLOCK = 256
ROPE_BASE = 10000.0
N_EXPERTS = 8
TOP_K = 2
EPS = 1e-6
GROUP_WIDTH = 512
N_GROUPS = 7
MOE_TILE_TOKENS = 512
MOE_RUN_BITS = 11
FFN_ROWS = 512


def _tile(total, target, mult):
    best = None
    t = mult
    while t <= min(total, target):
        if total % t == 0:
            best = t
        t += mult
    return best if best is not None else total


def _params(*sem):
    return pltpu.CompilerParams(dimension_semantics=sem, vmem_limit_bytes=V7X_VMEM_LIMIT_BYTES)


def _rms(x):
    return x * lax.rsqrt(jnp.mean(x * x, axis=-1, keepdims=True) + EPS)


def _silu(x):
    return x * jax.nn.sigmoid(x)


def _store_token_tiles(ref, x, row0=0):
    rows, d = x.shape
    sub = d // LANES
    for s in range(sub):
        ref[pl.ds(row0 * sub + s, rows, stride=sub), :] = x[:, s * LANES:(s + 1) * LANES].astype(ref.dtype)


def _load_token_tiles(ref, rows, d, row0=0):
    sub = d // LANES
    return jnp.concatenate([ref[pl.ds(row0 * sub + s, rows, stride=sub), :] for s in range(sub)], axis=1)


def _token_tile(ref, t, sub):
    return ref.at[pl.ds(pl.multiple_of(t * sub, sub), sub)]


def _adaln_kernel(c_ref, w_ref, b_ref, o_ref):
    cs = _silu(c_ref[...]).astype(BF16)
    o_ref[0] = jnp.dot(cs, w_ref[0].astype(BF16), preferred_element_type=F32) + b_ref[0]


def _adaln(c, ada_w, ada_b):
    depth, d, w = ada_w.shape
    b = c.shape[0]
    tn = _tile(w, 1536, LANES)
    return pl.pallas_call(
        _adaln_kernel,
        grid=(depth, w // tn),
        in_specs=[
            pl.BlockSpec((b, d), lambda l, j: (0, 0)),
            pl.BlockSpec((1, d, tn), lambda l, j: (l, 0, j)),
            pl.BlockSpec((1, 1, tn), lambda l, j: (l, 0, j)),
        ],
        out_specs=pl.BlockSpec((1, b, tn), lambda l, j: (l, 0, j)),
        out_shape=jax.ShapeDtypeStruct((depth, b, w), F32),
        compiler_params=_params("arbitrary", "arbitrary"),
    )(c, ada_w, ada_b.reshape(depth, 1, w))


def _inproj_kernel(x_ref, mod_ref, g_ref, w_ref, cq_ref, sq_ref, ck_ref, sk_ref, o_ref):
    m = mod_ref[0]
    h = (_rms(x_ref[...]) * g_ref[...] * (1.0 + m[1:2]) + m[0:1]).astype(BF16)
    rotary = {3: (cq_ref, sq_ref), 4: (ck_ref, sk_ref)}
    for j in range(N_GROUPS):
        c0 = j * GROUP_WIDTH
        acc = jnp.dot(h, w_ref[:, c0:c0 + GROUP_WIDTH], preferred_element_type=F32)
        if j in rotary:
            cos = rotary[j][0][...]
            sin = rotary[j][1][...]
            for hd in range(RET_HEADS):
                a = acc[:, hd * RET_HEAD_DIM:(hd + 1) * RET_HEAD_DIM]
                r = a * cos + pltpu.roll(a, RET_HEAD_DIM // 2, 1) * sin
                o_ref[:, c0 + hd * RET_HEAD_DIM:c0 + (hd + 1) * RET_HEAD_DIM] = r.astype(o_ref.dtype)
        elif j == 0:
            o_ref[:, c0:c0 + GROUP_WIDTH] = (acc * MOBA_Q_SCALE).astype(o_ref.dtype)
        else:
            o_ref[:, c0:c0 + GROUP_WIDTH] = acc.astype(o_ref.dtype)


def _rotary_tables(s):
    half = RET_HEAD_DIM // 2
    inv_freq = ROPE_BASE ** (-jnp.arange(half, dtype=F32) / half)
    ang = jnp.arange(s, dtype=F32)[:, None] * inv_freq[None, :]
    cos = jnp.cos(ang)
    sin = jnp.sin(ang)
    cos2 = jnp.concatenate([cos, cos], axis=-1)
    sin2 = jnp.concatenate([-sin, sin], axis=-1)
    k_scale = RET_HEAD_DIM ** -0.5
    return cos2, sin2, cos2 * k_scale, sin2 * k_scale


def _inproj(x2d, mod, g, w_bf16, tables, seq):
    n, d = x2d.shape
    tm = _tile(seq, 1024, 16)
    per_seq = seq // tm
    width = N_GROUPS * GROUP_WIDTH
    return pl.pallas_call(
        _inproj_kernel,
        grid=(n // tm,),
        in_specs=[
            pl.BlockSpec((tm, d), lambda i: (i, 0)),
            pl.BlockSpec((1, 6, d), lambda i: (i // per_seq, 0, 0)),
            pl.BlockSpec((1, d), lambda i: (0, 0)),
            pl.BlockSpec((d, width), lambda i: (0, 0)),
        ] + [pl.BlockSpec((tm, RET_HEAD_DIM), lambda i: (i % per_seq, 0))] * 4,
        out_specs=pl.BlockSpec((tm, width), lambda i: (i, 0)),
        out_shape=jax.ShapeDtypeStruct((n, width), BF16),
        compiler_params=_params("arbitrary"),
    )(x2d, mod, g.reshape(1, d), w_bf16, *tables)


def _cast_rider_specs(rider, steps, index):
    rows, cols = rider.shape
    assert rows % steps == 0 and (rows // steps) % 16 == 0
    spec = pl.BlockSpec((rows // steps, cols), index)
    return spec, spec, jax.ShapeDtypeStruct(rider.shape, BF16)


def _moba_kernel(q_ref, k_ref, v_ref, *rest):
    if len(rest) == 3:
        w_ref, o_ref, w_bf16_ref = rest
        w_bf16_ref[...] = w_ref[...].astype(BF16)
    else:
        o_ref, = rest
    s = q_ref.shape[0]
    blk = MOBA_BLOCK
    nb = s // blk
    assert nb <= MOBA_GATE_ROWS
    n_sel = min(MOBA_TOPK, nb - 1)
    shift = blk.bit_length() - 1
    contract_lanes = (((1,), (1,)), ((), ()))
    lane = lax.broadcasted_iota(jnp.int32, (1, LANES), 1)
    head_of_lane = jnp.right_shift(lane, ATT_HEAD_DIM.bit_length() - 1)

    key_pos = lax.broadcasted_iota(jnp.int32, (s, LANES), 0)
    key_lane = lax.broadcasted_iota(jnp.int32, (s, LANES), 1)
    key_block_onehot = jnp.where(jnp.right_shift(key_pos, shift) == key_lane, 1.0, 0.0).astype(BF16)
    ones = jnp.ones((s, LANES), BF16)

    blk_id = lax.broadcasted_iota(jnp.int32, (MOBA_GATE_ROWS, s), 0)
    q_blk = jnp.right_shift(lax.broadcasted_iota(jnp.int32, (MOBA_GATE_ROWS, s), 1), shift)
    past = blk_id < q_blk
    block_mean = jnp.where(q_blk == blk_id, 1.0 / blk, 0.0).astype(BF16)
    eye = jnp.where(lax.broadcasted_iota(jnp.int32, (MOBA_GATE_ROWS, LANES), 0)
                    == lax.broadcasted_iota(jnp.int32, (MOBA_GATE_ROWS, LANES), 1), 1.0, 0.0).astype(BF16)
    row = lax.broadcasted_iota(jnp.int32, (2 * blk, blk), 0)
    col = lax.broadcasted_iota(jnp.int32, (2 * blk, blk), 1)
    causal_bias = jnp.where(col <= jnp.bitwise_and(row, blk - 1), 0.0, MASK_VALUE)

    for pair in range(q_ref.shape[1] // LANES):
        cols = slice(pair * LANES, (pair + 1) * LANES)
        q_all = q_ref[:, cols]
        k_all = k_ref[:, cols]
        k_aug = jnp.concatenate([k_all, key_block_onehot], axis=1)
        v_aug = jnp.concatenate([v_ref[:, cols], ones], axis=1)
        k_mean = jnp.dot(block_mean, k_all, preferred_element_type=F32).astype(BF16)

        heads = []
        for hh in range(2):
            qh = jnp.where(head_of_lane == hh, q_all, jnp.zeros_like(q_all))
            gate = lax.dot_general(k_mean, qh, contract_lanes, preferred_element_type=F32)
            gm = jnp.where(past, gate, MASK_VALUE)
            rank = jnp.zeros(gm.shape, jnp.int32)
            for jp in range(nb):
                cv = gm[jp:jp + 1, :]
                beats = jnp.logical_or(cv > gm, jnp.logical_and(cv == gm, jp < blk_id))
                rank = rank + jnp.where(beats, 1, 0)
            visible = jnp.logical_or(jnp.logical_and(rank < n_sel, past), blk_id >= q_blk)
            bias_t = jnp.where(visible, 0.0, MASK_VALUE).astype(BF16)
            bias = lax.dot_general(bias_t, eye, (((0,), (0,)), ((), ())), preferred_element_type=F32)
            heads.append(jnp.concatenate([qh, bias.astype(BF16)], axis=1))

        for i in range(nb):
            nk = (i + 1) * blk
            rows = slice(i * blk, (i + 1) * blk)
            q_both = jnp.concatenate([heads[0][rows], heads[1][rows]], axis=0)
            sc = lax.dot_general(q_both, k_aug[:nk], contract_lanes, preferred_element_type=F32)
            sc_own = sc[:, i * blk:] + causal_bias
            m = jnp.max(sc_own, axis=-1, keepdims=True)
            if i > 0:
                sc_past = sc[:, :i * blk]
                m = jnp.maximum(m, jnp.max(sc_past, axis=-1, keepdims=True))
                p = jnp.concatenate([jnp.exp2(sc_past - m), jnp.exp2(sc_own - m)], axis=1)
            else:
                p = jnp.exp2(sc_own - m)
            o = jnp.dot(p.astype(BF16), v_aug[:nk], preferred_element_type=F32)
            o = o[:, :LANES] / o[:, LANES:LANES + 1]
            o_ref[rows, cols] = jnp.where(head_of_lane == 0, o[:blk], o[blk:]).astype(o_ref.dtype)


def _moba(proj, batch, seq, rider=None):
    n = proj.shape[0]
    width = MOBA_PAIRS_PER_STEP * LANES
    steps = ATT_WIDTH // width
    spec = lambda off: pl.BlockSpec((seq, width), lambda b, p: (b, off + p))
    in_specs = [spec(0), spec(steps), spec(2 * steps)]
    out_specs = [pl.BlockSpec((seq, width), lambda b, p: (b, p))]
    out_shape = [jax.ShapeDtypeStruct((n, ATT_WIDTH), BF16)]
    args = [proj, proj, proj]
    if rider is not None:
        r_in, r_out, r_shape = _cast_rider_specs(rider, batch * steps, lambda b, p: (b * steps + p, 0))
        in_specs.append(r_in)
        out_specs.append(r_out)
        out_shape.append(r_shape)
        args.append(rider)
    out = pl.pallas_call(
        _moba_kernel,
        grid=(batch, steps),
        in_specs=in_specs,
        out_specs=out_specs,
        out_shape=out_shape,
        compiler_params=_params("arbitrary", "arbitrary"),
    )(*args)
    return out if rider is not None else out[0]


def _retention_kernel(q_ref, k_ref, v_ref, g_ref, dm_ref, qd_ref, kd_ref, cd_ref, og_ref, o_ref):
    s = q_ref.shape[0]
    c = RET_BLOCK
    states = [jnp.zeros((RET_HEAD_DIM, RET_HEAD_DIM), F32) for _ in range(RET_HEADS)]
    for n in range(s // c):
        rows = slice(n * c, (n + 1) * c)
        for hd in range(RET_HEADS):
            cols = slice(hd * RET_HEAD_DIM, (hd + 1) * RET_HEAD_DIM)
            qc = q_ref[rows, cols]
            kc = k_ref[rows, cols]
            vc = v_ref[rows, cols]
            scores = lax.dot_general(qc, kc, (((1,), (1,)), ((), ())), preferred_element_type=F32) * dm_ref[hd]
            inner = jnp.dot(scores.astype(BF16), vc, preferred_element_type=F32)
            cross = jnp.dot(qc, states[hd].astype(BF16), preferred_element_type=F32) * qd_ref[hd]
            o = _rms(inner + cross) * og_ref[hd]
            gate = g_ref[rows, cols].astype(F32)
            o_ref[rows, cols] = (_silu(gate) * o).astype(o_ref.dtype)
            k_dec = (kc.astype(F32) * kd_ref[hd]).astype(BF16)
            kv = lax.dot_general(k_dec, vc, (((0,), (0,)), ((), ())), preferred_element_type=F32)
            states[hd] = states[hd] * cd_ref[hd] + kv


def _retention_tables():
    h, c = RET_HEADS, RET_BLOCK
    lg = jnp.log(1.0 - 2.0 ** (-5.0 - jnp.arange(h, dtype=F32)))
    idx = jnp.arange(c, dtype=F32)
    diff = idx[:, None] - idx[None, :]
    dmask = jnp.where(diff >= 0, jnp.exp(jnp.maximum(diff, 0.0)[None] * lg[:, None, None]), 0.0)
    rows = lambda v: jnp.broadcast_to(v[:, :, None], (h, c, RET_HEAD_DIM))
    qdec = rows(jnp.exp((idx + 1.0)[None, :] * lg[:, None]))
    kdec = rows(jnp.exp((c - 1 - idx)[None, :] * lg[:, None]))
    cdec = jnp.broadcast_to(jnp.exp(c * lg)[:, None, None], (h, RET_HEAD_DIM, RET_HEAD_DIM))
    return dmask, qdec, kdec, cdec


def _retention(proj, ret_out_g, batch, seq):
    n = proj.shape[0]
    base = 3 * ATT_WIDTH // RET_WIDTH
    spec = lambda off: pl.BlockSpec((seq, RET_WIDTH), lambda b: (b, base + off))
    tables = _retention_tables()
    whole = lambda t: pl.BlockSpec(t.shape, lambda b: (0, 0, 0))
    return pl.pallas_call(
        _retention_kernel,
        grid=(batch,),
        in_specs=[spec(0), spec(1), spec(2), spec(3)] + [whole(t) for t in tables]
                 + [pl.BlockSpec((RET_HEADS, 1, RET_HEAD_DIM), lambda b: (0, 0, 0))],
        out_specs=pl.BlockSpec((seq, RET_WIDTH), lambda b: (b, 0)),
        out_shape=jax.ShapeDtypeStruct((n, RET_WIDTH), BF16),
        compiler_params=_params("arbitrary"),
    )(proj, proj, proj, proj, *tables, ret_out_g.reshape(RET_HEADS, 1, RET_HEAD_DIM))


def _top2_route(logits):
    lane = lax.broadcasted_iota(jnp.int32, logits.shape, 1)
    lg = jnp.where(lane < N_EXPERTS, logits, -jnp.inf)
    m1 = jnp.max(lg, axis=-1, keepdims=True)
    i1 = jnp.min(jnp.where(lg == m1, lane, LANES), axis=-1, keepdims=True)
    lg2 = jnp.where(lane == i1, -jnp.inf, lg)
    m2 = jnp.max(lg2, axis=-1, keepdims=True)
    i2 = jnp.min(jnp.where(lg2 == m2, lane, LANES), axis=-1, keepdims=True)
    e = jnp.exp(m2 - m1)
    w1 = 1.0 / (1.0 + e)
    w2 = e / (1.0 + e)
    out = jnp.where(lane == 0, i1.astype(F32), 0.0)
    out = jnp.where(lane == 1, i2.astype(F32), out)
    out = jnp.where(lane == 2, w1, out)
    return jnp.where(lane == 3, w2, out)


def _mix_and_norm(oa_ref, or_ref, x_ref, w_ref, ag_ref, m, ng_ref):
    oa = (_rms(oa_ref[...].astype(F32)) * ag_ref[...]).astype(BF16)
    mix = (jnp.dot(oa, w_ref[:ATT_WIDTH, :], preferred_element_type=F32)
           + jnp.dot(or_ref[...], w_ref[ATT_WIDTH:, :], preferred_element_type=F32))
    x1 = x_ref[...] + m[2:3] * mix
    h = _rms(x1) * ng_ref[...] * (1.0 + m[4:5]) + m[3:4]
    return x1, h


def _mixer_specs(tm, d, per_seq):
    row = lambda w: pl.BlockSpec((tm, w), lambda i: (i, 0))
    full = lambda a, b: pl.BlockSpec((a, b), lambda i: (0, 0))
    return [row(ATT_WIDTH), row(RET_WIDTH), row(d), full(ATT_WIDTH + RET_WIDTH, d), full(1, ATT_WIDTH),
            pl.BlockSpec((1, 6, d), lambda i: (i // per_seq, 0, 0)), full(1, d)]


def _outproj_route_kernel(oa_ref, or_ref, x_ref, w_ref, ag_ref, mod_ref, ng_ref, rw_ref, x1_ref, h_ref, rt_ref):
    x1, h = _mix_and_norm(oa_ref, or_ref, x_ref, w_ref, ag_ref, mod_ref[0], ng_ref)
    x1_ref[...] = x1
    _store_token_tiles(h_ref, h)
    rt_ref[...] = _top2_route(jnp.dot(h.astype(BF16), rw_ref[...], preferred_element_type=F32))


def _outproj_route(o_a, o_r, x2d, w_bf16, att_g, mod, norm_g, seq, router_w):
    n, d = x2d.shape
    tm = _tile(seq, 512, 16)
    sub = d // LANES
    rw = jnp.zeros((d, LANES), BF16).at[:, :N_EXPERTS].set(router_w.astype(BF16))
    return pl.pallas_call(
        _outproj_route_kernel,
        grid=(n // tm,),
        in_specs=_mixer_specs(tm, d, seq // tm) + [pl.BlockSpec((d, LANES), lambda i: (0, 0))],
        out_specs=[pl.BlockSpec((tm, d), lambda i: (i, 0)),
                   pl.BlockSpec((tm * sub, LANES), lambda i: (i, 0)),
                   pl.BlockSpec((tm, LANES), lambda i: (i, 0))],
        out_shape=[jax.ShapeDtypeStruct((n, d), F32), jax.ShapeDtypeStruct((n * sub, LANES), F32),
                   jax.ShapeDtypeStruct((n, LANES), F32)],
        compiler_params=_params("arbitrary"),
    )(o_a, o_r, x2d, w_bf16, att_g.reshape(1, ATT_WIDTH), mod, norm_g.reshape(1, d), rw)


def _swiglu(h, wg_ref, wu_ref, wd_ref, act_ref):
    ff = act_ref.shape[1]
    tf = _tile(ff, V7X_MXU_WIDTH, LANES)
    for c0 in range(0, ff, tf):
        g = jnp.dot(h, wg_ref[:, c0:c0 + tf], preferred_element_type=F32)
        u = jnp.dot(h, wu_ref[:, c0:c0 + tf], preferred_element_type=F32)
        act_ref[:, c0:c0 + tf] = (_silu(g) * u).astype(act_ref.dtype)
    return jnp.dot(act_ref[...], wd_ref[...], preferred_element_type=F32)


def _outproj_ffn_kernel(oa_ref, or_ref, x_ref, w_ref, ag_ref, mod_ref, ng_ref, wg_ref, wu_ref, wd_ref, fg_ref,
                        *rest, final):
    if len(rest) == 4:
        rider_ref, o_ref, rider_bf16_ref, act_ref = rest
        rider_bf16_ref[...] = rider_ref[...].astype(BF16)
    else:
        o_ref, act_ref = rest
    m = mod_ref[0]
    x1, h = _mix_and_norm(oa_ref, or_ref, x_ref, w_ref, ag_ref, m, ng_ref)
    y = _swiglu(h.astype(BF16), wg_ref, wu_ref, wd_ref, act_ref)
    x2 = x1 + m[5:6] * y
    o_ref[...] = _rms(x2) * fg_ref[...] if final else x2


def _outproj_ffn(o_a, o_r, x2d, w_bf16, att_g, mod, norm_g, wg, wu, wd, final_g, seq, final, rider=None):
    n, d = x2d.shape
    ff = wg.shape[1]
    tm = _tile(seq, FFN_ROWS, 16)
    whole = lambda t: pl.BlockSpec(t.shape, lambda i: (0, 0))
    in_specs = _mixer_specs(tm, d, seq // tm) + [whole(wg), whole(wu), whole(wd), pl.BlockSpec((1, d), lambda i: (0, 0))]
    out_specs = [pl.BlockSpec((tm, d), lambda i: (i, 0))]
    out_shape = [jax.ShapeDtypeStruct((n, d), F32)]
    args = [o_a, o_r, x2d, w_bf16, att_g.reshape(1, ATT_WIDTH), mod, norm_g.reshape(1, d), wg, wu, wd,
            final_g.reshape(1, d)]
    if rider is not None:
        r_in, r_out, r_shape = _cast_rider_specs(rider, n // tm, lambda i: (i, 0))
        in_specs.append(r_in)
        out_specs.append(r_out)
        out_shape.append(r_shape)
        args.append(rider)
    out = pl.pallas_call(
        functools.partial(_outproj_ffn_kernel, final=final),
        grid=(n // tm,),
        in_specs=in_specs,
        out_specs=out_specs,
        out_shape=out_shape,
        scratch_shapes=[pltpu.VMEM((tm, ff), BF16)],
        compiler_params=_params("arbitrary"),
    )(*args)
    return out if rider is not None else out[0]


def _moe_ffn_kernel(blk_e_ref, n_used_ref, x_ref, wg_ref, wu_ref, wd_ref, o_ref, xs_ref, act_ref):
    del blk_e_ref
    g, d = xs_ref.shape

    @pl.when(pl.program_id(0) < n_used_ref[0])
    def _():
        xs_ref[...] = _load_token_tiles(x_ref, g, d).astype(BF16)
        _store_token_tiles(o_ref, _swiglu(xs_ref[...], wg_ref.at[0], wu_ref.at[0], wd_ref.at[0], act_ref))

    @pl.when(pl.program_id(0) >= n_used_ref[0])
    def _():
        o_ref[...] = jnp.zeros_like(o_ref)


def _moe_grouped(xb, blk_e, n_used, wg, wu, wd):
    _, d, ff = wg.shape
    sub = d // LANES
    g = FFN_ROWS
    grid_spec = pltpu.PrefetchScalarGridSpec(
        num_scalar_prefetch=2,
        grid=(xb.shape[0] // (g * sub),),
        in_specs=[
            pl.BlockSpec((g * sub, LANES), lambda b, be, nu: (b, 0)),
            pl.BlockSpec((1, d, ff), lambda b, be, nu: (be[b], 0, 0)),
            pl.BlockSpec((1, d, ff), lambda b, be, nu: (be[b], 0, 0)),
            pl.BlockSpec((1, ff, d), lambda b, be, nu: (be[b], 0, 0)),
        ],
        out_specs=pl.BlockSpec((g * sub, LANES), lambda b, be, nu: (b, 0)),
        scratch_shapes=[pltpu.VMEM((g, d), BF16), pltpu.VMEM((g, ff), BF16)],
    )
    return pl.pallas_call(
        _moe_ffn_kernel,
        grid_spec=grid_spec,
        out_shape=jax.ShapeDtypeStruct(xb.shape, F32),
        compiler_params=_params("arbitrary"),
    )(blk_e, n_used, xb, wg, wu, wd)


def _start_run_copies(src_ref, dst_ref, src_row, dst_row, count, sub, sem, wait=False):
    for bit in range(MOE_RUN_BITS):
        size = 1 << bit
        off = jnp.left_shift(jnp.right_shift(count, bit + 1), bit + 1)

        @pl.when(jnp.bitwise_and(jnp.right_shift(count, bit), 1) == 1)
        def _():
            copy = pltpu.make_async_copy(
                src_ref.at[pl.ds(pl.multiple_of((src_row + off) * sub, sub), size * sub)],
                dst_ref.at[pl.ds(pl.multiple_of((dst_row + off) * sub, sub), size * sub)], sem)
            copy.wait() if wait else copy.start()


def _dispatch_kernel(pos_ref, cnt_ref, glob_ref, loc_ref, pad_row_ref, pad_len_ref, h_ref, xb_ref,
                     sbuf, zbuf, sems, *, sub):
    i = pl.program_id(0)
    last = pl.num_programs(0) - 1
    tm = h_ref.shape[0] // sub
    ta = tm * TOP_K
    slot = lax.rem(i, 2)
    g = zbuf.shape[0] // sub
    blocks = xb_ref.shape[0] // (g * sub)

    def zero_fill(wait):
        for e in range(N_EXPERTS):
            _start_run_copies(zbuf, xb_ref, 0, pad_row_ref[e], pad_len_ref[e], sub, sems.at[2], wait)
            unused = pad_len_ref[N_EXPERTS] + e

            @pl.when(unused < blocks)
            def _():
                copy = pltpu.make_async_copy(
                    zbuf, xb_ref.at[pl.ds(pl.multiple_of(unused * (g * sub), g * sub), g * sub)], sems.at[2])
                copy.wait() if wait else copy.start()

    @pl.when(i == 0)
    def _():
        zbuf[...] = jnp.zeros_like(zbuf)
        zero_fill(False)

    def wait_slot(s):
        pltpu.make_async_copy(sbuf.at[s], xb_ref.at[pl.ds(0, ta * sub)], sems.at[s]).wait()

    @pl.when(i >= 2)
    def _():
        wait_slot(slot)

    def place(t, carry):
        v = _token_tile(h_ref, t, sub)[...]
        for k in range(TOP_K):
            _token_tile(sbuf.at[slot], pos_ref[i * ta + TOP_K * t + k], sub)[...] = v
        return carry

    lax.fori_loop(0, tm, place, 0, unroll=8)

    for e in range(N_EXPERTS):
        r = i * N_EXPERTS + e
        _start_run_copies(sbuf.at[slot], xb_ref, loc_ref[r], glob_ref[r], cnt_ref[r], sub, sems.at[slot])

    @pl.when(i == last)
    def _():
        @pl.when(i >= 1)
        def _():
            wait_slot(1 - slot)
        wait_slot(slot)
        zero_fill(True)


def _dispatch(h_tiles, layout, seq, d):
    sub = d // LANES
    n = h_tiles.shape[0] // sub
    tm = _tile(seq, MOE_TILE_TOKENS, 8)
    grid_spec = pltpu.PrefetchScalarGridSpec(
        num_scalar_prefetch=6,
        grid=(n // tm,),
        in_specs=[pl.BlockSpec((tm * sub, LANES), lambda i, *_: (i, 0))],
        out_specs=pl.BlockSpec(memory_space=pl.ANY),
        scratch_shapes=[pltpu.VMEM((2, tm * TOP_K * sub, LANES), h_tiles.dtype),
                        pltpu.VMEM((FFN_ROWS * sub, LANES), h_tiles.dtype),
                        pltpu.SemaphoreType.DMA((3,))],
    )
    return pl.pallas_call(
        functools.partial(_dispatch_kernel, sub=sub),
        grid_spec=grid_spec,
        out_shape=jax.ShapeDtypeStruct((layout["p_total"] * sub, LANES), h_tiles.dtype),
        compiler_params=_params("arbitrary"),
    )(layout["pos"], layout["cnt"], layout["glob"], layout["loc"], layout["pad_row"], layout["pad_len"], h_tiles)


def _combine_kernel(pos_ref, cnt_ref, glob_ref, loc_ref, x1_ref, rt_ref, mod_ref, fg_ref, yb_ref, o_ref,
                    ybuf, tbuf, sems, *, final):
    i = pl.program_id(0)
    tm, d = x1_ref.shape
    sub = d // LANES
    ta = tm * TOP_K
    slot = lax.rem(i, 2)

    def fetch(tile, s):
        for e in range(N_EXPERTS):
            r = tile * N_EXPERTS + e
            _start_run_copies(yb_ref, ybuf.at[s], glob_ref[r], loc_ref[r], cnt_ref[r], sub, sems.at[s])

    @pl.when(i == 0)
    def _():
        fetch(0, 0)

    @pl.when(i + 1 < pl.num_programs(0))
    def _():
        fetch(i + 1, 1 - slot)

    pltpu.make_async_copy(yb_ref.at[pl.ds(0, ta * sub)], ybuf.at[slot], sems.at[slot]).wait()

    def unpermute(t, carry):
        for k in range(TOP_K):
            _token_tile(tbuf.at[k], t, sub)[...] = _token_tile(ybuf.at[slot], pos_ref[i * ta + TOP_K * t + k], sub)[...]
        return carry

    lax.fori_loop(0, tm, unpermute, 0, unroll=8)

    rt = rt_ref[...]
    y = (rt[:, 2:3] * _load_token_tiles(tbuf.at[0], tm, d) + rt[:, 3:4] * _load_token_tiles(tbuf.at[1], tm, d))
    x2 = x1_ref[...] + mod_ref[0][5:6] * y
    o_ref[...] = _rms(x2) * fg_ref[...] if final else x2


def _combine(x1, yb, layout, route, mod, final_g, seq, final):
    n, d = x1.shape
    sub = d // LANES
    tm = _tile(seq, MOE_TILE_TOKENS, 8)
    per_seq = seq // tm
    grid_spec = pltpu.PrefetchScalarGridSpec(
        num_scalar_prefetch=4,
        grid=(n // tm,),
        in_specs=[
            pl.BlockSpec((tm, d), lambda i, *_: (i, 0)),
            pl.BlockSpec((tm, LANES), lambda i, *_: (i, 0)),
            pl.BlockSpec((1, 6, d), lambda i, *_: (i // per_seq, 0, 0)),
            pl.BlockSpec((1, d), lambda i, *_: (0, 0)),
            pl.BlockSpec(memory_space=pl.ANY),
        ],
        out_specs=pl.BlockSpec((tm, d), lambda i, *_: (i, 0)),
        scratch_shapes=[pltpu.VMEM((2, TOP_K * tm * sub, LANES), F32),
                        pltpu.VMEM((TOP_K, tm * sub, LANES), F32),
                        pltpu.SemaphoreType.DMA((2,))],
    )
    return pl.pallas_call(
        functools.partial(_combine_kernel, final=final),
        grid_spec=grid_spec,
        out_shape=jax.ShapeDtypeStruct((n, d), F32),
        compiler_params=_params("arbitrary"),
    )(layout["pos"], layout["cnt"], layout["glob"], layout["loc"], x1, route, mod, final_g.reshape(1, d), yb)


def _moe_layout(route, n, seq):
    g = FFN_ROWS
    a = n * TOP_K
    ta = _tile(seq, MOE_TILE_TOKENS, 8) * TOP_K
    tiles = a // ta
    assert ta < (1 << MOE_RUN_BITS)
    e_flat = route[:, :TOP_K].astype(jnp.int32).reshape(-1)
    onehot = (e_flat[None, :] == jnp.arange(N_EXPERTS, dtype=jnp.int32)[:, None]).astype(jnp.int32)
    csum = jnp.cumsum(onehot, axis=1)
    rank = jnp.sum(csum * onehot, axis=0) - 1
    counts = csum[:, -1]
    padded = (counts + g - 1) // g * g
    pad_ends = jnp.cumsum(padded)
    pad_starts = pad_ends - padded
    dest = jnp.sum(pad_starts[:, None] * onehot, axis=0) + rank
    tile_end = csum[:, ta - 1::ta]
    cnt = tile_end - jnp.concatenate([jnp.zeros((N_EXPERTS, 1), jnp.int32), tile_end[:, :-1]], axis=1)
    glob = pad_starts[:, None] + tile_end - cnt
    loc = jnp.cumsum(cnt, axis=0) - cnt
    per_assignment = lambda v: jnp.sum(
        jnp.broadcast_to(v[:, :, None], (N_EXPERTS, tiles, ta)).reshape(N_EXPERTS, a) * onehot, axis=0)
    pos = per_assignment(loc) + dest - per_assignment(glob)
    p_total = (-(-a // g) + N_EXPERTS) * g
    blk_start = jnp.arange(p_total // g, dtype=jnp.int32) * g
    blk_e = jnp.minimum(jnp.sum((pad_ends[:, None] <= blk_start[None, :]).astype(jnp.int32), axis=0),
                        N_EXPERTS - 1)
    i32 = lambda v: v.astype(jnp.int32).reshape(-1)
    n_used = pad_ends[-1:] // g
    cnt, glob, loc = cnt.T, glob.T, loc.T
    return dict(pos=i32(pos), cnt=i32(cnt), glob=i32(glob), loc=i32(loc), blk_e=i32(blk_e), n_used=i32(n_used),
                pad_row=i32(pad_starts + counts), pad_len=i32(jnp.concatenate([padded - counts, n_used])),
                p_total=p_total)


def _moe_ffn(h_tiles, x1, route, mod, wg, wu, wd, final_g, seq, final):
    n, d = x1.shape
    layout = _moe_layout(route, n, seq)
    xb = _dispatch(h_tiles, layout, seq, d)
    yb = _moe_grouped(xb, layout["blk_e"], layout["n_used"], wg, wu, wd)
    return _combine(x1, yb, layout, route, mod, final_g, seq, final)


def kernel(x, c, norm_mix_g, norm_ffn_g, ada_w, ada_b, w_in, w_out, att_out_g, ret_out_g, ffn_w_gate,
           ffn_w_up, ffn_w_down, router_w, moe_w_gate, moe_w_up, moe_w_down, final_norm_g):
    batch, seq, d = x.shape
    depth = ada_w.shape[0]
    assert seq % MOBA_BLOCK == 0 and seq % RET_BLOCK == 0
    assert w_in.shape[2] == N_GROUPS * GROUP_WIDTH
    mods = _adaln(c, ada_w, ada_b).reshape(depth, batch, 6, d)
    tables = _rotary_tables(seq)
    xf = x.reshape(batch * seq, d)
    flat = lambda w: w.reshape(-1, w.shape[-1])
    moe_bf16 = {}
    for l in range(depth):
        mod = mods[l]
        final = l == depth - 1
        proj = _inproj(xf, mod, norm_mix_g[l], w_in[l].astype(BF16), tables, seq)
        o_r = _retention(proj, ret_out_g[l], batch, seq)
        if l % 2 == 0:
            nxt = (l + 1) // 2
            ride = l + 1 < depth
            o_a = _moba(proj, batch, seq, rider=flat(moe_w_gate[nxt]) if ride else None)
            if ride:
                o_a, gate_bf16 = o_a
                moe_bf16[nxt, "gate"] = gate_bf16.reshape(moe_w_gate[nxt].shape)
            xf = _outproj_ffn(o_a, o_r, xf, w_out[l].astype(BF16), att_out_g[l], mod, norm_ffn_g[l],
                              ffn_w_gate[l // 2].astype(BF16), ffn_w_up[l // 2].astype(BF16),
                              ffn_w_down[l // 2].astype(BF16), final_norm_g, seq, final,
                              rider=flat(moe_w_down[nxt]) if ride else None)
            if ride:
                xf, down_bf16 = xf
                moe_bf16[nxt, "down"] = down_bf16.reshape(moe_w_down[nxt].shape)
        else:
            e = l // 2
            o_a, up_bf16 = _moba(proj, batch, seq, rider=flat(moe_w_up[e]))
            x1, h, route = _outproj_route(o_a, o_r, xf, w_out[l].astype(BF16), att_out_g[l], mod, norm_ffn_g[l],
                                          seq, router_w[e])
            xf = _moe_ffn(h, x1, route, mod, moe_bf16[e, "gate"], up_bf16.reshape(moe_w_up[e].shape),
                          moe_bf16[e, "down"], final_norm_g, seq, final)
    return xf.reshape(batch, seq, d)
```

```python
import functools

import jax
import jax.numpy as jnp
from jax import lax
from jax.experimental import pallas as pl
from jax.experimental.pallas import tpu as pltpu

F32 = jnp.float32
BF16 = jnp.bfloat16

LANES = 128
V7X_MXU_WIDTH = 256
V7X_VMEM_LIMIT_BYTES = 56 * 1024 * 1024

ATT_HEADS = 8
ATT_HEAD_DIM = 64
ATT_WIDTH = ATT_HEADS * ATT_HEAD_DIM
MOBA_BLOCK = 256
MOBA_TOPK = 3
MOBA_PAIRS_PER_STEP = 2
MOBA_GATE_ROWS = 16
MASK_VALUE = -1e30
MOBA_Q_SCALE = ATT_HEAD_DIM ** -0.5 * 1.4426950408889634
RET_HEADS = 4
RET_HEAD_DIM = 128
RET_WIDTH = RET_HEADS * RET_HEAD_DIM
RET_BLOCK = 256
ROPE_BASE = 10000.0
N_EXPERTS = 8
TOP_K = 2
EPS = 1e-6
GROUP_WIDTH = 512
N_GROUPS = 7
MOE_TILE_TOKENS = 512
MOE_RUN_BITS = 11
FFN_ROWS = 512


def _tile(total, target, mult):
    best = None
    t = mult
    while t <= min(total, target):
        if total % t == 0:
            best = t
        t += mult
    return best if best is not None else total


def _params(*sem):
    return pltpu.CompilerParams(dimension_semantics=sem, vmem_limit_bytes=V7X_VMEM_LIMIT_BYTES)


def _rms(x):
    return x * lax.rsqrt(jnp.mean(x * x, axis=-1, keepdims=True) + EPS)


def _silu(x):
    return x * jax.nn.sigmoid(x)


def _store_token_tiles(ref, x, row0=0):
    rows, d = x.shape
    sub = d // LANES
    for s in range(sub):
        ref[pl.ds(row0 * sub + s, rows, stride=sub), :] = x[:, s * LANES:(s + 1) * LANES].astype(ref.dtype)


def _load_token_tiles(ref, rows, d, row0=0):
    sub = d // LANES
    return jnp.concatenate([ref[pl.ds(row0 * sub + s, rows, stride=sub), :] for s in range(sub)], axis=1)


def _token_tile(ref, t, sub):
    return ref.at[pl.ds(pl.multiple_of(t * sub, sub), sub)]


def _adaln_kernel(c_ref, w_ref, b_ref, o_ref):
    cs = _silu(c_ref[...]).astype(BF16)
    o_ref[0] = jnp.dot(cs, w_ref[0].astype(BF16), preferred_element_type=F32) + b_ref[0]


def _adaln(c, ada_w, ada_b):
    depth, d, w = ada_w.shape
    b = c.shape[0]
    tn = _tile(w, 1536, LANES)
    return pl.pallas_call(
        _adaln_kernel,
        grid=(depth, w // tn),
        in_specs=[
            pl.BlockSpec((b, d), lambda l, j: (0, 0)),
            pl.BlockSpec((1, d, tn), lambda l, j: (l, 0, j)),
            pl.BlockSpec((1, 1, tn), lambda l, j: (l, 0, j)),
        ],
        out_specs=pl.BlockSpec((1, b, tn), lambda l, j: (l, 0, j)),
        out_shape=jax.ShapeDtypeStruct((depth, b, w), F32),
        compiler_params=_params("arbitrary", "arbitrary"),
    )(c, ada_w, ada_b.reshape(depth, 1, w))


def _inproj_kernel(x_ref, mod_ref, g_ref, w_ref, cq_ref, sq_ref, ck_ref, sk_ref, o_ref):
    m = mod_ref[0]
    h = (_rms(x_ref[...]) * g_ref[...] * (1.0 + m[1:2]) + m[0:1]).astype(BF16)
    rotary = {3: (cq_ref, sq_ref), 4: (ck_ref, sk_ref)}
    for j in range(N_GROUPS):
        c0 = j * GROUP_WIDTH
        acc = jnp.dot(h, w_ref[:, c0:c0 + GROUP_WIDTH], preferred_element_type=F32)
        if j in rotary:
            cos = rotary[j][0][...]
            sin = rotary[j][1][...]
            for hd in range(RET_HEADS):
                a = acc[:, hd * RET_HEAD_DIM:(hd + 1) * RET_HEAD_DIM]
                r = a * cos + pltpu.roll(a, RET_HEAD_DIM // 2, 1) * sin
                o_ref[:, c0 + hd * RET_HEAD_DIM:c0 + (hd + 1) * RET_HEAD_DIM] = r.astype(o_ref.dtype)
        elif j == 0:
            o_ref[:, c0:c0 + GROUP_WIDTH] = (acc * MOBA_Q_SCALE).astype(o_ref.dtype)
        else:
            o_ref[:, c0:c0 + GROUP_WIDTH] = acc.astype(o_ref.dtype)


def _rotary_tables(s):
    half = RET_HEAD_DIM // 2
    inv_freq = ROPE_BASE ** (-jnp.arange(half, dtype=F32) / half)
    ang = jnp.arange(s, dtype=F32)[:, None] * inv_freq[None, :]
    cos = jnp.cos(ang)
    sin = jnp.sin(ang)
    cos2 = jnp.concatenate([cos, cos], axis=-1)
    sin2 = jnp.concatenate([-sin, sin], axis=-1)
    k_scale = RET_HEAD_DIM ** -0.5
    return cos2, sin2, cos2 * k_scale, sin2 * k_scale


def _inproj(x2d, mod, g, w_bf16, tables, seq):
    n, d = x2d.shape
    tm = _tile(seq, 1024, 16)
    per_seq = seq // tm
    width = N_GROUPS * GROUP_WIDTH
    return pl.pallas_call(
        _inproj_kernel,
        grid=(n // tm,),
        in_specs=[
            pl.BlockSpec((tm, d), lambda i: (i, 0)),
            pl.BlockSpec((1, 6, d), lambda i: (i // per_seq, 0, 0)),
            pl.BlockSpec((1, d), lambda i: (0, 0)),
            pl.BlockSpec((d, width), lambda i: (0, 0)),
        ] + [pl.BlockSpec((tm, RET_HEAD_DIM), lambda i: (i % per_seq, 0))] * 4,
        out_specs=pl.BlockSpec((tm, width), lambda i: (i, 0)),
        out_shape=jax.ShapeDtypeStruct((n, width), BF16),
        compiler_params=_params("arbitrary"),
    )(x2d, mod, g.reshape(1, d), w_bf16, *tables)


def _cast_rider_specs(rider, steps, index):
    rows, cols = rider.shape
    assert rows % steps == 0 and (rows // steps) % 16 == 0
    spec = pl.BlockSpec((rows // steps, cols), index)
    return spec, spec, jax.ShapeDtypeStruct(rider.shape, BF16)


def _moba_kernel(q_ref, k_ref, v_ref, *rest):
    if len(rest) == 3:
        w_ref, o_ref, w_bf16_ref = rest
        w_bf16_ref[...] = w_ref[...].astype(BF16)
    else:
        o_ref, = rest
    s = q_ref.shape[0]
    blk = MOBA_BLOCK
    nb = s // blk
    assert nb <= MOBA_GATE_ROWS
    n_sel = min(MOBA_TOPK, nb - 1)
    shift = blk.bit_length() - 1
    contract_lanes = (((1,), (1,)), ((), ()))
    lane = lax.broadcasted_iota(jnp.int32, (1, LANES), 1)
    head_of_lane = jnp.right_shift(lane, ATT_HEAD_DIM.bit_length() - 1)

    key_pos = lax.broadcasted_iota(jnp.int32, (s, LANES), 0)
    key_lane = lax.broadcasted_iota(jnp.int32, (s, LANES), 1)
    key_block_onehot = jnp.where(jnp.right_shift(key_pos, shift) == key_lane, 1.0, 0.0).astype(BF16)
    ones = jnp.ones((s, LANES), BF16)

    blk_id = lax.broadcasted_iota(jnp.int32, (MOBA_GATE_ROWS, s), 0)
    q_blk = jnp.right_shift(lax.broadcasted_iota(jnp.int32, (MOBA_GATE_ROWS, s), 1), shift)
    past = blk_id < q_blk
    block_mean = jnp.where(q_blk == blk_id, 1.0 / blk, 0.0).astype(BF16)
    eye = jnp.where(lax.broadcasted_iota(jnp.int32, (MOBA_GATE_ROWS, LANES), 0)
                    == lax.broadcasted_iota(jnp.int32, (MOBA_GATE_ROWS, LANES), 1), 1.0, 0.0).astype(BF16)
    row = lax.broadcasted_iota(jnp.int32, (2 * blk, blk), 0)
    col = lax.broadcasted_iota(jnp.int32, (2 * blk, blk), 1)
    causal_bias = jnp.where(col <= jnp.bitwise_and(row, blk - 1), 0.0, MASK_VALUE)

    pairs = []
    for pair in range(q_ref.shape[1] // LANES):
        cols = slice(pair * LANES, (pair + 1) * LANES)
        q_all = q_ref[:, cols]
        k_all = k_ref[:, cols]
        k_aug = jnp.concatenate([k_all, key_block_onehot], axis=1)
        v_aug = jnp.concatenate([v_ref[:, cols], ones], axis=1)
        k_mean = jnp.dot(block_mean, k_all, preferred_element_type=F32).astype(BF16)

        heads = []
        for hh in range(2):
            qh = jnp.where(head_of_lane == hh, q_all, jnp.zeros_like(q_all))
            gate = lax.dot_general(k_mean, qh, contract_lanes, preferred_element_type=F32)
            gm = jnp.where(past, gate, MASK_VALUE)
            rank = jnp.zeros(gm.shape, jnp.int32)
            for jp in range(nb):
                cv = gm[jp:jp + 1, :]
                beats = jnp.logical_or(cv > gm, jnp.logical_and(cv == gm, jp < blk_id))
                rank = rank + jnp.where(beats, 1, 0)
            visible = jnp.logical_or(jnp.logical_and(rank < n_sel, past), blk_id >= q_blk)
            bias_t = jnp.where(visible, 0.0, MASK_VALUE).astype(BF16)
            bias = lax.dot_general(bias_t, eye, (((0,), (0,)), ((), ())), preferred_element_type=F32)
            heads.append(jnp.concatenate([qh, bias.astype(BF16)], axis=1))
        pairs.append((cols, heads, k_aug, v_aug))

    def scores(i, pair):
        _, heads, k_aug, _ = pair
        rows = slice(i * blk, (i + 1) * blk)
        q_both = jnp.concatenate([heads[0][rows], heads[1][rows]], axis=0)
        return lax.dot_general(q_both, k_aug[:(i + 1) * blk], contract_lanes, preferred_element_type=F32)

    def softmax_pv(i, pair, sc):
        cols, _, _, v_aug = pair
        sc_own = sc[:, i * blk:] + causal_bias
        m = jnp.max(sc_own, axis=-1, keepdims=True)
        if i > 0:
            sc_past = sc[:, :i * blk]
            m = jnp.maximum(m, jnp.max(sc_past, axis=-1, keepdims=True))
            p = jnp.concatenate([jnp.exp2(sc_past - m), jnp.exp2(sc_own - m)], axis=1)
        else:
            p = jnp.exp2(sc_own - m)
        o = jnp.dot(p.astype(BF16), v_aug[:(i + 1) * blk], preferred_element_type=F32)
        o = o[:, :LANES] / o[:, LANES:LANES + 1]
        o_ref[i * blk:(i + 1) * blk, cols] = jnp.where(head_of_lane == 0, o[:blk], o[blk:]).astype(o_ref.dtype)

    for i in reversed(range(nb)):
        for pair in pairs:
            softmax_pv(i, pair, scores(i, pair))


def _moba(proj, batch, seq, rider=None):
    n = proj.shape[0]
    width = MOBA_PAIRS_PER_STEP * LANES
    steps = ATT_WIDTH // width
    spec = lambda off: pl.BlockSpec((seq, width), lambda b, p: (b, off + p))
    in_specs = [spec(0), spec(steps), spec(2 * steps)]
    out_specs = [pl.BlockSpec((seq, width), lambda b, p: (b, p))]
    out_shape = [jax.ShapeDtypeStruct((n, ATT_WIDTH), BF16)]
    args = [proj, proj, proj]
    if rider is not None:
        r_in, r_out, r_shape = _cast_rider_specs(rider, batch * steps, lambda b, p: (b * steps + p, 0))
        in_specs.append(r_in)
        out_specs.append(r_out)
        out_shape.append(r_shape)
        args.append(rider)
    out = pl.pallas_call(
        _moba_kernel,
        grid=(batch, steps),
        in_specs=in_specs,
        out_specs=out_specs,
        out_shape=out_shape,
        compiler_params=_params("arbitrary", "arbitrary"),
    )(*args)
    return out if rider is not None else out[0]


def _retention_kernel(q_ref, k_ref, v_ref, g_ref, dm_ref, qd_ref, kd_ref, cd_ref, og_ref, o_ref):
    s = q_ref.shape[0]
    c = RET_BLOCK
    states = [jnp.zeros((RET_HEAD_DIM, RET_HEAD_DIM), F32) for _ in range(RET_HEADS)]
    for n in range(s // c):
        rows = slice(n * c, (n + 1) * c)
        for hd in range(RET_HEADS):
            cols = slice(hd * RET_HEAD_DIM, (hd + 1) * RET_HEAD_DIM)
            qc = q_ref[rows, cols]
            kc = k_ref[rows, cols]
            vc = v_ref[rows, cols]
            scores = lax.dot_general(qc, kc, (((1,), (1,)), ((), ())), preferred_element_type=F32) * dm_ref[hd]
            inner = jnp.dot(scores.astype(BF16), vc, preferred_element_type=F32)
            cross = jnp.dot(qc, states[hd].astype(BF16), preferred_element_type=F32) * qd_ref[hd]
            o = _rms(inner + cross) * og_ref[hd]
            gate = g_ref[rows, cols].astype(F32)
            o_ref[rows, cols] = (_silu(gate) * o).astype(o_ref.dtype)
            k_dec = (kc.astype(F32) * kd_ref[hd]).astype(BF16)
            kv = lax.dot_general(k_dec, vc, (((0,), (0,)), ((), ())), preferred_element_type=F32)
            states[hd] = states[hd] * cd_ref[hd] + kv


def _retention_tables():
    h, c = RET_HEADS, RET_BLOCK
    lg = jnp.log(1.0 - 2.0 ** (-5.0 - jnp.arange(h, dtype=F32)))
    idx = jnp.arange(c, dtype=F32)
    diff = idx[:, None] - idx[None, :]
    dmask = jnp.where(diff >= 0, jnp.exp(jnp.maximum(diff, 0.0)[None] * lg[:, None, None]), 0.0)
    rows = lambda v: jnp.broadcast_to(v[:, :, None], (h, c, RET_HEAD_DIM))
    qdec = rows(jnp.exp((idx + 1.0)[None, :] * lg[:, None]))
    kdec = rows(jnp.exp((c - 1 - idx)[None, :] * lg[:, None]))
    cdec = jnp.broadcast_to(jnp.exp(c * lg)[:, None, None], (h, RET_HEAD_DIM, RET_HEAD_DIM))
    return dmask, qdec, kdec, cdec


def _retention(proj, ret_out_g, batch, seq):
    n = proj.shape[0]
    base = 3 * ATT_WIDTH // RET_WIDTH
    spec = lambda off: pl.BlockSpec((seq, RET_WIDTH), lambda b: (b, base + off))
    tables = _retention_tables()
    whole = lambda t: pl.BlockSpec(t.shape, lambda b: (0, 0, 0))
    return pl.pallas_call(
        _retention_kernel,
        grid=(batch,),
        in_specs=[spec(0), spec(1), spec(2), spec(3)] + [whole(t) for t in tables]
                 + [pl.BlockSpec((RET_HEADS, 1, RET_HEAD_DIM), lambda b: (0, 0, 0))],
        out_specs=pl.BlockSpec((seq, RET_WIDTH), lambda b: (b, 0)),
        out_shape=jax.ShapeDtypeStruct((n, RET_WIDTH), BF16),
        compiler_params=_params("arbitrary"),
    )(proj, proj, proj, proj, *tables, ret_out_g.reshape(RET_HEADS, 1, RET_HEAD_DIM))


def _top2_route(logits):
    lane = lax.broadcasted_iota(jnp.int32, logits.shape, 1)
    lg = jnp.where(lane < N_EXPERTS, logits, -jnp.inf)
    m1 = jnp.max(lg, axis=-1, keepdims=True)
    i1 = jnp.min(jnp.where(lg == m1, lane, LANES), axis=-1, keepdims=True)
    lg2 = jnp.where(lane == i1, -jnp.inf, lg)
    m2 = jnp.max(lg2, axis=-1, keepdims=True)
    i2 = jnp.min(jnp.where(lg2 == m2, lane, LANES), axis=-1, keepdims=True)
    e = jnp.exp(m2 - m1)
    w1 = 1.0 / (1.0 + e)
    w2 = e / (1.0 + e)
    out = jnp.where(lane == 0, i1.astype(F32), 0.0)
    out = jnp.where(lane == 1, i2.astype(F32), out)
    out = jnp.where(lane == 2, w1, out)
    return jnp.where(lane == 3, w2, out)


def _mix_and_norm(oa_ref, or_ref, x_ref, w_ref, ag_ref, m, ng_ref):
    oa = (_rms(oa_ref[...].astype(F32)) * ag_ref[...]).astype(BF16)
    mix = (jnp.dot(oa, w_ref[:ATT_WIDTH, :], preferred_element_type=F32)
           + jnp.dot(or_ref[...], w_ref[ATT_WIDTH:, :], preferred_element_type=F32))
    x1 = x_ref[...] + m[2:3] * mix
    h = _rms(x1) * ng_ref[...] * (1.0 + m[4:5]) + m[3:4]
    return x1, h


def _mixer_specs(tm, d, per_seq):
    row = lambda w: pl.BlockSpec((tm, w), lambda i: (i, 0))
    full = lambda a, b: pl.BlockSpec((a, b), lambda i: (0, 0))
    return [row(ATT_WIDTH), row(RET_WIDTH), row(d), full(ATT_WIDTH + RET_WIDTH, d), full(1, ATT_WIDTH),
            pl.BlockSpec((1, 6, d), lambda i: (i // per_seq, 0, 0)), full(1, d)]


def _outproj_route_kernel(oa_ref, or_ref, x_ref, w_ref, ag_ref, mod_ref, ng_ref, rw_ref, x1_ref, h_ref, rt_ref):
    x1, h = _mix_and_norm(oa_ref, or_ref, x_ref, w_ref, ag_ref, mod_ref[0], ng_ref)
    x1_ref[...] = x1
    _store_token_tiles(h_ref, h)
    rt_ref[...] = _top2_route(jnp.dot(h.astype(BF16), rw_ref[...], preferred_element_type=F32))


def _outproj_route(o_a, o_r, x2d, w_bf16, att_g, mod, norm_g, seq, router_w):
    n, d = x2d.shape
    tm = _tile(seq, 512, 16)
    sub = d // LANES
    rw = jnp.zeros((d, LANES), BF16).at[:, :N_EXPERTS].set(router_w.astype(BF16))
    return pl.pallas_call(
        _outproj_route_kernel,
        grid=(n // tm,),
        in_specs=_mixer_specs(tm, d, seq // tm) + [pl.BlockSpec((d, LANES), lambda i: (0, 0))],
        out_specs=[pl.BlockSpec((tm, d), lambda i: (i, 0)),
                   pl.BlockSpec((tm * sub, LANES), lambda i: (i, 0)),
                   pl.BlockSpec((tm, LANES), lambda i: (i, 0))],
        out_shape=[jax.ShapeDtypeStruct((n, d), F32), jax.ShapeDtypeStruct((n * sub, LANES), F32),
                   jax.ShapeDtypeStruct((n, LANES), F32)],
        compiler_params=_params("arbitrary"),
    )(o_a, o_r, x2d, w_bf16, att_g.reshape(1, ATT_WIDTH), mod, norm_g.reshape(1, d), rw)


def _swiglu(h, wg_ref, wu_ref, wd_ref, act_ref):
    ff = act_ref.shape[1]
    tf = _tile(ff, V7X_MXU_WIDTH, LANES)
    for c0 in range(0, ff, tf):
        g = jnp.dot(h, wg_ref[:, c0:c0 + tf], preferred_element_type=F32)
        u = jnp.dot(h, wu_ref[:, c0:c0 + tf], preferred_element_type=F32)
        act_ref[:, c0:c0 + tf] = (_silu(g) * u).astype(act_ref.dtype)
    return jnp.dot(act_ref[...], wd_ref[...], preferred_element_type=F32)


def _outproj_ffn_kernel(oa_ref, or_ref, x_ref, w_ref, ag_ref, mod_ref, ng_ref, wg_ref, wu_ref, wd_ref, fg_ref,
                        *rest, final):
    if len(rest) == 4:
        rider_ref, o_ref, rider_bf16_ref, act_ref = rest
        rider_bf16_ref[...] = rider_ref[...].astype(BF16)
    else:
        o_ref, act_ref = rest
    m = mod_ref[0]
    x1, h = _mix_and_norm(oa_ref, or_ref, x_ref, w_ref, ag_ref, m, ng_ref)
    y = _swiglu(h.astype(BF16), wg_ref, wu_ref, wd_ref, act_ref)
    x2 = x1 + m[5:6] * y
    o_ref[...] = _rms(x2) * fg_ref[...] if final else x2


def _outproj_ffn(o_a, o_r, x2d, w_bf16, att_g, mod, norm_g, wg, wu, wd, final_g, seq, final, rider=None):
    n, d = x2d.shape
    ff = wg.shape[1]
    tm = _tile(seq, FFN_ROWS, 16)
    whole = lambda t: pl.BlockSpec(t.shape, lambda i: (0, 0))
    in_specs = _mixer_specs(tm, d, seq // tm) + [whole(wg), whole(wu), whole(wd), pl.BlockSpec((1, d), lambda i: (0, 0))]
    out_specs = [pl.BlockSpec((tm, d), lambda i: (i, 0))]
    out_shape = [jax.ShapeDtypeStruct((n, d), F32)]
    args = [o_a, o_r, x2d, w_bf16, att_g.reshape(1, ATT_WIDTH), mod, norm_g.reshape(1, d), wg, wu, wd,
            final_g.reshape(1, d)]
    if rider is not None:
        r_in, r_out, r_shape = _cast_rider_specs(rider, n // tm, lambda i: (i, 0))
        in_specs.append(r_in)
        out_specs.append(r_out)
        out_shape.append(r_shape)
        args.append(rider)
    out = pl.pallas_call(
        functools.partial(_outproj_ffn_kernel, final=final),
        grid=(n // tm,),
        in_specs=in_specs,
        out_specs=out_specs,
        out_shape=out_shape,
        scratch_shapes=[pltpu.VMEM((tm, ff), BF16)],
        compiler_params=_params("arbitrary"),
    )(*args)
    return out if rider is not None else out[0]


def _moe_ffn_kernel(blk_e_ref, n_used_ref, x_ref, wg_ref, wu_ref, wd_ref, o_ref, xs_ref, act_ref):
    del blk_e_ref
    g, d = xs_ref.shape

    @pl.when(pl.program_id(0) < n_used_ref[0])
    def _():
        xs_ref[...] = _load_token_tiles(x_ref, g, d).astype(BF16)
        _store_token_tiles(o_ref, _swiglu(xs_ref[...], wg_ref.at[0], wu_ref.at[0], wd_ref.at[0], act_ref))

    @pl.when(pl.program_id(0) >= n_used_ref[0])
    def _():
        o_ref[...] = jnp.zeros_like(o_ref)


def _moe_grouped(xb, blk_e, n_used, wg, wu, wd):
    _, d, ff = wg.shape
    sub = d // LANES
    g = FFN_ROWS
    grid_spec = pltpu.PrefetchScalarGridSpec(
        num_scalar_prefetch=2,
        grid=(xb.shape[0] // (g * sub),),
        in_specs=[
            pl.BlockSpec((g * sub, LANES), lambda b, be, nu: (b, 0)),
            pl.BlockSpec((1, d, ff), lambda b, be, nu: (be[b], 0, 0)),
            pl.BlockSpec((1, d, ff), lambda b, be, nu: (be[b], 0, 0)),
            pl.BlockSpec((1, ff, d), lambda b, be, nu: (be[b], 0, 0)),
        ],
        out_specs=pl.BlockSpec((g * sub, LANES), lambda b, be, nu: (b, 0)),
        scratch_shapes=[pltpu.VMEM((g, d), BF16), pltpu.VMEM((g, ff), BF16)],
    )
    return pl.pallas_call(
        _moe_ffn_kernel,
        grid_spec=grid_spec,
        out_shape=jax.ShapeDtypeStruct(xb.shape, F32),
        compiler_params=_params("arbitrary"),
    )(blk_e, n_used, xb, wg, wu, wd)


def _start_run_copies(src_ref, dst_ref, src_row, dst_row, count, sub, sem, wait=False):
    for bit in range(MOE_RUN_BITS):
        size = 1 << bit
        off = jnp.left_shift(jnp.right_shift(count, bit + 1), bit + 1)

        @pl.when(jnp.bitwise_and(jnp.right_shift(count, bit), 1) == 1)
        def _():
            copy = pltpu.make_async_copy(
                src_ref.at[pl.ds(pl.multiple_of((src_row + off) * sub, sub), size * sub)],
                dst_ref.at[pl.ds(pl.multiple_of((dst_row + off) * sub, sub), size * sub)], sem)
            copy.wait() if wait else copy.start()


def _dispatch_kernel(pos_ref, cnt_ref, glob_ref, loc_ref, pad_row_ref, pad_len_ref, h_ref, xb_ref,
                     sbuf, zbuf, sems, *, sub):
    i = pl.program_id(0)
    last = pl.num_programs(0) - 1
    tm = h_ref.shape[0] // sub
    ta = tm * TOP_K
    slot = lax.rem(i, 2)
    g = zbuf.shape[0] // sub
    blocks = xb_ref.shape[0] // (g * sub)

    def zero_fill(wait):
        for e in range(N_EXPERTS):
            _start_run_copies(zbuf, xb_ref, 0, pad_row_ref[e], pad_len_ref[e], sub, sems.at[2], wait)
            unused = pad_len_ref[N_EXPERTS] + e

            @pl.when(unused < blocks)
            def _():
                copy = pltpu.make_async_copy(
                    zbuf, xb_ref.at[pl.ds(pl.multiple_of(unused * (g * sub), g * sub), g * sub)], sems.at[2])
                copy.wait() if wait else copy.start()

    @pl.when(i == 0)
    def _():
        zbuf[...] = jnp.zeros_like(zbuf)
        zero_fill(False)

    def wait_slot(s):
        pltpu.make_async_copy(sbuf.at[s], xb_ref.at[pl.ds(0, ta * sub)], sems.at[s]).wait()

    @pl.when(i >= 2)
    def _():
        wait_slot(slot)

    def place(t, carry):
        v = _token_tile(h_ref, t, sub)[...]
        for k in range(TOP_K):
            _token_tile(sbuf.at[slot], pos_ref[i * ta + TOP_K * t + k], sub)[...] = v
        return carry

    lax.fori_loop(0, tm, place, 0, unroll=8)

    for e in range(N_EXPERTS):
        r = i * N_EXPERTS + e
        _start_run_copies(sbuf.at[slot], xb_ref, loc_ref[r], glob_ref[r], cnt_ref[r], sub, sems.at[slot])

    @pl.when(i == last)
    def _():
        @pl.when(i >= 1)
        def _():
            wait_slot(1 - slot)
        wait_slot(slot)
        zero_fill(True)


def _dispatch(h_tiles, layout, seq, d):
    sub = d // LANES
    n = h_tiles.shape[0] // sub
    tm = _tile(seq, MOE_TILE_TOKENS, 8)
    grid_spec = pltpu.PrefetchScalarGridSpec(
        num_scalar_prefetch=6,
        grid=(n // tm,),
        in_specs=[pl.BlockSpec((tm * sub, LANES), lambda i, *_: (i, 0))],
        out_specs=pl.BlockSpec(memory_space=pl.ANY),
        scratch_shapes=[pltpu.VMEM((2, tm * TOP_K * sub, LANES), h_tiles.dtype),
                        pltpu.VMEM((FFN_ROWS * sub, LANES), h_tiles.dtype),
                        pltpu.SemaphoreType.DMA((3,))],
    )
    return pl.pallas_call(
        functools.partial(_dispatch_kernel, sub=sub),
        grid_spec=grid_spec,
        out_shape=jax.ShapeDtypeStruct((layout["p_total"] * sub, LANES), h_tiles.dtype),
        compiler_params=_params("arbitrary"),
    )(layout["pos"], layout["cnt"], layout["glob"], layout["loc"], layout["pad_row"], layout["pad_len"], h_tiles)


def _combine_kernel(pos_ref, cnt_ref, glob_ref, loc_ref, x1_ref, rt_ref, mod_ref, fg_ref, yb_ref, o_ref,
                    ybuf, tbuf, sems, *, final):
    i = pl.program_id(0)
    tm, d = x1_ref.shape
    sub = d // LANES
    ta = tm * TOP_K
    slot = lax.rem(i, 2)

    def fetch(tile, s):
        for e in range(N_EXPERTS):
            r = tile * N_EXPERTS + e
            _start_run_copies(yb_ref, ybuf.at[s], glob_ref[r], loc_ref[r], cnt_ref[r], sub, sems.at[s])

    @pl.when(i == 0)
    def _():
        fetch(0, 0)

    @pl.when(i + 1 < pl.num_programs(0))
    def _():
        fetch(i + 1, 1 - slot)

    pltpu.make_async_copy(yb_ref.at[pl.ds(0, ta * sub)], ybuf.at[slot], sems.at[slot]).wait()

    def unpermute(t, carry):
        for k in range(TOP_K):
            _token_tile(tbuf.at[k], t, sub)[...] = _token_tile(ybuf.at[slot], pos_ref[i * ta + TOP_K * t + k], sub)[...]
        return carry

    lax.fori_loop(0, tm, unpermute, 0, unroll=8)

    rt = rt_ref[...]
    y = (rt[:, 2:3] * _load_token_tiles(tbuf.at[0], tm, d) + rt[:, 3:4] * _load_token_tiles(tbuf.at[1], tm, d))
    x2 = x1_ref[...] + mod_ref[0][5:6] * y
    o_ref[...] = _rms(x2) * fg_ref[...] if final else x2


def _combine(x1, yb, layout, route, mod, final_g, seq, final):
    n, d = x1.shape
    sub = d // LANES
    tm = _tile(seq, MOE_TILE_TOKENS, 8)
    per_seq = seq // tm
    grid_spec = pltpu.PrefetchScalarGridSpec(
        num_scalar_prefetch=4,
        grid=(n // tm,),
        in_specs=[
            pl.BlockSpec((tm, d), lambda i, *_: (i, 0)),
            pl.BlockSpec((tm, LANES), lambda i, *_: (i, 0)),
            pl.BlockSpec((1, 6, d), lambda i, *_: (i // per_seq, 0, 0)),
            pl.BlockSpec((1, d), lambda i, *_: (0, 0)),
            pl.BlockSpec(memory_space=pl.ANY),
        ],
        out_specs=pl.BlockSpec((tm, d), lambda i, *_: (i, 0)),
        scratch_shapes=[pltpu.VMEM((2, TOP_K * tm * sub, LANES), F32),
                        pltpu.VMEM((TOP_K, tm * sub, LANES), F32),
                        pltpu.SemaphoreType.DMA((2,))],
    )
    return pl.pallas_call(
        functools.partial(_combine_kernel, final=final),
        grid_spec=grid_spec,
        out_shape=jax.ShapeDtypeStruct((n, d), F32),
        compiler_params=_params("arbitrary"),
    )(layout["pos"], layout["cnt"], layout["glob"], layout["loc"], x1, route, mod, final_g.reshape(1, d), yb)


def _moe_layout(route, n, seq):
    g = FFN_ROWS
    a = n * TOP_K
    ta = _tile(seq, MOE_TILE_TOKENS, 8) * TOP_K
    tiles = a // ta
    assert ta < (1 << MOE_RUN_BITS)
    e_flat = route[:, :TOP_K].astype(jnp.int32).reshape(-1)
    onehot = (e_flat[None, :] == jnp.arange(N_EXPERTS, dtype=jnp.int32)[:, None]).astype(jnp.int32)
    csum = jnp.cumsum(onehot, axis=1)
    rank = jnp.sum(csum * onehot, axis=0) - 1
    counts = csum[:, -1]
    padded = (counts + g - 1) // g * g
    pad_ends = jnp.cumsum(padded)
    pad_starts = pad_ends - padded
    dest = jnp.sum(pad_starts[:, None] * onehot, axis=0) + rank
    tile_end = csum[:, ta - 1::ta]
    cnt = tile_end - jnp.concatenate([jnp.zeros((N_EXPERTS, 1), jnp.int32), tile_end[:, :-1]], axis=1)
    glob = pad_starts[:, None] + tile_end - cnt
    loc = jnp.cumsum(cnt, axis=0) - cnt
    per_assignment = lambda v: jnp.sum(
        jnp.broadcast_to(v[:, :, None], (N_EXPERTS, tiles, ta)).reshape(N_EXPERTS, a) * onehot, axis=0)
    pos = per_assignment(loc - glob) + dest
    p_total = (-(-a // g) + N_EXPERTS) * g
    blk_start = jnp.arange(p_total // g, dtype=jnp.int32) * g
    blk_e = jnp.minimum(jnp.sum((pad_ends[:, None] <= blk_start[None, :]).astype(jnp.int32), axis=0),
                        N_EXPERTS - 1)
    i32 = lambda v: v.astype(jnp.int32).reshape(-1)
    n_used = pad_ends[-1:] // g
    cnt, glob, loc = cnt.T, glob.T, loc.T
    return dict(pos=i32(pos), cnt=i32(cnt), glob=i32(glob), loc=i32(loc), blk_e=i32(blk_e), n_used=i32(n_used),
                pad_row=i32(pad_starts + counts), pad_len=i32(jnp.concatenate([padded - counts, n_used])),
                p_total=p_total)


def _moe_ffn(h_tiles, x1, route, mod, wg, wu, wd, final_g, seq, final):
    n, d = x1.shape
    layout = _moe_layout(route, n, seq)
    xb = _dispatch(h_tiles, layout, seq, d)
    yb = _moe_grouped(xb, layout["blk_e"], layout["n_used"], wg, wu, wd)
    return _combine(x1, yb, layout, route, mod, final_g, seq, final)


def kernel(x, c, norm_mix_g, norm_ffn_g, ada_w, ada_b, w_in, w_out, att_out_g, ret_out_g, ffn_w_gate,
           ffn_w_up, ffn_w_down, router_w, moe_w_gate, moe_w_up, moe_w_down, final_norm_g):
    batch, seq, d = x.shape
    depth = ada_w.shape[0]
    assert seq % MOBA_BLOCK == 0 and seq % RET_BLOCK == 0
    assert w_in.shape[2] == N_GROUPS * GROUP_WIDTH
    mods = _adaln(c, ada_w, ada_b).reshape(depth, batch, 6, d)
    tables = _rotary_tables(seq)
    xf = x.reshape(batch * seq, d)
    flat = lambda w: w.reshape(-1, w.shape[-1])
    moe_bf16 = {}
    for l in range(depth):
        mod = mods[l]
        final = l == depth - 1
        proj = _inproj(xf, mod, norm_mix_g[l], w_in[l].astype(BF16), tables, seq)
        o_r = _retention(proj, ret_out_g[l], batch, seq)
        if l % 2 == 0:
            nxt = (l + 1) // 2
            ride = l + 1 < depth
            o_a = _moba(proj, batch, seq, rider=flat(moe_w_gate[nxt]) if ride else None)
            if ride:
                o_a, gate_bf16 = o_a
                moe_bf16[nxt, "gate"] = gate_bf16.reshape(moe_w_gate[nxt].shape)
            xf = _outproj_ffn(o_a, o_r, xf, w_out[l].astype(BF16), att_out_g[l], mod, norm_ffn_g[l],
                              ffn_w_gate[l // 2].astype(BF16), ffn_w_up[l // 2].astype(BF16),
                              ffn_w_down[l // 2].astype(BF16), final_norm_g, seq, final,
                              rider=flat(moe_w_down[nxt]) if ride else None)
            if ride:
                xf, down_bf16 = xf
                moe_bf16[nxt, "down"] = down_bf16.reshape(moe_w_down[nxt].shape)
        else:
            e = l // 2
            o_a, up_bf16 = _moba(proj, batch, seq, rider=flat(moe_w_up[e]))
            x1, h, route = _outproj_route(o_a, o_r, xf, w_out[l].astype(BF16), att_out_g[l], mod, norm_ffn_g[l],
                                          seq, router_w[e])
            xf = _moe_ffn(h, x1, route, mod, moe_bf16[e, "gate"], up_bf16.reshape(moe_w_up[e].shape),
                          moe_bf16[e, "down"], final_norm_g, seq, final)
    return xf.reshape(batch, seq, d)
```

```python
import functools

import jax
import jax.numpy as jnp
from jax import lax
from jax.experimental import pallas as pl
from jax.experimental.pallas import tpu as pltpu

F32 = jnp.float32
BF16 = jnp.bfloat16

LANES = 128
V7X_MXU_WIDTH = 256
V7X_VMEM_LIMIT_BYTES = 56 * 1024 * 1024

ATT_HEADS = 8
ATT_HEAD_DIM = 64
ATT_WIDTH = ATT_HEADS * ATT_HEAD_DIM
MOBA_BLOCK = 256
MOBA_TOPK = 3
MOBA_PAIRS_PER_STEP = 2
MOBA_RANK_ROWS = 8
MOBA_GATE_ROWS = 16
MASK_VALUE = -1e30
MOBA_Q_SCALE = ATT_HEAD_DIM ** -0.5 * 1.4426950408889634
RET_HEADS = 4
RET_HEAD_DIM = 128
RET_WIDTH = RET_HEADS * RET_HEAD_DIM
RET_BLOCK = 256
ROPE_BASE = 10000.0
N_EXPERTS = 8
TOP_K = 2
EPS = 1e-6
GROUP_WIDTH = 512
N_GROUPS = 7
MOE_TILE_TOKENS = 512
MOE_RUN_BITS = 11
FFN_ROWS = 512


def _tile(total, target, mult):
    best = None
    t = mult
    while t <= min(total, target):
        if total % t == 0:
            best = t
        t += mult
    return best if best is not None else total


def _params(*sem):
    return pltpu.CompilerParams(dimension_semantics=sem, vmem_limit_bytes=V7X_VMEM_LIMIT_BYTES)


def _rms(x):
    return x * lax.rsqrt(jnp.mean(x * x, axis=-1, keepdims=True) + EPS)


def _silu(x):
    return x * jax.nn.sigmoid(x)


def _store_token_tiles(ref, x, row0=0):
    rows, d = x.shape
    sub = d // LANES
    for s in range(sub):
        ref[pl.ds(row0 * sub + s, rows, stride=sub), :] = x[:, s * LANES:(s + 1) * LANES].astype(ref.dtype)


def _load_token_tiles(ref, rows, d, row0=0):
    sub = d // LANES
    return jnp.concatenate([ref[pl.ds(row0 * sub + s, rows, stride=sub), :] for s in range(sub)], axis=1)


def _token_tile(ref, t, sub):
    return ref.at[pl.ds(pl.multiple_of(t * sub, sub), sub)]


def _adaln_kernel(c_ref, w_ref, b_ref, o_ref):
    cs = _silu(c_ref[...]).astype(BF16)
    o_ref[0] = jnp.dot(cs, w_ref[0].astype(BF16), preferred_element_type=F32) + b_ref[0]


def _adaln(c, ada_w, ada_b):
    depth, d, w = ada_w.shape
    b = c.shape[0]
    tn = _tile(w, 1536, LANES)
    return pl.pallas_call(
        _adaln_kernel,
        grid=(depth, w // tn),
        in_specs=[
            pl.BlockSpec((b, d), lambda l, j: (0, 0)),
            pl.BlockSpec((1, d, tn), lambda l, j: (l, 0, j)),
            pl.BlockSpec((1, 1, tn), lambda l, j: (l, 0, j)),
        ],
        out_specs=pl.BlockSpec((1, b, tn), lambda l, j: (l, 0, j)),
        out_shape=jax.ShapeDtypeStruct((depth, b, w), F32),
        compiler_params=_params("arbitrary", "arbitrary"),
    )(c, ada_w, ada_b.reshape(depth, 1, w))


def _inproj_kernel(x_ref, mod_ref, g_ref, w_ref, cq_ref, sq_ref, ck_ref, sk_ref, o_ref):
    m = mod_ref[0]
    h = (_rms(x_ref[...]) * g_ref[...] * (1.0 + m[1:2]) + m[0:1]).astype(BF16)
    rotary = {3: (cq_ref, sq_ref), 4: (ck_ref, sk_ref)}
    for j in range(N_GROUPS):
        c0 = j * GROUP_WIDTH
        acc = jnp.dot(h, w_ref[:, c0:c0 + GROUP_WIDTH].astype(BF16), preferred_element_type=F32)
        if j in rotary:
            cos = rotary[j][0][...]
            sin = rotary[j][1][...]
            for hd in range(RET_HEADS):
                a = acc[:, hd * RET_HEAD_DIM:(hd + 1) * RET_HEAD_DIM]
                r = a * cos + pltpu.roll(a, RET_HEAD_DIM // 2, 1) * sin
                o_ref[:, c0 + hd * RET_HEAD_DIM:c0 + (hd + 1) * RET_HEAD_DIM] = r.astype(o_ref.dtype)
        elif j == 0:
            o_ref[:, c0:c0 + GROUP_WIDTH] = (acc * MOBA_Q_SCALE).astype(o_ref.dtype)
        else:
            o_ref[:, c0:c0 + GROUP_WIDTH] = acc.astype(o_ref.dtype)


def _rotary_tables(s):
    half = RET_HEAD_DIM // 2
    inv_freq = ROPE_BASE ** (-jnp.arange(half, dtype=F32) / half)
    ang = jnp.arange(s, dtype=F32)[:, None] * inv_freq[None, :]
    cos = jnp.cos(ang)
    sin = jnp.sin(ang)
    cos2 = jnp.concatenate([cos, cos], axis=-1)
    sin2 = jnp.concatenate([-sin, sin], axis=-1)
    k_scale = RET_HEAD_DIM ** -0.5
    return cos2, sin2, cos2 * k_scale, sin2 * k_scale


def _inproj(x2d, mod, g, w, tables, seq):
    n, d = x2d.shape
    tm = _tile(seq, 1024, 16)
    per_seq = seq // tm
    width = N_GROUPS * GROUP_WIDTH
    return pl.pallas_call(
        _inproj_kernel,
        grid=(n // tm,),
        in_specs=[
            pl.BlockSpec((tm, d), lambda i: (i, 0)),
            pl.BlockSpec((1, 6, d), lambda i: (i // per_seq, 0, 0)),
            pl.BlockSpec((1, d), lambda i: (0, 0)),
            pl.BlockSpec((d, width), lambda i: (0, 0)),
        ] + [pl.BlockSpec((tm, RET_HEAD_DIM), lambda i: (i % per_seq, 0))] * 4,
        out_specs=pl.BlockSpec((tm, width), lambda i: (i, 0)),
        out_shape=jax.ShapeDtypeStruct((n, width), BF16),
        compiler_params=_params("arbitrary"),
    )(x2d, mod, g.reshape(1, d), w, *tables)


def _cast_rider_specs(rider, steps, index):
    rows, cols = rider.shape
    assert rows % steps == 0 and (rows // steps) % 16 == 0
    spec = pl.BlockSpec((rows // steps, cols), index)
    return spec, spec, jax.ShapeDtypeStruct(rider.shape, BF16)


def _moba_kernel(q_ref, k_ref, v_ref, *rest):
    riders = (len(rest) - 1) // 2
    o_ref = rest[riders]
    for w_ref, w_bf16_ref in zip(rest[:riders], rest[riders + 1:]):
        w_bf16_ref[...] = w_ref[...].astype(BF16)
    s = q_ref.shape[0]
    blk = MOBA_BLOCK
    nb = s // blk
    assert nb <= MOBA_RANK_ROWS
    n_sel = min(MOBA_TOPK, nb - 1)
    shift = blk.bit_length() - 1
    contract_lanes = (((1,), (1,)), ((), ()))
    lane = lax.broadcasted_iota(jnp.int32, (1, LANES), 1)
    head_of_lane = jnp.right_shift(lane, ATT_HEAD_DIM.bit_length() - 1)

    key_pos = lax.broadcasted_iota(jnp.int32, (s, LANES), 0)
    key_lane = lax.broadcasted_iota(jnp.int32, (s, LANES), 1)
    key_block_onehot = jnp.where(jnp.right_shift(key_pos, shift) == key_lane, 1.0, 0.0).astype(BF16)
    ones = jnp.ones((s, LANES), BF16)

    blk_id = lax.broadcasted_iota(jnp.int32, (MOBA_RANK_ROWS, s), 0)
    q_blk = jnp.right_shift(lax.broadcasted_iota(jnp.int32, (MOBA_RANK_ROWS, s), 1), shift)
    past = blk_id < q_blk
    block_mean = jnp.where(
        jnp.right_shift(lax.broadcasted_iota(jnp.int32, (MOBA_GATE_ROWS, s), 1), shift)
        == lax.broadcasted_iota(jnp.int32, (MOBA_GATE_ROWS, s), 0), 1.0 / blk, 0.0).astype(BF16)
    no_bias = jnp.zeros((MOBA_GATE_ROWS - MOBA_RANK_ROWS, s), F32)
    eye = jnp.where(lax.broadcasted_iota(jnp.int32, (MOBA_GATE_ROWS, LANES), 0)
                    == lax.broadcasted_iota(jnp.int32, (MOBA_GATE_ROWS, LANES), 1), 1.0, 0.0).astype(BF16)
    row = lax.broadcasted_iota(jnp.int32, (2 * blk, blk), 0)
    col = lax.broadcasted_iota(jnp.int32, (2 * blk, blk), 1)
    causal_bias = jnp.where(col <= jnp.bitwise_and(row, blk - 1), 0.0, MASK_VALUE)

    def prepare(pair):
        cols = slice(pair * LANES, (pair + 1) * LANES)
        q_all = q_ref[:, cols]
        k_all = k_ref[:, cols]
        k_aug = jnp.concatenate([k_all, key_block_onehot], axis=1)
        v_aug = jnp.concatenate([v_ref[:, cols], ones], axis=1)
        k_mean = jnp.dot(block_mean, k_all, preferred_element_type=F32).astype(BF16)

        heads = []
        for hh in range(2):
            qh = jnp.where(head_of_lane == hh, q_all, jnp.zeros_like(q_all))
            gate = lax.dot_general(k_mean, qh, contract_lanes, preferred_element_type=F32)
            gm = jnp.where(past, gate[:MOBA_RANK_ROWS], MASK_VALUE)
            rank = jnp.zeros(gm.shape, jnp.int32)
            for jp in range(nb):
                cv = gm[jp:jp + 1, :]
                beats = jnp.logical_or(cv > gm, jnp.logical_and(cv == gm, jp < blk_id))
                rank = rank + jnp.where(beats, 1, 0)
            visible = jnp.logical_or(jnp.logical_and(rank < n_sel, past), blk_id >= q_blk)
            bias_t = jnp.concatenate([jnp.where(visible, 0.0, MASK_VALUE), no_bias], axis=0).astype(BF16)
            bias = lax.dot_general(bias_t, eye, (((0,), (0,)), ((), ())), preferred_element_type=F32)
            heads.append(jnp.concatenate([qh, bias.astype(BF16)], axis=1))
        return cols, heads, k_aug, v_aug

    def scores(i, pair):
        _, heads, k_aug, _ = pair
        rows = slice(i * blk, (i + 1) * blk)
        q_both = jnp.concatenate([heads[0][rows], heads[1][rows]], axis=0)
        return lax.dot_general(q_both, k_aug[:(i + 1) * blk], contract_lanes, preferred_element_type=F32)

    def softmax_pv(i, pair, sc):
        cols, _, _, v_aug = pair
        sc_own = sc[:, i * blk:] + causal_bias
        chunks = [sc_own[:, c:c + LANES] for c in range(0, blk, LANES)]
        chunks += [sc[:, c:c + LANES] for c in range(0, i * blk, LANES)]
        m = jnp.max(functools.reduce(jnp.maximum, chunks), axis=-1, keepdims=True)
        if i > 0:
            p = jnp.concatenate([jnp.exp2(sc[:, :i * blk] - m), jnp.exp2(sc_own - m)], axis=1)
        else:
            p = jnp.exp2(sc_own - m)
        o = jnp.dot(p.astype(BF16), v_aug[:(i + 1) * blk], preferred_element_type=F32)
        o = o[:, :LANES] / o[:, LANES:]
        o_ref[i * blk:(i + 1) * blk, cols] = jnp.where(head_of_lane == 0, o[:blk], o[blk:]).astype(o_ref.dtype)

    pairs = {}
    for i in reversed(range(nb)):
        for idx in range(q_ref.shape[1] // LANES):
            if idx not in pairs:
                pairs[idx] = prepare(idx)
            softmax_pv(i, pairs[idx], scores(i, pairs[idx]))


def _moba(proj, batch, seq, riders=()):
    n = proj.shape[0]
    width = MOBA_PAIRS_PER_STEP * LANES
    steps = ATT_WIDTH // width
    spec = lambda off: pl.BlockSpec((seq, width), lambda b, p: (b, off + p))
    in_specs = [spec(0), spec(steps), spec(2 * steps)]
    out_specs = [pl.BlockSpec((seq, width), lambda b, p: (b, p))]
    out_shape = [jax.ShapeDtypeStruct((n, ATT_WIDTH), BF16)]
    args = [proj, proj, proj]
    for rider in riders:
        r_in, r_out, r_shape = _cast_rider_specs(rider, batch * steps, lambda b, p: (b * steps + p, 0))
        in_specs.append(r_in)
        out_specs.append(r_out)
        out_shape.append(r_shape)
        args.append(rider)
    return pl.pallas_call(
        _moba_kernel,
        grid=(batch, steps),
        in_specs=in_specs,
        out_specs=out_specs,
        out_shape=out_shape,
        compiler_params=_params("arbitrary", "arbitrary"),
    )(*args)


def _retention_kernel(q_ref, k_ref, v_ref, g_ref, dm_ref, qd_ref, kd_ref, cd_ref, og_ref, o_ref):
    s = q_ref.shape[0]
    c = RET_BLOCK
    states = [jnp.zeros((RET_HEAD_DIM, RET_HEAD_DIM), F32) for _ in range(RET_HEADS)]
    for n in range(s // c):
        rows = slice(n * c, (n + 1) * c)
        for hd in range(RET_HEADS):
            cols = slice(hd * RET_HEAD_DIM, (hd + 1) * RET_HEAD_DIM)
            qc = q_ref[rows, cols]
            kc = k_ref[rows, cols]
            vc = v_ref[rows, cols]
            scores = lax.dot_general(qc, kc, (((1,), (1,)), ((), ())), preferred_element_type=F32) * dm_ref[hd]
            inner = jnp.dot(scores.astype(BF16), vc, preferred_element_type=F32)
            cross = jnp.dot(qc, states[hd].astype(BF16), preferred_element_type=F32) * qd_ref[hd]
            o = _rms(inner + cross) * og_ref[hd]
            gate = g_ref[rows, cols].astype(F32)
            o_ref[rows, cols] = (_silu(gate) * o).astype(o_ref.dtype)
            k_dec = (kc.astype(F32) * kd_ref[hd]).astype(BF16)
            kv = lax.dot_general(k_dec, vc, (((0,), (0,)), ((), ())), preferred_element_type=F32)
            states[hd] = states[hd] * cd_ref[hd] + kv


def _retention_tables():
    h, c = RET_HEADS, RET_BLOCK
    lg = jnp.log(1.0 - 2.0 ** (-5.0 - jnp.arange(h, dtype=F32)))
    idx = jnp.arange(c, dtype=F32)
    diff = idx[:, None] - idx[None, :]
    dmask = jnp.where(diff >= 0, jnp.exp(jnp.maximum(diff, 0.0)[None] * lg[:, None, None]), 0.0)
    rows = lambda v: jnp.broadcast_to(v[:, :, None], (h, c, RET_HEAD_DIM))
    qdec = rows(jnp.exp((idx + 1.0)[None, :] * lg[:, None]))
    kdec = rows(jnp.exp((c - 1 - idx)[None, :] * lg[:, None]))
    cdec = jnp.broadcast_to(jnp.exp(c * lg)[:, None, None], (h, RET_HEAD_DIM, RET_HEAD_DIM))
    return dmask, qdec, kdec, cdec


def _retention(proj, ret_out_g, batch, seq):
    n = proj.shape[0]
    base = 3 * ATT_WIDTH // RET_WIDTH
    spec = lambda off: pl.BlockSpec((seq, RET_WIDTH), lambda b: (b, base + off))
    tables = _retention_tables()
    whole = lambda t: pl.BlockSpec(t.shape, lambda b: (0, 0, 0))
    return pl.pallas_call(
        _retention_kernel,
        grid=(batch,),
        in_specs=[spec(0), spec(1), spec(2), spec(3)] + [whole(t) for t in tables]
                 + [pl.BlockSpec((RET_HEADS, 1, RET_HEAD_DIM), lambda b: (0, 0, 0))],
        out_specs=pl.BlockSpec((seq, RET_WIDTH), lambda b: (b, 0)),
        out_shape=jax.ShapeDtypeStruct((n, RET_WIDTH), BF16),
        compiler_params=_params("arbitrary"),
    )(proj, proj, proj, proj, *tables, ret_out_g.reshape(RET_HEADS, 1, RET_HEAD_DIM))


def _top2_route(logits):
    lane = lax.broadcasted_iota(jnp.int32, logits.shape, 1)
    lg = jnp.where(lane < N_EXPERTS, logits, -jnp.inf)
    m1 = jnp.max(lg, axis=-1, keepdims=True)
    i1 = jnp.min(jnp.where(lg == m1, lane, LANES), axis=-1, keepdims=True)
    lg2 = jnp.where(lane == i1, -jnp.inf, lg)
    m2 = jnp.max(lg2, axis=-1, keepdims=True)
    i2 = jnp.min(jnp.where(lg2 == m2, lane, LANES), axis=-1, keepdims=True)
    e = jnp.exp(m2 - m1)
    w1 = 1.0 / (1.0 + e)
    w2 = e / (1.0 + e)
    out = jnp.where(lane == 0, i1.astype(F32), 0.0)
    out = jnp.where(lane == 1, i2.astype(F32), out)
    out = jnp.where(lane == 2, w1, out)
    return jnp.where(lane == 3, w2, out)


def _mix_and_norm(oa_ref, or_ref, x_ref, w_ref, ag_ref, m, ng_ref):
    oa = (_rms(oa_ref[...].astype(F32)) * ag_ref[...]).astype(BF16)
    mix = (jnp.dot(oa, w_ref[:ATT_WIDTH, :].astype(BF16), preferred_element_type=F32)
           + jnp.dot(or_ref[...], w_ref[ATT_WIDTH:, :].astype(BF16), preferred_element_type=F32))
    x1 = x_ref[...] + m[2:3] * mix
    h = _rms(x1) * ng_ref[...] * (1.0 + m[4:5]) + m[3:4]
    return x1, h


def _mixer_specs(tm, d, per_seq):
    row = lambda w: pl.BlockSpec((tm, w), lambda i: (i, 0))
    full = lambda a, b: pl.BlockSpec((a, b), lambda i: (0, 0))
    return [row(ATT_WIDTH), row(RET_WIDTH), row(d), full(ATT_WIDTH + RET_WIDTH, d), full(1, ATT_WIDTH),
            pl.BlockSpec((1, 6, d), lambda i: (i // per_seq, 0, 0)), full(1, d)]


def _outproj_route_kernel(oa_ref, or_ref, x_ref, w_ref, ag_ref, mod_ref, ng_ref, rw_ref, x1_ref, h_ref, rt_ref):
    x1, h = _mix_and_norm(oa_ref, or_ref, x_ref, w_ref, ag_ref, mod_ref[0], ng_ref)
    x1_ref[...] = x1
    _store_token_tiles(h_ref, h)
    rt_ref[...] = _top2_route(jnp.dot(h.astype(BF16), rw_ref[...], preferred_element_type=F32))


def _outproj_route(o_a, o_r, x2d, w, att_g, mod, norm_g, seq, router_w):
    n, d = x2d.shape
    tm = _tile(seq, 512, 16)
    sub = d // LANES
    rw = jnp.zeros((d, LANES), BF16).at[:, :N_EXPERTS].set(router_w.astype(BF16))
    return pl.pallas_call(
        _outproj_route_kernel,
        grid=(n // tm,),
        in_specs=_mixer_specs(tm, d, seq // tm) + [pl.BlockSpec((d, LANES), lambda i: (0, 0))],
        out_specs=[pl.BlockSpec((tm, d), lambda i: (i, 0)),
                   pl.BlockSpec((tm * sub, LANES), lambda i: (i, 0)),
                   pl.BlockSpec((tm, LANES), lambda i: (i, 0))],
        out_shape=[jax.ShapeDtypeStruct((n, d), F32), jax.ShapeDtypeStruct((n * sub, LANES), F32),
                   jax.ShapeDtypeStruct((n, LANES), F32)],
        compiler_params=_params("arbitrary"),
    )(o_a, o_r, x2d, w, att_g.reshape(1, ATT_WIDTH), mod, norm_g.reshape(1, d), rw)


def _swiglu(h, wg_ref, wu_ref, wd_ref, act_ref):
    ff = act_ref.shape[1]
    tf = _tile(ff, V7X_MXU_WIDTH, LANES)
    for c0 in range(0, ff, tf):
        g = jnp.dot(h, wg_ref[:, c0:c0 + tf], preferred_element_type=F32)
        u = jnp.dot(h, wu_ref[:, c0:c0 + tf], preferred_element_type=F32)
        act_ref[:, c0:c0 + tf] = (_silu(g) * u).astype(act_ref.dtype)
    return jnp.dot(act_ref[...], wd_ref[...], preferred_element_type=F32)


def _outproj_ffn_kernel(oa_ref, or_ref, x_ref, w_ref, ag_ref, mod_ref, ng_ref, wg_ref, wu_ref, wd_ref, fg_ref,
                        *rest, final):
    if len(rest) == 4:
        rider_ref, o_ref, rider_bf16_ref, act_ref = rest
        rider_bf16_ref[...] = rider_ref[...].astype(BF16)
    else:
        o_ref, act_ref = rest
    m = mod_ref[0]
    x1, h = _mix_and_norm(oa_ref, or_ref, x_ref, w_ref, ag_ref, m, ng_ref)
    y = _swiglu(h.astype(BF16), wg_ref, wu_ref, wd_ref, act_ref)
    x2 = x1 + m[5:6] * y
    o_ref[...] = _rms(x2) * fg_ref[...] if final else x2


def _outproj_ffn(o_a, o_r, x2d, w, att_g, mod, norm_g, wg, wu, wd, final_g, seq, final, rider=None):
    n, d = x2d.shape
    ff = wg.shape[1]
    tm = _tile(seq, FFN_ROWS, 16)
    whole = lambda t: pl.BlockSpec(t.shape, lambda i: (0, 0))
    in_specs = _mixer_specs(tm, d, seq // tm) + [whole(wg), whole(wu), whole(wd), pl.BlockSpec((1, d), lambda i: (0, 0))]
    out_specs = [pl.BlockSpec((tm, d), lambda i: (i, 0))]
    out_shape = [jax.ShapeDtypeStruct((n, d), F32)]
    args = [o_a, o_r, x2d, w, att_g.reshape(1, ATT_WIDTH), mod, norm_g.reshape(1, d), wg, wu, wd,
            final_g.reshape(1, d)]
    if rider is not None:
        r_in, r_out, r_shape = _cast_rider_specs(rider, n // tm, lambda i: (i, 0))
        in_specs.append(r_in)
        out_specs.append(r_out)
        out_shape.append(r_shape)
        args.append(rider)
    out = pl.pallas_call(
        functools.partial(_outproj_ffn_kernel, final=final),
        grid=(n // tm,),
        in_specs=in_specs,
        out_specs=out_specs,
        out_shape=out_shape,
        scratch_shapes=[pltpu.VMEM((tm, ff), BF16)],
        compiler_params=_params("arbitrary"),
    )(*args)
    return out if rider is not None else out[0]


def _moe_ffn_kernel(blk_e_ref, n_used_ref, x_ref, wg_ref, wu_ref, wd_ref, o_ref, xs_ref, act_ref):
    del blk_e_ref
    g, d = xs_ref.shape

    @pl.when(pl.program_id(0) < n_used_ref[0])
    def _():
        xs_ref[...] = _load_token_tiles(x_ref, g, d).astype(BF16)
        _store_token_tiles(o_ref, _swiglu(xs_ref[...], wg_ref.at[0], wu_ref.at[0], wd_ref.at[0], act_ref))

    @pl.when(pl.program_id(0) >= n_used_ref[0])
    def _():
        o_ref[...] = jnp.zeros_like(o_ref)


def _moe_grouped(xb, blk_e, n_used, wg, wu, wd):
    _, d, ff = wg.shape
    sub = d // LANES
    g = FFN_ROWS
    grid_spec = pltpu.PrefetchScalarGridSpec(
        num_scalar_prefetch=2,
        grid=(xb.shape[0] // (g * sub),),
        in_specs=[
            pl.BlockSpec((g * sub, LANES), lambda b, be, nu: (b, 0)),
            pl.BlockSpec((1, d, ff), lambda b, be, nu: (be[b], 0, 0)),
            pl.BlockSpec((1, d, ff), lambda b, be, nu: (be[b], 0, 0)),
            pl.BlockSpec((1, ff, d), lambda b, be, nu: (be[b], 0, 0)),
        ],
        out_specs=pl.BlockSpec((g * sub, LANES), lambda b, be, nu: (b, 0)),
        scratch_shapes=[pltpu.VMEM((g, d), BF16), pltpu.VMEM((g, ff), BF16)],
    )
    return pl.pallas_call(
        _moe_ffn_kernel,
        grid_spec=grid_spec,
        out_shape=jax.ShapeDtypeStruct(xb.shape, F32),
        compiler_params=_params("arbitrary"),
    )(blk_e, n_used, xb, wg, wu, wd)


def _start_run_copies(src_ref, dst_ref, src_row, dst_row, count, sub, sem, wait=False):
    for bit in range(MOE_RUN_BITS):
        size = 1 << bit
        off = jnp.left_shift(jnp.right_shift(count, bit + 1), bit + 1)

        @pl.when(jnp.bitwise_and(jnp.right_shift(count, bit), 1) == 1)
        def _():
            copy = pltpu.make_async_copy(
                src_ref.at[pl.ds(pl.multiple_of((src_row + off) * sub, sub), size * sub)],
                dst_ref.at[pl.ds(pl.multiple_of((dst_row + off) * sub, sub), size * sub)], sem)
            copy.wait() if wait else copy.start()


def _dispatch_kernel(pos_ref, cnt_ref, glob_ref, loc_ref, pad_row_ref, pad_len_ref, h_ref, xb_ref,
                     sbuf, zbuf, sems, *, sub):
    i = pl.program_id(0)
    last = pl.num_programs(0) - 1
    tm = h_ref.shape[0] // sub
    ta = tm * TOP_K
    slot = lax.rem(i, 2)
    g = zbuf.shape[0] // sub
    blocks = xb_ref.shape[0] // (g * sub)

    def zero_fill(wait):
        for e in range(N_EXPERTS):
            _start_run_copies(zbuf, xb_ref, 0, pad_row_ref[e], pad_len_ref[e], sub, sems.at[2], wait)
            unused = pad_len_ref[N_EXPERTS] + e

            @pl.when(unused < blocks)
            def _():
                copy = pltpu.make_async_copy(
                    zbuf, xb_ref.at[pl.ds(pl.multiple_of(unused * (g * sub), g * sub), g * sub)], sems.at[2])
                copy.wait() if wait else copy.start()

    @pl.when(i == 0)
    def _():
        zbuf[...] = jnp.zeros_like(zbuf)
        zero_fill(False)

    def wait_slot(s):
        pltpu.make_async_copy(sbuf.at[s], xb_ref.at[pl.ds(0, ta * sub)], sems.at[s]).wait()

    @pl.when(i >= 2)
    def _():
        wait_slot(slot)

    def place(t, carry):
        v = _token_tile(h_ref, t, sub)[...]
        for k in range(TOP_K):
            _token_tile(sbuf.at[slot], pos_ref[i * ta + TOP_K * t + k], sub)[...] = v
        return carry

    lax.fori_loop(0, tm, place, 0, unroll=8)

    for e in range(N_EXPERTS):
        r = i * N_EXPERTS + e
        _start_run_copies(sbuf.at[slot], xb_ref, loc_ref[r], glob_ref[r], cnt_ref[r], sub, sems.at[slot])

    @pl.when(i == last)
    def _():
        @pl.when(i >= 1)
        def _():
            wait_slot(1 - slot)
        wait_slot(slot)
        zero_fill(True)


def _dispatch(h_tiles, layout, seq, d):
    sub = d // LANES
    n = h_tiles.shape[0] // sub
    tm = _tile(seq, MOE_TILE_TOKENS, 8)
    grid_spec = pltpu.PrefetchScalarGridSpec(
        num_scalar_prefetch=6,
        grid=(n // tm,),
        in_specs=[pl.BlockSpec((tm * sub, LANES), lambda i, *_: (i, 0))],
        out_specs=pl.BlockSpec(memory_space=pl.ANY),
        scratch_shapes=[pltpu.VMEM((2, tm * TOP_K * sub, LANES), h_tiles.dtype),
                        pltpu.VMEM((FFN_ROWS * sub, LANES), h_tiles.dtype),
                        pltpu.SemaphoreType.DMA((3,))],
    )
    return pl.pallas_call(
        functools.partial(_dispatch_kernel, sub=sub),
        grid_spec=grid_spec,
        out_shape=jax.ShapeDtypeStruct((layout["p_total"] * sub, LANES), h_tiles.dtype),
        compiler_params=_params("arbitrary"),
    )(layout["pos"], layout["cnt"], layout["glob"], layout["loc"], layout["pad_row"], layout["pad_len"], h_tiles)


def _combine_kernel(pos_ref, cnt_ref, glob_ref, loc_ref, x1_ref, rt_ref, mod_ref, fg_ref, yb_ref, o_ref,
                    ybuf, tbuf, sems, *, final):
    i = pl.program_id(0)
    tm, d = x1_ref.shape
    sub = d // LANES
    ta = tm * TOP_K
    slot = lax.rem(i, 2)

    def fetch(tile, s):
        for e in range(N_EXPERTS):
            r = tile * N_EXPERTS + e
            _start_run_copies(yb_ref, ybuf.at[s], glob_ref[r], loc_ref[r], cnt_ref[r], sub, sems.at[s])

    @pl.when(i == 0)
    def _():
        fetch(0, 0)

    @pl.when(i + 1 < pl.num_programs(0))
    def _():
        fetch(i + 1, 1 - slot)

    pltpu.make_async_copy(yb_ref.at[pl.ds(0, ta * sub)], ybuf.at[slot], sems.at[slot]).wait()

    def unpermute(t, carry):
        for k in range(TOP_K):
            _token_tile(tbuf.at[k], t, sub)[...] = _token_tile(ybuf.at[slot], pos_ref[i * ta + TOP_K * t + k], sub)[...]
        return carry

    lax.fori_loop(0, tm, unpermute, 0, unroll=8)

    rt = rt_ref[...]
    y = (rt[:, 2:3] * _load_token_tiles(tbuf.at[0], tm, d) + rt[:, 3:4] * _load_token_tiles(tbuf.at[1], tm, d))
    x2 = x1_ref[...] + mod_ref[0][5:6] * y
    o_ref[...] = _rms(x2) * fg_ref[...] if final else x2


def _combine(x1, yb, layout, route, mod, final_g, seq, final):
    n, d = x1.shape
    sub = d // LANES
    tm = _tile(seq, MOE_TILE_TOKENS, 8)
    per_seq = seq // tm
    grid_spec = pltpu.PrefetchScalarGridSpec(
        num_scalar_prefetch=4,
        grid=(n // tm,),
        in_specs=[
            pl.BlockSpec((tm, d), lambda i, *_: (i, 0)),
            pl.BlockSpec((tm, LANES), lambda i, *_: (i, 0)),
            pl.BlockSpec((1, 6, d), lambda i, *_: (i // per_seq, 0, 0)),
            pl.BlockSpec((1, d), lambda i, *_: (0, 0)),
            pl.BlockSpec(memory_space=pl.ANY),
        ],
        out_specs=pl.BlockSpec((tm, d), lambda i, *_: (i, 0)),
        scratch_shapes=[pltpu.VMEM((2, TOP_K * tm * sub, LANES), F32),
                        pltpu.VMEM((TOP_K, tm * sub, LANES), F32),
                        pltpu.SemaphoreType.DMA((2,))],
    )
    return pl.pallas_call(
        functools.partial(_combine_kernel, final=final),
        grid_spec=grid_spec,
        out_shape=jax.ShapeDtypeStruct((n, d), F32),
        compiler_params=_params("arbitrary"),
    )(layout["pos"], layout["cnt"], layout["glob"], layout["loc"], x1, route, mod, final_g.reshape(1, d), yb)


def _moe_layout(route, n, seq):
    g = FFN_ROWS
    a = n * TOP_K
    ta = _tile(seq, MOE_TILE_TOKENS, 8) * TOP_K
    tiles = a // ta
    assert ta < (1 << MOE_RUN_BITS)
    e_flat = route[:, :TOP_K].astype(jnp.int32).reshape(-1)
    onehot = (e_flat[None, :] == jnp.arange(N_EXPERTS, dtype=jnp.int32)[:, None]).astype(jnp.int32)
    csum = jnp.cumsum(onehot, axis=1)
    rank = jnp.sum(csum * onehot, axis=0) - 1
    counts = csum[:, -1]
    padded = (counts + g - 1) // g * g
    pad_ends = jnp.cumsum(padded)
    pad_starts = pad_ends - padded
    dest = jnp.sum(pad_starts[:, None] * onehot, axis=0) + rank
    tile_end = csum[:, ta - 1::ta]
    cnt = tile_end - jnp.concatenate([jnp.zeros((N_EXPERTS, 1), jnp.int32), tile_end[:, :-1]], axis=1)
    glob = pad_starts[:, None] + tile_end - cnt
    loc = jnp.cumsum(cnt, axis=0) - cnt
    per_assignment = lambda v: jnp.sum(
        jnp.broadcast_to(v[:, :, None], (N_EXPERTS, tiles, ta)).reshape(N_EXPERTS, a) * onehot, axis=0)
    pos = per_assignment(loc - glob) + dest
    p_total = (-(-a // g) + N_EXPERTS) * g
    blk_start = jnp.arange(p_total // g, dtype=jnp.int32) * g
    blk_e = jnp.minimum(jnp.sum((pad_ends[:, None] <= blk_start[None, :]).astype(jnp.int32), axis=0),
                        N_EXPERTS - 1)
    i32 = lambda v: v.astype(jnp.int32).reshape(-1)
    n_used = pad_ends[-1:] // g
    cnt, glob, loc = cnt.T, glob.T, loc.T
    return dict(pos=i32(pos), cnt=i32(cnt), glob=i32(glob), loc=i32(loc), blk_e=i32(blk_e), n_used=i32(n_used),
                pad_row=i32(pad_starts + counts), pad_len=i32(jnp.concatenate([padded - counts, n_used])),
                p_total=p_total)


def _moe_ffn(h_tiles, x1, route, mod, wg, wu, wd, final_g, seq, final):
    n, d = x1.shape
    layout = _moe_layout(route, n, seq)
    xb = _dispatch(h_tiles, layout, seq, d)
    yb = _moe_grouped(xb, layout["blk_e"], layout["n_used"], wg, wu, wd)
    return _combine(x1, yb, layout, route, mod, final_g, seq, final)


def kernel(x, c, norm_mix_g, norm_ffn_g, ada_w, ada_b, w_in, w_out, att_out_g, ret_out_g, ffn_w_gate,
           ffn_w_up, ffn_w_down, router_w, moe_w_gate, moe_w_up, moe_w_down, final_norm_g):
    batch, seq, d = x.shape
    depth = ada_w.shape[0]
    assert seq % MOBA_BLOCK == 0 and seq % RET_BLOCK == 0
    assert w_in.shape[2] == N_GROUPS * GROUP_WIDTH
    mods = _adaln(c, ada_w, ada_b).reshape(depth, batch, 6, d)
    tables = _rotary_tables(seq)
    xf = x.reshape(batch * seq, d)
    flat = lambda w: w.reshape(-1, w.shape[-1])
    moe_bf16 = {}
    for l in range(depth):
        mod = mods[l]
        final = l == depth - 1
        proj = _inproj(xf, mod, norm_mix_g[l], w_in[l], tables, seq)
        o_r = _retention(proj, ret_out_g[l], batch, seq)
        if l % 2 == 0:
            nxt = (l + 1) // 2
            ride = l + 1 < depth
            riders = [ffn_w_gate[l // 2], ffn_w_up[l // 2], ffn_w_down[l // 2]]
            o_a, wg, wu, wd, *rest = _moba(proj, batch, seq, riders + ([flat(moe_w_gate[nxt])] if ride else []))
            if ride:
                moe_bf16[nxt, "gate"] = rest[0].reshape(moe_w_gate[nxt].shape)
            xf = _outproj_ffn(o_a, o_r, xf, w_out[l], att_out_g[l], mod, norm_ffn_g[l], wg, wu, wd,
                              final_norm_g, seq, final, rider=flat(moe_w_down[nxt]) if ride else None)
            if ride:
                xf, down_bf16 = xf
                moe_bf16[nxt, "down"] = down_bf16.reshape(moe_w_down[nxt].shape)
        else:
            e = l // 2
            o_a, up_bf16 = _moba(proj, batch, seq, [flat(moe_w_up[e])])
            x1, h, route = _outproj_route(o_a, o_r, xf, w_out[l], att_out_g[l], mod, norm_ffn_g[l],
                                          seq, router_w[e])
            xf = _moe_ffn(h, x1, route, mod, moe_bf16[e, "gate"], up_bf16.reshape(moe_w_up[e].shape),
                          moe_bf16[e, "down"], final_norm_g, seq, final)
    return xf.reshape(batch, seq, d)
```

```python
import functools

import jax
import jax.numpy as jnp
from jax import lax
from jax.experimental import pallas as pl
from jax.experimental.pallas import tpu as pltpu

F32 = jnp.float32
BF16 = jnp.bfloat16

LANES = 128
V7X_MXU_WIDTH = 256
V7X_VMEM_LIMIT_BYTES = 56 * 1024 * 1024

ATT_HEADS = 8
ATT_HEAD_DIM = 64
ATT_WIDTH = ATT_HEADS * ATT_HEAD_DIM
MOBA_BLOCK = 256
MOBA_TOPK = 3
MOBA_PAIRS_PER_STEP = 2
MOBA_RANK_ROWS = 8
MOBA_GATE_ROWS = 16
MASK_VALUE = -1e30
MOBA_Q_SCALE = ATT_HEAD_DIM ** -0.5 * 1.4426950408889634
RET_HEADS = 4
RET_HEAD_DIM = 128
RET_WIDTH = RET_HEADS * RET_HEAD_DIM
RET_BLOCK = 256
ROPE_BASE = 10000.0
N_EXPERTS = 8
TOP_K = 2
EPS = 1e-6
GROUP_WIDTH = 512
N_GROUPS = 7
MOE_TILE_TOKENS = 512
MOE_RUN_BITS = 11
FFN_ROWS = 512


def _tile(total, target, mult):
    best = None
    t = mult
    while t <= min(total, target):
        if total % t == 0:
            best = t
        t += mult
    return best if best is not None else total


def _params(*sem):
    return pltpu.CompilerParams(dimension_semantics=sem, vmem_limit_bytes=V7X_VMEM_LIMIT_BYTES)


def _rms(x):
    return x * lax.rsqrt(jnp.mean(x * x, axis=-1, keepdims=True) + EPS)


def _silu(x):
    return x * jax.nn.sigmoid(x)


def _store_token_tiles(ref, x, row0=0):
    rows, d = x.shape
    sub = d // LANES
    for s in range(sub):
        ref[pl.ds(row0 * sub + s, rows, stride=sub), :] = x[:, s * LANES:(s + 1) * LANES].astype(ref.dtype)


def _load_token_tiles(ref, rows, d, row0=0):
    sub = d // LANES
    return jnp.concatenate([ref[pl.ds(row0 * sub + s, rows, stride=sub), :] for s in range(sub)], axis=1)


def _token_tile(ref, t, sub):
    return ref.at[pl.ds(pl.multiple_of(t * sub, sub), sub)]


def _adaln_kernel(c_ref, w_ref, b_ref, o_ref):
    cs = _silu(c_ref[...]).astype(BF16)
    o_ref[0] = jnp.dot(cs, w_ref[0].astype(BF16), preferred_element_type=F32) + b_ref[0]


def _adaln(c, ada_w, ada_b):
    depth, d, w = ada_w.shape
    b = c.shape[0]
    tn = _tile(w, 1536, LANES)
    return pl.pallas_call(
        _adaln_kernel,
        grid=(depth, w // tn),
        in_specs=[
            pl.BlockSpec((b, d), lambda l, j: (0, 0)),
            pl.BlockSpec((1, d, tn), lambda l, j: (l, 0, j)),
            pl.BlockSpec((1, 1, tn), lambda l, j: (l, 0, j)),
        ],
        out_specs=pl.BlockSpec((1, b, tn), lambda l, j: (l, 0, j)),
        out_shape=jax.ShapeDtypeStruct((depth, b, w), F32),
        compiler_params=_params("arbitrary", "arbitrary"),
    )(c, ada_w, ada_b.reshape(depth, 1, w))


def _inproj_kernel(x_ref, mod_ref, g_ref, w_ref, cq_ref, sq_ref, ck_ref, sk_ref, o_ref):
    m = mod_ref[0]
    h = (_rms(x_ref[...]) * g_ref[...] * (1.0 + m[1:2]) + m[0:1]).astype(BF16)
    rotary = {3: (cq_ref, sq_ref), 4: (ck_ref, sk_ref)}
    for j in range(N_GROUPS):
        c0 = j * GROUP_WIDTH
        acc = jnp.dot(h, w_ref[:, c0:c0 + GROUP_WIDTH].astype(BF16), preferred_element_type=F32)
        if j in rotary:
            cos = rotary[j][0][...]
            sin = rotary[j][1][...]
            for hd in range(RET_HEADS):
                a = acc[:, hd * RET_HEAD_DIM:(hd + 1) * RET_HEAD_DIM]
                r = a * cos + pltpu.roll(a, RET_HEAD_DIM // 2, 1) * sin
                o_ref[:, c0 + hd * RET_HEAD_DIM:c0 + (hd + 1) * RET_HEAD_DIM] = r.astype(o_ref.dtype)
        elif j == 0:
            o_ref[:, c0:c0 + GROUP_WIDTH] = (acc * MOBA_Q_SCALE).astype(o_ref.dtype)
        else:
            o_ref[:, c0:c0 + GROUP_WIDTH] = acc.astype(o_ref.dtype)


def _rotary_tables(s):
    half = RET_HEAD_DIM // 2
    inv_freq = ROPE_BASE ** (-jnp.arange(half, dtype=F32) / half)
    ang = jnp.arange(s, dtype=F32)[:, None] * inv_freq[None, :]
    cos = jnp.cos(ang)
    sin = jnp.sin(ang)
    cos2 = jnp.concatenate([cos, cos], axis=-1)
    sin2 = jnp.concatenate([-sin, sin], axis=-1)
    k_scale = RET_HEAD_DIM ** -0.5
    return cos2, sin2, cos2 * k_scale, sin2 * k_scale


def _inproj(x2d, mod, g, w_all, layer, tables, seq):
    n, d = x2d.shape
    tm = _tile(seq, 1024, 16)
    per_seq = seq // tm
    width = N_GROUPS * GROUP_WIDTH
    return pl.pallas_call(
        _inproj_kernel,
        grid=(n // tm,),
        in_specs=[
            pl.BlockSpec((tm, d), lambda i: (i, 0)),
            pl.BlockSpec((1, 6, d), lambda i: (i // per_seq, 0, 0)),
            pl.BlockSpec((1, d), lambda i: (0, 0)),
            pl.BlockSpec((None, d, width), lambda i: (layer, 0, 0)),
        ] + [pl.BlockSpec((tm, RET_HEAD_DIM), lambda i: (i % per_seq, 0))] * 4,
        out_specs=pl.BlockSpec((tm, width), lambda i: (i, 0)),
        out_shape=jax.ShapeDtypeStruct((n, width), BF16),
        compiler_params=_params("arbitrary"),
    )(x2d, mod, g.reshape(1, d), w_all, *tables)


def _cast_rider_specs(rider, steps, index):
    rows, cols = rider.shape
    assert rows % steps == 0 and (rows // steps) % 16 == 0
    spec = pl.BlockSpec((rows // steps, cols), index)
    return spec, spec, jax.ShapeDtypeStruct(rider.shape, BF16)


def _moba_kernel(q_ref, k_ref, v_ref, *rest):
    riders = (len(rest) - 1) // 2
    o_ref = rest[riders]
    for w_ref, w_bf16_ref in zip(rest[:riders], rest[riders + 1:]):
        w_bf16_ref[...] = w_ref[...].astype(BF16)
    s = q_ref.shape[0]
    blk = MOBA_BLOCK
    nb = s // blk
    assert nb <= MOBA_RANK_ROWS
    n_sel = min(MOBA_TOPK, nb - 1)
    shift = blk.bit_length() - 1
    contract_lanes = (((1,), (1,)), ((), ()))
    lane = lax.broadcasted_iota(jnp.int32, (1, LANES), 1)
    head_of_lane = jnp.right_shift(lane, ATT_HEAD_DIM.bit_length() - 1)

    key_pos = lax.broadcasted_iota(jnp.int32, (s, LANES), 0)
    key_lane = lax.broadcasted_iota(jnp.int32, (s, LANES), 1)
    key_block_onehot = jnp.where(jnp.right_shift(key_pos, shift) == key_lane, 1.0, 0.0).astype(BF16)
    ones = jnp.ones((s, LANES), BF16)

    blk_id = lax.broadcasted_iota(jnp.int32, (MOBA_RANK_ROWS, s), 0)
    q_blk = jnp.right_shift(lax.broadcasted_iota(jnp.int32, (MOBA_RANK_ROWS, s), 1), shift)
    past = blk_id < q_blk
    block_mean = jnp.where(
        jnp.right_shift(lax.broadcasted_iota(jnp.int32, (MOBA_GATE_ROWS, s), 1), shift)
        == lax.broadcasted_iota(jnp.int32, (MOBA_GATE_ROWS, s), 0), 1.0 / blk, 0.0).astype(BF16)
    no_bias = jnp.zeros((MOBA_GATE_ROWS - MOBA_RANK_ROWS, s), F32)
    eye = jnp.where(lax.broadcasted_iota(jnp.int32, (MOBA_GATE_ROWS, LANES), 0)
                    == lax.broadcasted_iota(jnp.int32, (MOBA_GATE_ROWS, LANES), 1), 1.0, 0.0).astype(BF16)
    row = lax.broadcasted_iota(jnp.int32, (2 * blk, blk), 0)
    col = lax.broadcasted_iota(jnp.int32, (2 * blk, blk), 1)
    causal_bias = jnp.where(col <= jnp.bitwise_and(row, blk - 1), 0.0, MASK_VALUE)

    def prepare(pair):
        cols = slice(pair * LANES, (pair + 1) * LANES)
        q_all = q_ref[:, cols]
        k_all = k_ref[:, cols]
        k_aug = jnp.concatenate([k_all, key_block_onehot], axis=1)
        v_aug = jnp.concatenate([v_ref[:, cols], ones], axis=1)
        k_mean = jnp.dot(block_mean, k_all, preferred_element_type=F32).astype(BF16)

        heads = []
        for hh in range(2):
            qh = jnp.where(head_of_lane == hh, q_all, jnp.zeros_like(q_all))
            gate = lax.dot_general(k_mean, qh, contract_lanes, preferred_element_type=F32)
            gm = jnp.where(past, gate[:MOBA_RANK_ROWS], MASK_VALUE)
            rank = jnp.zeros(gm.shape, jnp.int32)
            for jp in range(nb):
                cv = gm[jp:jp + 1, :]
                beats = jnp.logical_or(cv > gm, jnp.logical_and(cv == gm, jp < blk_id))
                rank = rank + jnp.where(beats, 1, 0)
            visible = jnp.logical_or(jnp.logical_and(rank < n_sel, past), blk_id >= q_blk)
            bias_t = jnp.concatenate([jnp.where(visible, 0.0, MASK_VALUE), no_bias], axis=0).astype(BF16)
            bias = lax.dot_general(bias_t, eye, (((0,), (0,)), ((), ())), preferred_element_type=F32)
            heads.append(jnp.concatenate([qh, bias.astype(BF16)], axis=1))
        return cols, heads, k_aug, v_aug

    def scores(i, pair):
        _, heads, k_aug, _ = pair
        rows = slice(i * blk, (i + 1) * blk)
        q_both = jnp.concatenate([heads[0][rows], heads[1][rows]], axis=0)
        return lax.dot_general(q_both, k_aug[:(i + 1) * blk], contract_lanes, preferred_element_type=F32)

    def softmax_pv(i, pair, sc):
        cols, _, _, v_aug = pair
        sc_own = sc[:, i * blk:] + causal_bias
        chunks = [sc_own[:, c:c + LANES] for c in range(0, blk, LANES)]
        chunks += [sc[:, c:c + LANES] for c in range(0, i * blk, LANES)]
        m = jnp.max(functools.reduce(jnp.maximum, chunks), axis=-1, keepdims=True)
        if i > 0:
            p = jnp.concatenate([jnp.exp2(sc[:, :i * blk] - m), jnp.exp2(sc_own - m)], axis=1)
        else:
            p = jnp.exp2(sc_own - m)
        o = jnp.dot(p.astype(BF16), v_aug[:(i + 1) * blk], preferred_element_type=F32)
        o = o[:, :LANES] / o[:, LANES:]
        o_ref[i * blk:(i + 1) * blk, cols] = jnp.where(head_of_lane == 0, o[:blk], o[blk:]).astype(o_ref.dtype)

    pairs = {}
    for i in reversed(range(nb)):
        for idx in range(q_ref.shape[1] // LANES):
            if idx not in pairs:
                pairs[idx] = prepare(idx)
            softmax_pv(i, pairs[idx], scores(i, pairs[idx]))


def _moba(proj, batch, seq, riders=()):
    n = proj.shape[0]
    width = MOBA_PAIRS_PER_STEP * LANES
    steps = ATT_WIDTH // width
    spec = lambda off: pl.BlockSpec((seq, width), lambda b, p: (b, off + p))
    in_specs = [spec(0), spec(steps), spec(2 * steps)]
    out_specs = [pl.BlockSpec((seq, width), lambda b, p: (b, p))]
    out_shape = [jax.ShapeDtypeStruct((n, ATT_WIDTH), BF16)]
    args = [proj, proj, proj]
    for rider in riders:
        r_in, r_out, r_shape = _cast_rider_specs(rider, batch * steps, lambda b, p: (b * steps + p, 0))
        in_specs.append(r_in)
        out_specs.append(r_out)
        out_shape.append(r_shape)
        args.append(rider)
    return pl.pallas_call(
        _moba_kernel,
        grid=(batch, steps),
        in_specs=in_specs,
        out_specs=out_specs,
        out_shape=out_shape,
        compiler_params=_params("arbitrary", "arbitrary"),
    )(*args)


def _retention_kernel(q_ref, k_ref, v_ref, g_ref, dm_ref, qd_ref, kd_ref, cd_ref, og_ref, o_ref):
    s = q_ref.shape[0]
    c = RET_BLOCK
    states = [jnp.zeros((RET_HEAD_DIM, RET_HEAD_DIM), F32) for _ in range(RET_HEADS)]
    for n in range(s // c):
        rows = slice(n * c, (n + 1) * c)
        for hd in range(RET_HEADS):
            cols = slice(hd * RET_HEAD_DIM, (hd + 1) * RET_HEAD_DIM)
            qc = q_ref[rows, cols]
            kc = k_ref[rows, cols]
            vc = v_ref[rows, cols]
            scores = lax.dot_general(qc, kc, (((1,), (1,)), ((), ())), preferred_element_type=F32) * dm_ref[hd]
            inner = jnp.dot(scores.astype(BF16), vc, preferred_element_type=F32)
            cross = jnp.dot(qc, states[hd].astype(BF16), preferred_element_type=F32) * qd_ref[hd]
            o = _rms(inner + cross) * og_ref[hd]
            gate = g_ref[rows, cols].astype(F32)
            o_ref[rows, cols] = (_silu(gate) * o).astype(o_ref.dtype)
            k_dec = (kc.astype(F32) * kd_ref[hd]).astype(BF16)
            kv = lax.dot_general(k_dec, vc, (((0,), (0,)), ((), ())), preferred_element_type=F32)
            states[hd] = states[hd] * cd_ref[hd] + kv


def _retention_tables():
    h, c = RET_HEADS, RET_BLOCK
    lg = jnp.log(1.0 - 2.0 ** (-5.0 - jnp.arange(h, dtype=F32)))
    idx = jnp.arange(c, dtype=F32)
    diff = idx[:, None] - idx[None, :]
    dmask = jnp.where(diff >= 0, jnp.exp(jnp.maximum(diff, 0.0)[None] * lg[:, None, None]), 0.0)
    rows = lambda v: jnp.broadcast_to(v[:, :, None], (h, c, RET_HEAD_DIM))
    qdec = rows(jnp.exp((idx + 1.0)[None, :] * lg[:, None]))
    kdec = rows(jnp.exp((c - 1 - idx)[None, :] * lg[:, None]))
    cdec = jnp.broadcast_to(jnp.exp(c * lg)[:, None, None], (h, RET_HEAD_DIM, RET_HEAD_DIM))
    return dmask, qdec, kdec, cdec


def _retention(proj, ret_out_g, batch, seq):
    n = proj.shape[0]
    base = 3 * ATT_WIDTH // RET_WIDTH
    spec = lambda off: pl.BlockSpec((seq, RET_WIDTH), lambda b: (b, base + off))
    tables = _retention_tables()
    whole = lambda t: pl.BlockSpec(t.shape, lambda b: (0, 0, 0))
    return pl.pallas_call(
        _retention_kernel,
        grid=(batch,),
        in_specs=[spec(0), spec(1), spec(2), spec(3)] + [whole(t) for t in tables]
                 + [pl.BlockSpec((RET_HEADS, 1, RET_HEAD_DIM), lambda b: (0, 0, 0))],
        out_specs=pl.BlockSpec((seq, RET_WIDTH), lambda b: (b, 0)),
        out_shape=jax.ShapeDtypeStruct((n, RET_WIDTH), BF16),
        compiler_params=_params("arbitrary"),
    )(proj, proj, proj, proj, *tables, ret_out_g.reshape(RET_HEADS, 1, RET_HEAD_DIM))


def _top2_route(logits):
    lane = lax.broadcasted_iota(jnp.int32, logits.shape, 1)
    lg = jnp.where(lane < N_EXPERTS, logits, -jnp.inf)
    m1 = jnp.max(lg, axis=-1, keepdims=True)
    i1 = jnp.min(jnp.where(lg == m1, lane, LANES), axis=-1, keepdims=True)
    lg2 = jnp.where(lane == i1, -jnp.inf, lg)
    m2 = jnp.max(lg2, axis=-1, keepdims=True)
    i2 = jnp.min(jnp.where(lg2 == m2, lane, LANES), axis=-1, keepdims=True)
    e = jnp.exp(m2 - m1)
    w1 = 1.0 / (1.0 + e)
    w2 = e / (1.0 + e)
    out = jnp.where(lane == 0, i1.astype(F32), 0.0)
    out = jnp.where(lane == 1, i2.astype(F32), out)
    out = jnp.where(lane == 2, w1, out)
    return jnp.where(lane == 3, w2, out)


def _mix_and_norm(oa_ref, or_ref, x_ref, w_ref, ag_ref, m, ng_ref):
    oa = (_rms(oa_ref[...].astype(F32)) * ag_ref[...]).astype(BF16)
    mix = (jnp.dot(oa, w_ref[:ATT_WIDTH, :].astype(BF16), preferred_element_type=F32)
           + jnp.dot(or_ref[...], w_ref[ATT_WIDTH:, :].astype(BF16), preferred_element_type=F32))
    x1 = x_ref[...] + m[2:3] * mix
    h = _rms(x1) * ng_ref[...] * (1.0 + m[4:5]) + m[3:4]
    return x1, h


def _mixer_specs(tm, d, per_seq, layer):
    row = lambda w: pl.BlockSpec((tm, w), lambda i: (i, 0))
    full = lambda a, b: pl.BlockSpec((a, b), lambda i: (0, 0))
    w_out = pl.BlockSpec((None, ATT_WIDTH + RET_WIDTH, d), lambda i: (layer, 0, 0))
    return [row(ATT_WIDTH), row(RET_WIDTH), row(d), w_out, full(1, ATT_WIDTH),
            pl.BlockSpec((1, 6, d), lambda i: (i // per_seq, 0, 0)), full(1, d)]


def _outproj_route_kernel(oa_ref, or_ref, x_ref, w_ref, ag_ref, mod_ref, ng_ref, rw_ref,
                          x1_ref, h_ref, rt_ref, rt_t_ref):
    x1, h = _mix_and_norm(oa_ref, or_ref, x_ref, w_ref, ag_ref, mod_ref[0], ng_ref)
    x1_ref[...] = x1
    _store_token_tiles(h_ref, h)
    route = _top2_route(jnp.dot(h.astype(BF16), rw_ref[...], preferred_element_type=F32))
    rt_ref[...] = route
    rt_t_ref[...] = route.T[:rt_t_ref.shape[0]]


def _outproj_route(o_a, o_r, x2d, w_all, layer, att_g, mod, norm_g, seq, router_w):
    n, d = x2d.shape
    tm = _tile(seq, 512, 16)
    sub = d // LANES
    rw = jnp.zeros((d, LANES), BF16).at[:, :N_EXPERTS].set(router_w.astype(BF16))
    return pl.pallas_call(
        _outproj_route_kernel,
        grid=(n // tm,),
        in_specs=_mixer_specs(tm, d, seq // tm, layer) + [pl.BlockSpec((d, LANES), lambda i: (0, 0))],
        out_specs=[pl.BlockSpec((tm, d), lambda i: (i, 0)),
                   pl.BlockSpec((tm * sub, LANES), lambda i: (i, 0)),
                   pl.BlockSpec((tm, LANES), lambda i: (i, 0)),
                   pl.BlockSpec((8, tm), lambda i: (0, i))],
        out_shape=[jax.ShapeDtypeStruct((n, d), F32), jax.ShapeDtypeStruct((n * sub, LANES), F32),
                   jax.ShapeDtypeStruct((n, LANES), F32), jax.ShapeDtypeStruct((8, n), F32)],
        compiler_params=_params("arbitrary"),
    )(o_a, o_r, x2d, w_all, att_g.reshape(1, ATT_WIDTH), mod, norm_g.reshape(1, d), rw)


def _swiglu(h, wg_ref, wu_ref, wd_ref, act_ref):
    ff = act_ref.shape[1]
    tf = _tile(ff, V7X_MXU_WIDTH, LANES)
    for c0 in range(0, ff, tf):
        g = jnp.dot(h, wg_ref[:, c0:c0 + tf], preferred_element_type=F32)
        u = jnp.dot(h, wu_ref[:, c0:c0 + tf], preferred_element_type=F32)
        act_ref[:, c0:c0 + tf] = (_silu(g) * u).astype(act_ref.dtype)
    return jnp.dot(act_ref[...], wd_ref[...], preferred_element_type=F32)


def _outproj_ffn_kernel(oa_ref, or_ref, x_ref, w_ref, ag_ref, mod_ref, ng_ref, wg_ref, wu_ref, wd_ref, fg_ref,
                        *rest, final):
    if len(rest) == 4:
        rider_ref, o_ref, rider_bf16_ref, act_ref = rest
        rider_bf16_ref[...] = rider_ref[...].astype(BF16)
    else:
        o_ref, act_ref = rest
    m = mod_ref[0]
    x1, h = _mix_and_norm(oa_ref, or_ref, x_ref, w_ref, ag_ref, m, ng_ref)
    y = _swiglu(h.astype(BF16), wg_ref, wu_ref, wd_ref, act_ref)
    x2 = x1 + m[5:6] * y
    o_ref[...] = _rms(x2) * fg_ref[...] if final else x2


def _outproj_ffn(o_a, o_r, x2d, w_all, layer, att_g, mod, norm_g, wg, wu, wd, final_g, seq, final, rider=None):
    n, d = x2d.shape
    ff = wg.shape[1]
    tm = _tile(seq, FFN_ROWS, 16)
    whole = lambda t: pl.BlockSpec(t.shape, lambda i: (0, 0))
    in_specs = _mixer_specs(tm, d, seq // tm, layer) + [whole(wg), whole(wu), whole(wd), pl.BlockSpec((1, d), lambda i: (0, 0))]
    out_specs = [pl.BlockSpec((tm, d), lambda i: (i, 0))]
    out_shape = [jax.ShapeDtypeStruct((n, d), F32)]
    args = [o_a, o_r, x2d, w_all, att_g.reshape(1, ATT_WIDTH), mod, norm_g.reshape(1, d), wg, wu, wd,
            final_g.reshape(1, d)]
    if rider is not None:
        r_in, r_out, r_shape = _cast_rider_specs(rider, n // tm, lambda i: (i, 0))
        in_specs.append(r_in)
        out_specs.append(r_out)
        out_shape.append(r_shape)
        args.append(rider)
    out = pl.pallas_call(
        functools.partial(_outproj_ffn_kernel, final=final),
        grid=(n // tm,),
        in_specs=in_specs,
        out_specs=out_specs,
        out_shape=out_shape,
        scratch_shapes=[pltpu.VMEM((tm, ff), BF16)],
        compiler_params=_params("arbitrary"),
    )(*args)
    return out if rider is not None else out[0]


def _moe_ffn_kernel(blk_e_ref, n_used_ref, x_ref, wg_ref, wu_ref, wd_ref, o_ref, xs_ref, act_ref):
    del blk_e_ref
    g, d = xs_ref.shape

    @pl.when(pl.program_id(0) < n_used_ref[0])
    def _():
        xs_ref[...] = _load_token_tiles(x_ref, g, d).astype(BF16)
        _store_token_tiles(o_ref, _swiglu(xs_ref[...], wg_ref.at[0], wu_ref.at[0], wd_ref.at[0], act_ref))

    @pl.when(pl.program_id(0) >= n_used_ref[0])
    def _():
        o_ref[...] = jnp.zeros_like(o_ref)


def _moe_grouped(xb, blk_e, n_used, wg, wu, wd):
    _, d, ff = wg.shape
    sub = d // LANES
    g = FFN_ROWS
    grid_spec = pltpu.PrefetchScalarGridSpec(
        num_scalar_prefetch=2,
        grid=(xb.shape[0] // (g * sub),),
        in_specs=[
            pl.BlockSpec((g * sub, LANES), lambda b, be, nu: (b, 0)),
            pl.BlockSpec((1, d, ff), lambda b, be, nu: (be[b], 0, 0)),
            pl.BlockSpec((1, d, ff), lambda b, be, nu: (be[b], 0, 0)),
            pl.BlockSpec((1, ff, d), lambda b, be, nu: (be[b], 0, 0)),
        ],
        out_specs=pl.BlockSpec((g * sub, LANES), lambda b, be, nu: (b, 0)),
        scratch_shapes=[pltpu.VMEM((g, d), BF16), pltpu.VMEM((g, ff), BF16)],
    )
    return pl.pallas_call(
        _moe_ffn_kernel,
        grid_spec=grid_spec,
        out_shape=jax.ShapeDtypeStruct(xb.shape, F32),
        compiler_params=_params("arbitrary"),
    )(blk_e, n_used, xb, wg, wu, wd)


def _start_run_copies(src_ref, dst_ref, src_row, dst_row, count, sub, sem, wait=False):
    for bit in range(MOE_RUN_BITS):
        size = 1 << bit
        off = jnp.left_shift(jnp.right_shift(count, bit + 1), bit + 1)

        @pl.when(jnp.bitwise_and(jnp.right_shift(count, bit), 1) == 1)
        def _():
            copy = pltpu.make_async_copy(
                src_ref.at[pl.ds(pl.multiple_of((src_row + off) * sub, sub), size * sub)],
                dst_ref.at[pl.ds(pl.multiple_of((dst_row + off) * sub, sub), size * sub)], sem)
            copy.wait() if wait else copy.start()


def _dispatch_kernel(pos_ref, cnt_ref, glob_ref, loc_ref, pad_row_ref, pad_len_ref, h_ref, xb_ref,
                     sbuf, zbuf, sems, *, sub):
    i = pl.program_id(0)
    last = pl.num_programs(0) - 1
    tm = h_ref.shape[0] // sub
    ta = tm * TOP_K
    slot = lax.rem(i, 2)
    g = zbuf.shape[0] // sub
    blocks = xb_ref.shape[0] // (g * sub)

    def zero_fill(wait):
        for e in range(N_EXPERTS):
            _start_run_copies(zbuf, xb_ref, 0, pad_row_ref[e], pad_len_ref[e], sub, sems.at[2], wait)
            unused = pad_len_ref[N_EXPERTS] + e

            @pl.when(unused < blocks)
            def _():
                copy = pltpu.make_async_copy(
                    zbuf, xb_ref.at[pl.ds(pl.multiple_of(unused * (g * sub), g * sub), g * sub)], sems.at[2])
                copy.wait() if wait else copy.start()

    @pl.when(i == 0)
    def _():
        zbuf[...] = jnp.zeros_like(zbuf)
        zero_fill(False)

    def wait_slot(s):
        pltpu.make_async_copy(sbuf.at[s], xb_ref.at[pl.ds(0, ta * sub)], sems.at[s]).wait()

    @pl.when(i >= 2)
    def _():
        wait_slot(slot)

    def place(t, carry):
        v = _token_tile(h_ref, t, sub)[...]
        for k in range(TOP_K):
            _token_tile(sbuf.at[slot], pos_ref[i * ta + k * tm + t], sub)[...] = v
        return carry

    lax.fori_loop(0, tm, place, 0, unroll=8)

    for e in range(N_EXPERTS):
        r = i * N_EXPERTS + e
        _start_run_copies(sbuf.at[slot], xb_ref, loc_ref[r], glob_ref[r], cnt_ref[r], sub, sems.at[slot])

    @pl.when(i == last)
    def _():
        @pl.when(i >= 1)
        def _():
            wait_slot(1 - slot)
        wait_slot(slot)
        zero_fill(True)


def _dispatch(h_tiles, layout, seq, d):
    sub = d // LANES
    n = h_tiles.shape[0] // sub
    tm = _tile(seq, MOE_TILE_TOKENS, 8)
    grid_spec = pltpu.PrefetchScalarGridSpec(
        num_scalar_prefetch=6,
        grid=(n // tm,),
        in_specs=[pl.BlockSpec((tm * sub, LANES), lambda i, *_: (i, 0))],
        out_specs=pl.BlockSpec(memory_space=pl.ANY),
        scratch_shapes=[pltpu.VMEM((2, tm * TOP_K * sub, LANES), h_tiles.dtype),
                        pltpu.VMEM((FFN_ROWS * sub, LANES), h_tiles.dtype),
                        pltpu.SemaphoreType.DMA((3,))],
    )
    return pl.pallas_call(
        functools.partial(_dispatch_kernel, sub=sub),
        grid_spec=grid_spec,
        out_shape=jax.ShapeDtypeStruct((layout["p_total"] * sub, LANES), h_tiles.dtype),
        compiler_params=_params("arbitrary"),
    )(layout["pos"], layout["cnt"], layout["glob"], layout["loc"], layout["pad_row"], layout["pad_len"], h_tiles)


def _combine_kernel(pos_ref, cnt_ref, glob_ref, loc_ref, x1_ref, rt_ref, mod_ref, fg_ref, yb_ref, o_ref,
                    ybuf, tbuf, sems, *, final):
    i = pl.program_id(0)
    tm, d = x1_ref.shape
    sub = d // LANES
    ta = tm * TOP_K
    slot = lax.rem(i, 2)

    def fetch(tile, s):
        for e in range(N_EXPERTS):
            r = tile * N_EXPERTS + e
            _start_run_copies(yb_ref, ybuf.at[s], glob_ref[r], loc_ref[r], cnt_ref[r], sub, sems.at[s])

    @pl.when(i == 0)
    def _():
        fetch(0, 0)

    @pl.when(i + 1 < pl.num_programs(0))
    def _():
        fetch(i + 1, 1 - slot)

    pltpu.make_async_copy(yb_ref.at[pl.ds(0, ta * sub)], ybuf.at[slot], sems.at[slot]).wait()

    def unpermute(t, carry):
        for k in range(TOP_K):
            _token_tile(tbuf.at[k], t, sub)[...] = _token_tile(ybuf.at[slot], pos_ref[i * ta + k * tm + t], sub)[...]
        return carry

    lax.fori_loop(0, tm, unpermute, 0, unroll=8)

    rt = rt_ref[...]
    y = (rt[:, 2:3] * _load_token_tiles(tbuf.at[0], tm, d) + rt[:, 3:4] * _load_token_tiles(tbuf.at[1], tm, d))
    x2 = x1_ref[...] + mod_ref[0][5:6] * y
    o_ref[...] = _rms(x2) * fg_ref[...] if final else x2


def _combine(x1, yb, layout, route, mod, final_g, seq, final):
    n, d = x1.shape
    sub = d // LANES
    tm = _tile(seq, MOE_TILE_TOKENS, 8)
    per_seq = seq // tm
    grid_spec = pltpu.PrefetchScalarGridSpec(
        num_scalar_prefetch=4,
        grid=(n // tm,),
        in_specs=[
            pl.BlockSpec((tm, d), lambda i, *_: (i, 0)),
            pl.BlockSpec((tm, LANES), lambda i, *_: (i, 0)),
            pl.BlockSpec((1, 6, d), lambda i, *_: (i // per_seq, 0, 0)),
            pl.BlockSpec((1, d), lambda i, *_: (0, 0)),
            pl.BlockSpec(memory_space=pl.ANY),
        ],
        out_specs=pl.BlockSpec((tm, d), lambda i, *_: (i, 0)),
        scratch_shapes=[pltpu.VMEM((2, TOP_K * tm * sub, LANES), F32),
                        pltpu.VMEM((TOP_K, tm * sub, LANES), F32),
                        pltpu.SemaphoreType.DMA((2,))],
    )
    return pl.pallas_call(
        functools.partial(_combine_kernel, final=final),
        grid_spec=grid_spec,
        out_shape=jax.ShapeDtypeStruct((n, d), F32),
        compiler_params=_params("arbitrary"),
    )(layout["pos"], layout["cnt"], layout["glob"], layout["loc"], x1, route, mod, final_g.reshape(1, d), yb)


def _moe_layout(route_t, n, seq):
    g = FFN_ROWS
    a = n * TOP_K
    ta = _tile(seq, MOE_TILE_TOKENS, 8) * TOP_K
    tiles = a // ta
    assert ta < (1 << MOE_RUN_BITS)
    e_flat = route_t[:TOP_K].astype(jnp.int32).reshape(TOP_K, tiles, ta // TOP_K).transpose(1, 0, 2).reshape(-1)
    onehot = (e_flat[None, :] == jnp.arange(N_EXPERTS, dtype=jnp.int32)[:, None]).astype(jnp.int32)
    csum = jnp.cumsum(onehot, axis=1)
    rank = jnp.sum(csum * onehot, axis=0) - 1
    counts = csum[:, -1]
    padded = (counts + g - 1) // g * g
    pad_ends = jnp.cumsum(padded)
    pad_starts = pad_ends - padded
    dest = jnp.sum(pad_starts[:, None] * onehot, axis=0) + rank
    tile_end = csum[:, ta - 1::ta]
    cnt = tile_end - jnp.concatenate([jnp.zeros((N_EXPERTS, 1), jnp.int32), tile_end[:, :-1]], axis=1)
    glob = pad_starts[:, None] + tile_end - cnt
    loc = jnp.cumsum(cnt, axis=0) - cnt
    per_assignment = lambda v: jnp.sum(
        jnp.broadcast_to(v[:, :, None], (N_EXPERTS, tiles, ta)).reshape(N_EXPERTS, a) * onehot, axis=0)
    pos = per_assignment(loc - glob) + dest
    p_total = (-(-a // g) + N_EXPERTS) * g
    blk_start = jnp.arange(p_total // g, dtype=jnp.int32) * g
    blk_e = jnp.minimum(jnp.sum((pad_ends[:, None] <= blk_start[None, :]).astype(jnp.int32), axis=0),
                        N_EXPERTS - 1)
    i32 = lambda v: v.astype(jnp.int32).reshape(-1)
    n_used = pad_ends[-1:] // g
    cnt, glob, loc = cnt.T, glob.T, loc.T
    return dict(pos=i32(pos), cnt=i32(cnt), glob=i32(glob), loc=i32(loc), blk_e=i32(blk_e), n_used=i32(n_used),
                pad_row=i32(pad_starts + counts), pad_len=i32(jnp.concatenate([padded - counts, n_used])),
                p_total=p_total)


def _moe_ffn(h_tiles, x1, route, route_t, mod, wg, wu, wd, final_g, seq, final):
    n, d = x1.shape
    layout = _moe_layout(route_t, n, seq)
    xb = _dispatch(h_tiles, layout, seq, d)
    yb = _moe_grouped(xb, layout["blk_e"], layout["n_used"], wg, wu, wd)
    return _combine(x1, yb, layout, route, mod, final_g, seq, final)


def kernel(x, c, norm_mix_g, norm_ffn_g, ada_w, ada_b, w_in, w_out, att_out_g, ret_out_g, ffn_w_gate,
           ffn_w_up, ffn_w_down, router_w, moe_w_gate, moe_w_up, moe_w_down, final_norm_g):
    batch, seq, d = x.shape
    depth = ada_w.shape[0]
    assert seq % MOBA_BLOCK == 0 and seq % RET_BLOCK == 0
    assert w_in.shape[2] == N_GROUPS * GROUP_WIDTH
    mods = _adaln(c, ada_w, ada_b).reshape(depth, batch, 6, d)
    tables = _rotary_tables(seq)
    xf = x.reshape(batch * seq, d)
    flat = lambda w: w.reshape(-1, w.shape[-1])
    moe_bf16 = {}
    for l in range(depth):
        mod = mods[l]
        final = l == depth - 1
        proj = _inproj(xf, mod, norm_mix_g[l], w_in, l, tables, seq)
        o_r = _retention(proj, ret_out_g[l], batch, seq)
        if l % 2 == 0:
            nxt = (l + 1) // 2
            ride = l + 1 < depth
            riders = [ffn_w_gate[l // 2], ffn_w_up[l // 2], ffn_w_down[l // 2]]
            o_a, wg, wu, wd, *rest = _moba(proj, batch, seq, riders + ([flat(moe_w_gate[nxt])] if ride else []))
            if ride:
                moe_bf16[nxt, "gate"] = rest[0].reshape(moe_w_gate[nxt].shape)
            xf = _outproj_ffn(o_a, o_r, xf, w_out, l, att_out_g[l], mod, norm_ffn_g[l], wg, wu, wd,
                              final_norm_g, seq, final, rider=flat(moe_w_down[nxt]) if ride else None)
            if ride:
                xf, down_bf16 = xf
                moe_bf16[nxt, "down"] = down_bf16.reshape(moe_w_down[nxt].shape)
        else:
            e = l // 2
            o_a, up_bf16 = _moba(proj, batch, seq, [flat(moe_w_up[e])])
            x1, h, route, route_t = _outproj_route(o_a, o_r, xf, w_out, l, att_out_g[l], mod, norm_ffn_g[l],
                                                   seq, router_w[e])
            xf = _moe_ffn(h, x1, route, route_t, mod, moe_bf16[e, "gate"], up_bf16.reshape(moe_w_up[e].shape),
                          moe_bf16[e, "down"], final_norm_g, seq, final)
    return xf.reshape(batch, seq, d)
```

```python
import functools

import jax
import jax.numpy as jnp
from jax import lax
from jax.experimental import pallas as pl
from jax.experimental.pallas import tpu as pltpu

F32 = jnp.float32
BF16 = jnp.bfloat16

LANES = 128
F32_SUBLANES = 8
BF16_SUBLANES = 16
V7X_MXU_WIDTH = 256
V7X_VMEM_LIMIT_BYTES = 56 * 1024 * 1024

ADALN_COLS = 1536
INPROJ_ROWS = 1024
ROUTE_ROWS = 512

ATT_HEADS = 8
ATT_HEAD_DIM = 64
ATT_WIDTH = ATT_HEADS * ATT_HEAD_DIM
MOBA_BLOCK = 256
MOBA_TOPK = 3
MOBA_PAIRS_PER_STEP = 2
MOBA_RANK_ROWS = F32_SUBLANES
MOBA_GATE_ROWS = BF16_SUBLANES
MASK_VALUE = -1e30
MOBA_Q_SCALE = ATT_HEAD_DIM ** -0.5 * 1.4426950408889634
RET_HEADS = 4
RET_HEAD_DIM = 128
RET_WIDTH = RET_HEADS * RET_HEAD_DIM
RET_BLOCK = 256
ROPE_BASE = 10000.0
N_EXPERTS = 8
TOP_K = 2
EPS = 1e-6
GROUP_WIDTH = 512
N_GROUPS = 7
MOE_TILE_TOKENS = 512
MOE_RUN_BITS = 11
FFN_ROWS = 512
ROUTE_T_ROWS = F32_SUBLANES


def _tile(total, target, mult):
    best = None
    t = mult
    while t <= min(total, target):
        if total % t == 0:
            best = t
        t += mult
    return best if best is not None else total


def _params(*sem):
    return pltpu.CompilerParams(dimension_semantics=sem, vmem_limit_bytes=V7X_VMEM_LIMIT_BYTES)


def _rms(x):
    return x * lax.rsqrt(jnp.mean(x * x, axis=-1, keepdims=True) + EPS)


def _silu(x):
    return x * jax.nn.sigmoid(x)


def _store_token_tiles(ref, x, row0=0):
    rows, d = x.shape
    sub = d // LANES
    for s in range(sub):
        ref[pl.ds(row0 * sub + s, rows, stride=sub), :] = x[:, s * LANES:(s + 1) * LANES].astype(ref.dtype)


def _load_token_tiles(ref, rows, d, row0=0):
    sub = d // LANES
    return jnp.concatenate([ref[pl.ds(row0 * sub + s, rows, stride=sub), :] for s in range(sub)], axis=1)


def _token_tile(ref, t, sub, scaled=False):
    return ref.at[pl.ds(pl.multiple_of(t if scaled else t * sub, sub), sub)]


def _adaln_kernel(c_ref, w_ref, b_ref, o_ref):
    cs = _silu(c_ref[...]).astype(BF16)
    o_ref[0] = jnp.dot(cs, w_ref[0].astype(BF16), preferred_element_type=F32) + b_ref[0]


def _adaln(c, ada_w, ada_b):
    depth, d, w = ada_w.shape
    b = c.shape[0]
    tn = _tile(w, ADALN_COLS, LANES)
    return pl.pallas_call(
        _adaln_kernel,
        grid=(depth, w // tn),
        in_specs=[
            pl.BlockSpec((b, d), lambda l, j: (0, 0)),
            pl.BlockSpec((1, d, tn), lambda l, j: (l, 0, j)),
            pl.BlockSpec((1, 1, tn), lambda l, j: (l, 0, j)),
        ],
        out_specs=pl.BlockSpec((1, b, tn), lambda l, j: (l, 0, j)),
        out_shape=jax.ShapeDtypeStruct((depth, b, w), F32),
        compiler_params=_params("arbitrary", "arbitrary"),
    )(c, ada_w, ada_b.reshape(depth, 1, w))


def _inproj_kernel(x_ref, mod_ref, g_ref, w_ref, cq_ref, sq_ref, ck_ref, sk_ref, o_ref):
    m = mod_ref[0]
    h = (_rms(x_ref[...]) * g_ref[...] * (1.0 + m[1:2]) + m[0:1]).astype(BF16)
    rotary = {3: (cq_ref, sq_ref), 4: (ck_ref, sk_ref)}
    for j in range(N_GROUPS):
        c0 = j * GROUP_WIDTH
        acc = jnp.dot(h, w_ref[:, c0:c0 + GROUP_WIDTH].astype(BF16), preferred_element_type=F32)
        if j in rotary:
            cos = rotary[j][0][...]
            sin = rotary[j][1][...]
            for hd in range(RET_HEADS):
                a = acc[:, hd * RET_HEAD_DIM:(hd + 1) * RET_HEAD_DIM]
                r = a * cos + pltpu.roll(a, RET_HEAD_DIM // 2, 1) * sin
                o_ref[:, c0 + hd * RET_HEAD_DIM:c0 + (hd + 1) * RET_HEAD_DIM] = r.astype(o_ref.dtype)
        elif j == 0:
            o_ref[:, c0:c0 + GROUP_WIDTH] = (acc * MOBA_Q_SCALE).astype(o_ref.dtype)
        else:
            o_ref[:, c0:c0 + GROUP_WIDTH] = acc.astype(o_ref.dtype)


def _rotary_tables(s):
    half = RET_HEAD_DIM // 2
    inv_freq = ROPE_BASE ** (-jnp.arange(half, dtype=F32) / half)
    ang = jnp.arange(s, dtype=F32)[:, None] * inv_freq[None, :]
    cos = jnp.cos(ang)
    sin = jnp.sin(ang)
    cos2 = jnp.concatenate([cos, cos], axis=-1)
    sin2 = jnp.concatenate([-sin, sin], axis=-1)
    k_scale = RET_HEAD_DIM ** -0.5
    return cos2, sin2, cos2 * k_scale, sin2 * k_scale


def _inproj(x2d, mod, g, w_all, layer, tables, seq):
    n, d = x2d.shape
    tm = _tile(seq, INPROJ_ROWS, BF16_SUBLANES)
    per_seq = seq // tm
    width = N_GROUPS * GROUP_WIDTH
    return pl.pallas_call(
        _inproj_kernel,
        grid=(n // tm,),
        in_specs=[
            pl.BlockSpec((tm, d), lambda i: (i, 0)),
            pl.BlockSpec((1, 6, d), lambda i: (i // per_seq, 0, 0)),
            pl.BlockSpec((1, d), lambda i: (0, 0)),
            pl.BlockSpec((None, d, width), lambda i: (layer, 0, 0)),
        ] + [pl.BlockSpec((tm, RET_HEAD_DIM), lambda i: (i % per_seq, 0))] * 4,
        out_specs=pl.BlockSpec((tm, width), lambda i: (i, 0)),
        out_shape=jax.ShapeDtypeStruct((n, width), BF16),
        compiler_params=_params("arbitrary"),
    )(x2d, mod, g.reshape(1, d), w_all, *tables)


def _cast_rider_specs(rider, steps, index):
    rows, cols = rider.shape
    assert rows % steps == 0 and (rows // steps) % BF16_SUBLANES == 0
    spec = pl.BlockSpec((rows // steps, cols), index)
    return spec, spec, jax.ShapeDtypeStruct(rider.shape, BF16)


def _moba_kernel(q_ref, k_ref, v_ref, *rest):
    riders = (len(rest) - 1) // 2
    o_ref = rest[riders]
    for w_ref, w_bf16_ref in zip(rest[:riders], rest[riders + 1:]):
        w_bf16_ref[...] = w_ref[...].astype(BF16)
    s = q_ref.shape[0]
    blk = MOBA_BLOCK
    nb = s // blk
    assert nb <= MOBA_RANK_ROWS
    n_sel = min(MOBA_TOPK, nb - 1)
    shift = blk.bit_length() - 1
    contract_lanes = (((1,), (1,)), ((), ()))
    lane = lax.broadcasted_iota(jnp.int32, (1, LANES), 1)
    head_of_lane = jnp.right_shift(lane, ATT_HEAD_DIM.bit_length() - 1)

    key_pos = lax.broadcasted_iota(jnp.int32, (s, LANES), 0)
    key_lane = lax.broadcasted_iota(jnp.int32, (s, LANES), 1)
    key_block_onehot = jnp.where(jnp.right_shift(key_pos, shift) == key_lane, 1.0, 0.0).astype(BF16)
    ones = jnp.ones((s, LANES), BF16)

    blk_id = lax.broadcasted_iota(jnp.int32, (MOBA_RANK_ROWS, s), 0)
    q_blk = jnp.right_shift(lax.broadcasted_iota(jnp.int32, (MOBA_RANK_ROWS, s), 1), shift)
    past = blk_id < q_blk
    block_mean = jnp.where(
        jnp.right_shift(lax.broadcasted_iota(jnp.int32, (MOBA_GATE_ROWS, s), 1), shift)
        == lax.broadcasted_iota(jnp.int32, (MOBA_GATE_ROWS, s), 0), 1.0 / blk, 0.0).astype(BF16)
    no_bias = jnp.zeros((MOBA_GATE_ROWS - MOBA_RANK_ROWS, s), F32)
    eye = jnp.where(lax.broadcasted_iota(jnp.int32, (MOBA_GATE_ROWS, LANES), 0)
                    == lax.broadcasted_iota(jnp.int32, (MOBA_GATE_ROWS, LANES), 1), 1.0, 0.0).astype(BF16)
    row = lax.broadcasted_iota(jnp.int32, (2 * blk, blk), 0)
    col = lax.broadcasted_iota(jnp.int32, (2 * blk, blk), 1)
    causal_bias = jnp.where(col <= jnp.bitwise_and(row, blk - 1), 0.0, MASK_VALUE)

    def prepare(pair):
        cols = slice(pair * LANES, (pair + 1) * LANES)
        q_all = q_ref[:, cols]
        k_all = k_ref[:, cols]
        k_aug = jnp.concatenate([k_all, key_block_onehot], axis=1)
        v_aug = jnp.concatenate([v_ref[:, cols], ones], axis=1)
        k_mean = jnp.dot(block_mean, k_all, preferred_element_type=F32).astype(BF16)

        heads = []
        for hh in range(2):
            qh = jnp.where(head_of_lane == hh, q_all, jnp.zeros_like(q_all))
            gate = lax.dot_general(k_mean, qh, contract_lanes, preferred_element_type=F32)
            gm = jnp.where(past, gate[:MOBA_RANK_ROWS], MASK_VALUE)
            rank = jnp.zeros(gm.shape, jnp.int32)
            for jp in range(nb):
                cv = gm[jp:jp + 1, :]
                beats = jnp.logical_or(cv > gm, jnp.logical_and(cv == gm, jp < blk_id))
                rank = rank + jnp.where(beats, 1, 0)
            visible = jnp.logical_or(jnp.logical_and(rank < n_sel, past), blk_id >= q_blk)
            bias_t = jnp.concatenate([jnp.where(visible, 0.0, MASK_VALUE), no_bias], axis=0).astype(BF16)
            bias = lax.dot_general(bias_t, eye, (((0,), (0,)), ((), ())), preferred_element_type=F32)
            heads.append(jnp.concatenate([qh, bias.astype(BF16)], axis=1))
        return cols, heads, k_aug, v_aug

    def scores(i, pair):
        _, heads, k_aug, _ = pair
        rows = slice(i * blk, (i + 1) * blk)
        q_both = jnp.concatenate([heads[0][rows], heads[1][rows]], axis=0)
        return lax.dot_general(q_both, k_aug[:(i + 1) * blk], contract_lanes, preferred_element_type=F32)

    def softmax_pv(i, pair, sc):
        cols, _, _, v_aug = pair
        sc_own = sc[:, i * blk:] + causal_bias
        chunks = [sc_own[:, c:c + LANES] for c in range(0, blk, LANES)]
        chunks += [sc[:, c:c + LANES] for c in range(0, i * blk, LANES)]
        m = jnp.max(functools.reduce(jnp.maximum, chunks), axis=-1, keepdims=True)
        if i > 0:
            p = jnp.concatenate([jnp.exp2(sc[:, :i * blk] - m), jnp.exp2(sc_own - m)], axis=1)
        else:
            p = jnp.exp2(sc_own - m)
        o = jnp.dot(p.astype(BF16), v_aug[:(i + 1) * blk], preferred_element_type=F32)
        o = o[:, :LANES] / o[:, LANES:]
        o_ref[i * blk:(i + 1) * blk, cols] = jnp.where(head_of_lane == 0, o[:blk], o[blk:]).astype(o_ref.dtype)

    pairs = {}
    for i in reversed(range(nb)):
        for idx in range(q_ref.shape[1] // LANES):
            if idx not in pairs:
                pairs[idx] = prepare(idx)
            softmax_pv(i, pairs[idx], scores(i, pairs[idx]))


def _moba(proj, batch, seq, riders=()):
    n = proj.shape[0]
    width = MOBA_PAIRS_PER_STEP * LANES
    steps = ATT_WIDTH // width
    spec = lambda off: pl.BlockSpec((seq, width), lambda b, p: (b, off + p))
    in_specs = [spec(0), spec(steps), spec(2 * steps)]
    out_specs = [pl.BlockSpec((seq, width), lambda b, p: (b, p))]
    out_shape = [jax.ShapeDtypeStruct((n, ATT_WIDTH), BF16)]
    args = [proj, proj, proj]
    for rider in riders:
        r_in, r_out, r_shape = _cast_rider_specs(rider, batch * steps, lambda b, p: (b * steps + p, 0))
        in_specs.append(r_in)
        out_specs.append(r_out)
        out_shape.append(r_shape)
        args.append(rider)
    return pl.pallas_call(
        _moba_kernel,
        grid=(batch, steps),
        in_specs=in_specs,
        out_specs=out_specs,
        out_shape=out_shape,
        compiler_params=_params("arbitrary", "arbitrary"),
    )(*args)


def _retention_kernel(q_ref, k_ref, v_ref, g_ref, dm_ref, qd_ref, kd_ref, cd_ref, og_ref, o_ref):
    s = q_ref.shape[0]
    c = RET_BLOCK
    states = [jnp.zeros((RET_HEAD_DIM, RET_HEAD_DIM), F32) for _ in range(RET_HEADS)]
    for n in range(s // c):
        rows = slice(n * c, (n + 1) * c)
        for hd in range(RET_HEADS):
            cols = slice(hd * RET_HEAD_DIM, (hd + 1) * RET_HEAD_DIM)
            qc = q_ref[rows, cols]
            kc = k_ref[rows, cols]
            vc = v_ref[rows, cols]
            scores = lax.dot_general(qc, kc, (((1,), (1,)), ((), ())), preferred_element_type=F32) * dm_ref[hd]
            inner = jnp.dot(scores.astype(BF16), vc, preferred_element_type=F32)
            cross = jnp.dot(qc, states[hd].astype(BF16), preferred_element_type=F32) * qd_ref[hd]
            o = _rms(inner + cross) * og_ref[hd]
            gate = g_ref[rows, cols].astype(F32)
            o_ref[rows, cols] = (_silu(gate) * o).astype(o_ref.dtype)
            k_dec = (kc.astype(F32) * kd_ref[hd]).astype(BF16)
            kv = lax.dot_general(k_dec, vc, (((0,), (0,)), ((), ())), preferred_element_type=F32)
            states[hd] = states[hd] * cd_ref[hd] + kv


def _retention_tables():
    h, c = RET_HEADS, RET_BLOCK
    lg = jnp.log(1.0 - 2.0 ** (-5.0 - jnp.arange(h, dtype=F32)))
    idx = jnp.arange(c, dtype=F32)
    diff = idx[:, None] - idx[None, :]
    dmask = jnp.where(diff >= 0, jnp.exp(jnp.maximum(diff, 0.0)[None] * lg[:, None, None]), 0.0)
    rows = lambda v: jnp.broadcast_to(v[:, :, None], (h, c, RET_HEAD_DIM))
    qdec = rows(jnp.exp((idx + 1.0)[None, :] * lg[:, None]))
    kdec = rows(jnp.exp((c - 1 - idx)[None, :] * lg[:, None]))
    cdec = jnp.broadcast_to(jnp.exp(c * lg)[:, None, None], (h, RET_HEAD_DIM, RET_HEAD_DIM))
    return dmask, qdec, kdec, cdec


def _retention(proj, ret_out_g, batch, seq):
    n = proj.shape[0]
    base = 3 * ATT_WIDTH // RET_WIDTH
    spec = lambda off: pl.BlockSpec((seq, RET_WIDTH), lambda b: (b, base + off))
    tables = _retention_tables()
    whole = lambda t: pl.BlockSpec(t.shape, lambda b: (0, 0, 0))
    return pl.pallas_call(
        _retention_kernel,
        grid=(batch,),
        in_specs=[spec(0), spec(1), spec(2), spec(3)] + [whole(t) for t in tables]
                 + [pl.BlockSpec((RET_HEADS, 1, RET_HEAD_DIM), lambda b: (0, 0, 0))],
        out_specs=pl.BlockSpec((seq, RET_WIDTH), lambda b: (b, 0)),
        out_shape=jax.ShapeDtypeStruct((n, RET_WIDTH), BF16),
        compiler_params=_params("arbitrary"),
    )(proj, proj, proj, proj, *tables, ret_out_g.reshape(RET_HEADS, 1, RET_HEAD_DIM))


def _top2_route(logits):
    lane = lax.broadcasted_iota(jnp.int32, logits.shape, 1)
    lg = jnp.where(lane < N_EXPERTS, logits, -jnp.inf)
    m1 = jnp.max(lg, axis=-1, keepdims=True)
    i1 = jnp.min(jnp.where(lg == m1, lane, LANES), axis=-1, keepdims=True)
    lg2 = jnp.where(lane == i1, -jnp.inf, lg)
    m2 = jnp.max(lg2, axis=-1, keepdims=True)
    i2 = jnp.min(jnp.where(lg2 == m2, lane, LANES), axis=-1, keepdims=True)
    e = jnp.exp(m2 - m1)
    w1 = 1.0 / (1.0 + e)
    w2 = e / (1.0 + e)
    out = jnp.where(lane == 0, i1.astype(F32), 0.0)
    out = jnp.where(lane == 1, i2.astype(F32), out)
    out = jnp.where(lane == 2, w1, out)
    return jnp.where(lane == 3, w2, out)


def _mix_and_norm(oa_ref, or_ref, x_ref, w_ref, ag_ref, m, ng_ref):
    oa = (_rms(oa_ref[...].astype(F32)) * ag_ref[...]).astype(BF16)
    mix = (jnp.dot(oa, w_ref[:ATT_WIDTH, :].astype(BF16), preferred_element_type=F32)
           + jnp.dot(or_ref[...], w_ref[ATT_WIDTH:, :].astype(BF16), preferred_element_type=F32))
    x1 = x_ref[...] + m[2:3] * mix
    h = _rms(x1) * ng_ref[...] * (1.0 + m[4:5]) + m[3:4]
    return x1, h


def _mixer_specs(tm, d, per_seq, layer):
    row = lambda w: pl.BlockSpec((tm, w), lambda i: (i, 0))
    full = lambda a, b: pl.BlockSpec((a, b), lambda i: (0, 0))
    w_out = pl.BlockSpec((None, ATT_WIDTH + RET_WIDTH, d), lambda i: (layer, 0, 0))
    return [row(ATT_WIDTH), row(RET_WIDTH), row(d), w_out, full(1, ATT_WIDTH),
            pl.BlockSpec((1, 6, d), lambda i: (i // per_seq, 0, 0)), full(1, d)]


def _outproj_route_kernel(oa_ref, or_ref, x_ref, w_ref, ag_ref, mod_ref, ng_ref, rw_ref,
                          x1_ref, h_ref, rt_ref, rt_t_ref):
    x1, h = _mix_and_norm(oa_ref, or_ref, x_ref, w_ref, ag_ref, mod_ref[0], ng_ref)
    x1_ref[...] = x1
    _store_token_tiles(h_ref, h)
    route = _top2_route(jnp.dot(h.astype(BF16), rw_ref[...], preferred_element_type=F32))
    rt_ref[...] = route
    rt_t_ref[...] = route.T[:rt_t_ref.shape[0]]


def _outproj_route(o_a, o_r, x2d, w_all, layer, att_g, mod, norm_g, seq, router_w):
    n, d = x2d.shape
    tm = _tile(seq, ROUTE_ROWS, BF16_SUBLANES)
    sub = d // LANES
    rw = jnp.zeros((d, LANES), BF16).at[:, :N_EXPERTS].set(router_w.astype(BF16))
    return pl.pallas_call(
        _outproj_route_kernel,
        grid=(n // tm,),
        in_specs=_mixer_specs(tm, d, seq // tm, layer) + [pl.BlockSpec((d, LANES), lambda i: (0, 0))],
        out_specs=[pl.BlockSpec((tm, d), lambda i: (i, 0)),
                   pl.BlockSpec((tm * sub, LANES), lambda i: (i, 0)),
                   pl.BlockSpec((tm, LANES), lambda i: (i, 0)),
                   pl.BlockSpec((ROUTE_T_ROWS, tm), lambda i: (0, i))],
        out_shape=[jax.ShapeDtypeStruct((n, d), F32), jax.ShapeDtypeStruct((n * sub, LANES), F32),
                   jax.ShapeDtypeStruct((n, LANES), F32), jax.ShapeDtypeStruct((ROUTE_T_ROWS, n), F32)],
        compiler_params=_params("arbitrary"),
    )(o_a, o_r, x2d, w_all, att_g.reshape(1, ATT_WIDTH), mod, norm_g.reshape(1, d), rw)


def _swiglu(h, wg_ref, wu_ref, wd_ref, act_ref):
    ff = act_ref.shape[1]
    tf = _tile(ff, V7X_MXU_WIDTH, LANES)
    for c0 in range(0, ff, tf):
        g = jnp.dot(h, wg_ref[:, c0:c0 + tf], preferred_element_type=F32)
        u = jnp.dot(h, wu_ref[:, c0:c0 + tf], preferred_element_type=F32)
        act_ref[:, c0:c0 + tf] = (_silu(g) * u).astype(act_ref.dtype)
    return jnp.dot(act_ref[...], wd_ref[...], preferred_element_type=F32)


def _outproj_ffn_kernel(oa_ref, or_ref, x_ref, w_ref, ag_ref, mod_ref, ng_ref, wg_ref, wu_ref, wd_ref, fg_ref,
                        *rest, final):
    if len(rest) == 4:
        rider_ref, o_ref, rider_bf16_ref, act_ref = rest
        rider_bf16_ref[...] = rider_ref[...].astype(BF16)
    else:
        o_ref, act_ref = rest
    m = mod_ref[0]
    x1, h = _mix_and_norm(oa_ref, or_ref, x_ref, w_ref, ag_ref, m, ng_ref)
    y = _swiglu(h.astype(BF16), wg_ref, wu_ref, wd_ref, act_ref)
    x2 = x1 + m[5:6] * y
    o_ref[...] = _rms(x2) * fg_ref[...] if final else x2


def _outproj_ffn(o_a, o_r, x2d, w_all, layer, att_g, mod, norm_g, wg, wu, wd, final_g, seq, final, rider=None):
    n, d = x2d.shape
    ff = wg.shape[1]
    tm = _tile(seq, FFN_ROWS, BF16_SUBLANES)
    whole = lambda t: pl.BlockSpec(t.shape, lambda i: (0, 0))
    in_specs = _mixer_specs(tm, d, seq // tm, layer) + [whole(wg), whole(wu), whole(wd), pl.BlockSpec((1, d), lambda i: (0, 0))]
    out_specs = [pl.BlockSpec((tm, d), lambda i: (i, 0))]
    out_shape = [jax.ShapeDtypeStruct((n, d), F32)]
    args = [o_a, o_r, x2d, w_all, att_g.reshape(1, ATT_WIDTH), mod, norm_g.reshape(1, d), wg, wu, wd,
            final_g.reshape(1, d)]
    if rider is not None:
        r_in, r_out, r_shape = _cast_rider_specs(rider, n // tm, lambda i: (i, 0))
        in_specs.append(r_in)
        out_specs.append(r_out)
        out_shape.append(r_shape)
        args.append(rider)
    out = pl.pallas_call(
        functools.partial(_outproj_ffn_kernel, final=final),
        grid=(n // tm,),
        in_specs=in_specs,
        out_specs=out_specs,
        out_shape=out_shape,
        scratch_shapes=[pltpu.VMEM((tm, ff), BF16)],
        compiler_params=_params("arbitrary"),
    )(*args)
    return out if rider is not None else out[0]


def _moe_ffn_kernel(blk_e_ref, n_used_ref, x_ref, wg_ref, wu_ref, wd_ref, o_ref, xs_ref, act_ref):
    del blk_e_ref
    g, d = xs_ref.shape

    @pl.when(pl.program_id(0) < n_used_ref[0])
    def _():
        xs_ref[...] = _load_token_tiles(x_ref, g, d).astype(BF16)
        _store_token_tiles(o_ref, _swiglu(xs_ref[...], wg_ref.at[0], wu_ref.at[0], wd_ref.at[0], act_ref))

    @pl.when(pl.program_id(0) >= n_used_ref[0])
    def _():
        o_ref[...] = jnp.zeros_like(o_ref)


def _moe_grouped(xb, blk_e, n_used, wg, wu, wd):
    _, d, ff = wg.shape
    sub = d // LANES
    g = FFN_ROWS
    grid_spec = pltpu.PrefetchScalarGridSpec(
        num_scalar_prefetch=2,
        grid=(xb.shape[0] // (g * sub),),
        in_specs=[
            pl.BlockSpec((g * sub, LANES), lambda b, be, nu: (b, 0)),
            pl.BlockSpec((1, d, ff), lambda b, be, nu: (be[b], 0, 0)),
            pl.BlockSpec((1, d, ff), lambda b, be, nu: (be[b], 0, 0)),
            pl.BlockSpec((1, ff, d), lambda b, be, nu: (be[b], 0, 0)),
        ],
        out_specs=pl.BlockSpec((g * sub, LANES), lambda b, be, nu: (b, 0)),
        scratch_shapes=[pltpu.VMEM((g, d), BF16), pltpu.VMEM((g, ff), BF16)],
    )
    return pl.pallas_call(
        _moe_ffn_kernel,
        grid_spec=grid_spec,
        out_shape=jax.ShapeDtypeStruct(xb.shape, F32),
        compiler_params=_params("arbitrary"),
    )(blk_e, n_used, xb, wg, wu, wd)


def _start_run_copies(src_ref, dst_ref, src_row, dst_row, count, sub, sem, wait=False):
    for bit in range(MOE_RUN_BITS):
        size = 1 << bit
        off = jnp.left_shift(jnp.right_shift(count, bit + 1), bit + 1)

        @pl.when(jnp.bitwise_and(jnp.right_shift(count, bit), 1) == 1)
        def _():
            copy = pltpu.make_async_copy(
                src_ref.at[pl.ds(pl.multiple_of((src_row + off) * sub, sub), size * sub)],
                dst_ref.at[pl.ds(pl.multiple_of((dst_row + off) * sub, sub), size * sub)], sem)
            copy.wait() if wait else copy.start()


def _dispatch_kernel(pos_ref, cnt_ref, glob_ref, loc_ref, pad_row_ref, pad_len_ref, h_ref, xb_ref,
                     sbuf, zbuf, sems, *, sub):
    i = pl.program_id(0)
    last = pl.num_programs(0) - 1
    tm = h_ref.shape[0] // sub
    ta = tm * TOP_K
    slot = lax.rem(i, 2)
    g = zbuf.shape[0] // sub
    blocks = xb_ref.shape[0] // (g * sub)

    def zero_fill(wait):
        for e in range(N_EXPERTS):
            _start_run_copies(zbuf, xb_ref, 0, pad_row_ref[e], pad_len_ref[e], sub, sems.at[2], wait)
            unused = pad_len_ref[N_EXPERTS] + e

            @pl.when(unused < blocks)
            def _():
                copy = pltpu.make_async_copy(
                    zbuf, xb_ref.at[pl.ds(pl.multiple_of(unused * (g * sub), g * sub), g * sub)], sems.at[2])
                copy.wait() if wait else copy.start()

    @pl.when(i == 0)
    def _():
        zbuf[...] = jnp.zeros_like(zbuf)
        zero_fill(False)

    def wait_slot(s):
        pltpu.make_async_copy(sbuf.at[s], xb_ref.at[pl.ds(0, ta * sub)], sems.at[s]).wait()

    @pl.when(i >= 2)
    def _():
        wait_slot(slot)

    def place(t, carry):
        v = _token_tile(h_ref, t, sub)[...]
        for k in range(TOP_K):
            _token_tile(sbuf.at[slot], pos_ref[i * ta + k * tm + t], sub, scaled=True)[...] = v
        return carry

    lax.fori_loop(0, tm, place, 0, unroll=8)

    for e in range(N_EXPERTS):
        r = i * N_EXPERTS + e
        _start_run_copies(sbuf.at[slot], xb_ref, loc_ref[r], glob_ref[r], cnt_ref[r], sub, sems.at[slot])

    @pl.when(i == last)
    def _():
        @pl.when(i >= 1)
        def _():
            wait_slot(1 - slot)
        wait_slot(slot)
        zero_fill(True)


def _dispatch(h_tiles, layout, seq, d):
    sub = d // LANES
    n = h_tiles.shape[0] // sub
    tm = _tile(seq, MOE_TILE_TOKENS, F32_SUBLANES)
    grid_spec = pltpu.PrefetchScalarGridSpec(
        num_scalar_prefetch=6,
        grid=(n // tm,),
        in_specs=[pl.BlockSpec((tm * sub, LANES), lambda i, *_: (i, 0))],
        out_specs=pl.BlockSpec(memory_space=pl.ANY),
        scratch_shapes=[pltpu.VMEM((2, tm * TOP_K * sub, LANES), h_tiles.dtype),
                        pltpu.VMEM((FFN_ROWS * sub, LANES), h_tiles.dtype),
                        pltpu.SemaphoreType.DMA((3,))],
    )
    return pl.pallas_call(
        functools.partial(_dispatch_kernel, sub=sub),
        grid_spec=grid_spec,
        out_shape=jax.ShapeDtypeStruct((layout["p_total"] * sub, LANES), h_tiles.dtype),
        compiler_params=_params("arbitrary"),
    )(layout["pos"], layout["cnt"], layout["glob"], layout["loc"], layout["pad_row"], layout["pad_len"], h_tiles)


def _combine_kernel(pos_ref, cnt_ref, glob_ref, loc_ref, x1_ref, rt_ref, mod_ref, fg_ref, yb_ref, o_ref,
                    ybuf, tbuf, sems, *, final):
    i = pl.program_id(0)
    tm, d = x1_ref.shape
    sub = d // LANES
    ta = tm * TOP_K
    slot = lax.rem(i, 2)

    def fetch(tile, s):
        for e in range(N_EXPERTS):
            r = tile * N_EXPERTS + e
            _start_run_copies(yb_ref, ybuf.at[s], glob_ref[r], loc_ref[r], cnt_ref[r], sub, sems.at[s])

    @pl.when(i == 0)
    def _():
        fetch(0, 0)

    @pl.when(i + 1 < pl.num_programs(0))
    def _():
        fetch(i + 1, 1 - slot)

    pltpu.make_async_copy(yb_ref.at[pl.ds(0, ta * sub)], ybuf.at[slot], sems.at[slot]).wait()

    def unpermute(t, carry):
        for k in range(TOP_K):
            src = _token_tile(ybuf.at[slot], pos_ref[i * ta + k * tm + t], sub, scaled=True)
            _token_tile(tbuf.at[k], t, sub)[...] = src[...]
        return carry

    lax.fori_loop(0, tm, unpermute, 0, unroll=8)

    rt = rt_ref[...]
    y = (rt[:, 2:3] * _load_token_tiles(tbuf.at[0], tm, d) + rt[:, 3:4] * _load_token_tiles(tbuf.at[1], tm, d))
    x2 = x1_ref[...] + mod_ref[0][5:6] * y
    o_ref[...] = _rms(x2) * fg_ref[...] if final else x2


def _combine(x1, yb, layout, route, mod, final_g, seq, final):
    n, d = x1.shape
    sub = d // LANES
    tm = _tile(seq, MOE_TILE_TOKENS, F32_SUBLANES)
    per_seq = seq // tm
    grid_spec = pltpu.PrefetchScalarGridSpec(
        num_scalar_prefetch=4,
        grid=(n // tm,),
        in_specs=[
            pl.BlockSpec((tm, d), lambda i, *_: (i, 0)),
            pl.BlockSpec((tm, LANES), lambda i, *_: (i, 0)),
            pl.BlockSpec((1, 6, d), lambda i, *_: (i // per_seq, 0, 0)),
            pl.BlockSpec((1, d), lambda i, *_: (0, 0)),
            pl.BlockSpec(memory_space=pl.ANY),
        ],
        out_specs=pl.BlockSpec((tm, d), lambda i, *_: (i, 0)),
        scratch_shapes=[pltpu.VMEM((2, TOP_K * tm * sub, LANES), F32),
                        pltpu.VMEM((TOP_K, tm * sub, LANES), F32),
                        pltpu.SemaphoreType.DMA((2,))],
    )
    return pl.pallas_call(
        functools.partial(_combine_kernel, final=final),
        grid_spec=grid_spec,
        out_shape=jax.ShapeDtypeStruct((n, d), F32),
        compiler_params=_params("arbitrary"),
    )(layout["pos"], layout["cnt"], layout["glob"], layout["loc"], x1, route, mod, final_g.reshape(1, d), yb)


def _moe_layout(route_t, n, seq, sub):
    g = FFN_ROWS
    a = n * TOP_K
    ta = _tile(seq, MOE_TILE_TOKENS, F32_SUBLANES) * TOP_K
    tiles = a // ta
    assert ta < (1 << MOE_RUN_BITS)
    e_flat = route_t[:TOP_K].astype(jnp.int32).reshape(TOP_K, tiles, ta // TOP_K).transpose(1, 0, 2).reshape(-1)
    onehot = (e_flat[None, :] == jnp.arange(N_EXPERTS, dtype=jnp.int32)[:, None]).astype(jnp.int32)
    csum = jnp.cumsum(onehot, axis=1)
    rank = jnp.sum(csum * onehot, axis=0) - 1
    counts = csum[:, -1]
    padded = (counts + g - 1) // g * g
    pad_ends = jnp.cumsum(padded)
    pad_starts = pad_ends - padded
    dest = jnp.sum(pad_starts[:, None] * onehot, axis=0) + rank
    tile_end = csum[:, ta - 1::ta]
    cnt = tile_end - jnp.concatenate([jnp.zeros((N_EXPERTS, 1), jnp.int32), tile_end[:, :-1]], axis=1)
    glob = pad_starts[:, None] + tile_end - cnt
    loc = jnp.cumsum(cnt, axis=0) - cnt
    per_assignment = lambda v: jnp.sum(
        jnp.broadcast_to(v[:, :, None], (N_EXPERTS, tiles, ta)).reshape(N_EXPERTS, a) * onehot, axis=0)
    pos = per_assignment(loc - glob) + dest
    p_total = (-(-a // g) + N_EXPERTS) * g
    blk_start = jnp.arange(p_total // g, dtype=jnp.int32) * g
    blk_e = jnp.minimum(jnp.sum((pad_ends[:, None] <= blk_start[None, :]).astype(jnp.int32), axis=0),
                        N_EXPERTS - 1)
    i32 = lambda v: v.astype(jnp.int32).reshape(-1)
    n_used = pad_ends[-1:] // g
    cnt, glob, loc = cnt.T, glob.T, loc.T
    return dict(pos=i32(pos * sub), cnt=i32(cnt), glob=i32(glob), loc=i32(loc), blk_e=i32(blk_e), n_used=i32(n_used),
                pad_row=i32(pad_starts + counts), pad_len=i32(jnp.concatenate([padded - counts, n_used])),
                p_total=p_total)


def _moe_ffn(h_tiles, x1, route, route_t, mod, wg, wu, wd, final_g, seq, final):
    n, d = x1.shape
    layout = _moe_layout(route_t, n, seq, d // LANES)
    xb = _dispatch(h_tiles, layout, seq, d)
    yb = _moe_grouped(xb, layout["blk_e"], layout["n_used"], wg, wu, wd)
    return _combine(x1, yb, layout, route, mod, final_g, seq, final)


def kernel(x, c, norm_mix_g, norm_ffn_g, ada_w, ada_b, w_in, w_out, att_out_g, ret_out_g, ffn_w_gate,
           ffn_w_up, ffn_w_down, router_w, moe_w_gate, moe_w_up, moe_w_down, final_norm_g):
    batch, seq, d = x.shape
    depth = ada_w.shape[0]
    assert seq % MOBA_BLOCK == 0 and seq % RET_BLOCK == 0
    assert w_in.shape[2] == N_GROUPS * GROUP_WIDTH
    mods = _adaln(c, ada_w, ada_b).reshape(depth, batch, 6, d)
    tables = _rotary_tables(seq)
    xf = x.reshape(batch * seq, d)
    flat = lambda w: w.reshape(-1, w.shape[-1])
    moe_bf16 = {}
    for l in range(depth):
        mod = mods[l]
        final = l == depth - 1
        proj = _inproj(xf, mod, norm_mix_g[l], w_in, l, tables, seq)
        o_r = _retention(proj, ret_out_g[l], batch, seq)
        if l % 2 == 0:
            nxt = (l + 1) // 2
            ride = l + 1 < depth
            riders = [ffn_w_gate[l // 2], ffn_w_up[l // 2], ffn_w_down[l // 2]]
            o_a, wg, wu, wd, *rest = _moba(proj, batch, seq, riders + ([flat(moe_w_gate[nxt])] if ride else []))
            if ride:
                moe_bf16[nxt, "gate"] = rest[0].reshape(moe_w_gate[nxt].shape)
            xf = _outproj_ffn(o_a, o_r, xf, w_out, l, att_out_g[l], mod, norm_ffn_g[l], wg, wu, wd,
                              final_norm_g, seq, final, rider=flat(moe_w_down[nxt]) if ride else None)
            if ride:
                xf, down_bf16 = xf
                moe_bf16[nxt, "down"] = down_bf16.reshape(moe_w_down[nxt].shape)
        else:
            e = l // 2
            o_a, up_bf16 = _moba(proj, batch, seq, [flat(moe_w_up[e])])
            x1, h, route, route_t = _outproj_route(o_a, o_r, xf, w_out, l, att_out_g[l], mod, norm_ffn_g[l],
                                                   seq, router_w[e])
            xf = _moe_ffn(h, x1, route, route_t, mod, moe_bf16[e, "gate"], up_bf16.reshape(moe_w_up[e].shape),
                          moe_bf16[e, "down"], final_norm_g, seq, final)
    return xf.reshape(batch, seq, d)
```

```python
import functools

import jax
import jax.numpy as jnp
from jax import lax
from jax.experimental import pallas as pl
from jax.experimental.pallas import tpu as pltpu

F32 = jnp.float32
BF16 = jnp.bfloat16

LANES = 128
F32_SUBLANES = 8
BF16_SUBLANES = 16
V7X_MXU_WIDTH = 256
V7X_VMEM_LIMIT_BYTES = 56 * 1024 * 1024

ADALN_COLS = 1536
INPROJ_ROWS = 1024
ROUTE_ROWS = 512

ATT_HEADS = 8
ATT_HEAD_DIM = 64
ATT_WIDTH = ATT_HEADS * ATT_HEAD_DIM
MOBA_BLOCK = 256
MOBA_TOPK = 3
MOBA_PAIRS_PER_STEP = 2
MOBA_RANK_ROWS = F32_SUBLANES
MOBA_GATE_ROWS = BF16_SUBLANES
MASK_VALUE = -1e30
MOBA_Q_SCALE = ATT_HEAD_DIM ** -0.5 * 1.4426950408889634
RET_HEADS = 4
RET_HEAD_DIM = 128
RET_WIDTH = RET_HEADS * RET_HEAD_DIM
RET_BLOCK = 256
ROPE_BASE = 10000.0
N_EXPERTS = 8
TOP_K = 2
EPS = 1e-6
GROUP_WIDTH = 512
N_GROUPS = 7
MOE_TILE_TOKENS = 1024
MOE_RUN_BITS = 11
FFN_ROWS = 512
ROUTE_T_ROWS = F32_SUBLANES


def _tile(total, target, mult):
    best = None
    t = mult
    while t <= min(total, target):
        if total % t == 0:
            best = t
        t += mult
    return best if best is not None else total


def _params(*sem):
    return pltpu.CompilerParams(dimension_semantics=sem, vmem_limit_bytes=V7X_VMEM_LIMIT_BYTES)


def _rms(x):
    return x * lax.rsqrt(jnp.mean(x * x, axis=-1, keepdims=True) + EPS)


def _silu(x):
    return x * jax.nn.sigmoid(x)


def _store_token_tiles(ref, x, row0=0):
    rows, d = x.shape
    sub = d // LANES
    for s in range(sub):
        ref[pl.ds(row0 * sub + s, rows, stride=sub), :] = x[:, s * LANES:(s + 1) * LANES].astype(ref.dtype)


def _load_token_tiles(ref, rows, d, row0=0):
    sub = d // LANES
    return jnp.concatenate([ref[pl.ds(row0 * sub + s, rows, stride=sub), :] for s in range(sub)], axis=1)


def _token_tile(ref, t, sub, scaled=False):
    return ref.at[pl.ds(pl.multiple_of(t if scaled else t * sub, sub), sub)]


def _adaln_kernel(c_ref, w_ref, b_ref, o_ref):
    cs = _silu(c_ref[...]).astype(BF16)
    o_ref[0] = jnp.dot(cs, w_ref[0].astype(BF16), preferred_element_type=F32) + b_ref[0]


def _adaln(c, ada_w, ada_b):
    depth, d, w = ada_w.shape
    b = c.shape[0]
    tn = _tile(w, ADALN_COLS, LANES)
    return pl.pallas_call(
        _adaln_kernel,
        grid=(depth, w // tn),
        in_specs=[
            pl.BlockSpec((b, d), lambda l, j: (0, 0)),
            pl.BlockSpec((1, d, tn), lambda l, j: (l, 0, j)),
            pl.BlockSpec((1, 1, tn), lambda l, j: (l, 0, j)),
        ],
        out_specs=pl.BlockSpec((1, b, tn), lambda l, j: (l, 0, j)),
        out_shape=jax.ShapeDtypeStruct((depth, b, w), F32),
        compiler_params=_params("arbitrary", "arbitrary"),
    )(c, ada_w, ada_b.reshape(depth, 1, w))


def _inproj_kernel(x_ref, mod_ref, g_ref, w_ref, cq_ref, sq_ref, ck_ref, sk_ref, o_ref):
    m = mod_ref[0]
    h = (_rms(x_ref[...]) * g_ref[...] * (1.0 + m[1:2]) + m[0:1]).astype(BF16)
    rotary = {3: (cq_ref, sq_ref), 4: (ck_ref, sk_ref)}
    for j in range(N_GROUPS):
        c0 = j * GROUP_WIDTH
        acc = jnp.dot(h, w_ref[:, c0:c0 + GROUP_WIDTH].astype(BF16), preferred_element_type=F32)
        if j in rotary:
            cos = rotary[j][0][...]
            sin = rotary[j][1][...]
            for hd in range(RET_HEADS):
                a = acc[:, hd * RET_HEAD_DIM:(hd + 1) * RET_HEAD_DIM]
                r = a * cos + pltpu.roll(a, RET_HEAD_DIM // 2, 1) * sin
                o_ref[:, c0 + hd * RET_HEAD_DIM:c0 + (hd + 1) * RET_HEAD_DIM] = r.astype(o_ref.dtype)
        elif j == 0:
            o_ref[:, c0:c0 + GROUP_WIDTH] = (acc * MOBA_Q_SCALE).astype(o_ref.dtype)
        else:
            o_ref[:, c0:c0 + GROUP_WIDTH] = acc.astype(o_ref.dtype)


def _rotary_tables(s):
    half = RET_HEAD_DIM // 2
    inv_freq = ROPE_BASE ** (-jnp.arange(half, dtype=F32) / half)
    ang = jnp.arange(s, dtype=F32)[:, None] * inv_freq[None, :]
    cos = jnp.cos(ang)
    sin = jnp.sin(ang)
    cos2 = jnp.concatenate([cos, cos], axis=-1)
    sin2 = jnp.concatenate([-sin, sin], axis=-1)
    k_scale = RET_HEAD_DIM ** -0.5
    return cos2, sin2, cos2 * k_scale, sin2 * k_scale


def _inproj(x2d, mod, g, w_all, layer, tables, seq):
    n, d = x2d.shape
    tm = _tile(seq, INPROJ_ROWS, BF16_SUBLANES)
    per_seq = seq // tm
    width = N_GROUPS * GROUP_WIDTH
    return pl.pallas_call(
        _inproj_kernel,
        grid=(n // tm,),
        in_specs=[
            pl.BlockSpec((tm, d), lambda i: (i, 0)),
            pl.BlockSpec((1, 6, d), lambda i: (i // per_seq, 0, 0)),
            pl.BlockSpec((1, d), lambda i: (0, 0)),
            pl.BlockSpec((None, d, width), lambda i: (layer, 0, 0)),
        ] + [pl.BlockSpec((tm, RET_HEAD_DIM), lambda i: (i % per_seq, 0))] * 4,
        out_specs=pl.BlockSpec((tm, width), lambda i: (i, 0)),
        out_shape=jax.ShapeDtypeStruct((n, width), BF16),
        compiler_params=_params("arbitrary"),
    )(x2d, mod, g.reshape(1, d), w_all, *tables)


def _cast_rider_specs(rider, steps, index):
    rows, cols = rider.shape
    assert rows % steps == 0 and (rows // steps) % BF16_SUBLANES == 0
    spec = pl.BlockSpec((rows // steps, cols), index)
    return spec, spec, jax.ShapeDtypeStruct(rider.shape, BF16)


def _moba_kernel(q_ref, k_ref, v_ref, *rest):
    riders = (len(rest) - 1) // 2
    o_ref = rest[riders]
    for w_ref, w_bf16_ref in zip(rest[:riders], rest[riders + 1:]):
        w_bf16_ref[...] = w_ref[...].astype(BF16)
    s = q_ref.shape[0]
    blk = MOBA_BLOCK
    nb = s // blk
    assert nb <= MOBA_RANK_ROWS
    n_sel = min(MOBA_TOPK, nb - 1)
    shift = blk.bit_length() - 1
    contract_lanes = (((1,), (1,)), ((), ()))
    lane = lax.broadcasted_iota(jnp.int32, (1, LANES), 1)
    head_of_lane = jnp.right_shift(lane, ATT_HEAD_DIM.bit_length() - 1)

    key_pos = lax.broadcasted_iota(jnp.int32, (s, LANES), 0)
    key_lane = lax.broadcasted_iota(jnp.int32, (s, LANES), 1)
    key_block_onehot = jnp.where(jnp.right_shift(key_pos, shift) == key_lane, 1.0, 0.0).astype(BF16)
    ones = jnp.ones((s, LANES), BF16)

    blk_id = lax.broadcasted_iota(jnp.int32, (MOBA_RANK_ROWS, s), 0)
    q_blk = jnp.right_shift(lax.broadcasted_iota(jnp.int32, (MOBA_RANK_ROWS, s), 1), shift)
    past = blk_id < q_blk
    block_mean = jnp.where(
        jnp.right_shift(lax.broadcasted_iota(jnp.int32, (MOBA_GATE_ROWS, s), 1), shift)
        == lax.broadcasted_iota(jnp.int32, (MOBA_GATE_ROWS, s), 0), 1.0 / blk, 0.0).astype(BF16)
    no_bias = jnp.zeros((MOBA_GATE_ROWS - MOBA_RANK_ROWS, s), F32)
    eye = jnp.where(lax.broadcasted_iota(jnp.int32, (MOBA_GATE_ROWS, LANES), 0)
                    == lax.broadcasted_iota(jnp.int32, (MOBA_GATE_ROWS, LANES), 1), 1.0, 0.0).astype(BF16)
    row = lax.broadcasted_iota(jnp.int32, (2 * blk, blk), 0)
    col = lax.broadcasted_iota(jnp.int32, (2 * blk, blk), 1)
    causal_bias = jnp.where(col <= jnp.bitwise_and(row, blk - 1), 0.0, MASK_VALUE)

    def prepare(pair):
        cols = slice(pair * LANES, (pair + 1) * LANES)
        q_all = q_ref[:, cols]
        k_all = k_ref[:, cols]
        k_aug = jnp.concatenate([k_all, key_block_onehot], axis=1)
        v_aug = jnp.concatenate([v_ref[:, cols], ones], axis=1)
        k_mean = jnp.dot(block_mean, k_all, preferred_element_type=F32).astype(BF16)

        heads = []
        for hh in range(2):
            qh = jnp.where(head_of_lane == hh, q_all, jnp.zeros_like(q_all))
            gate = lax.dot_general(k_mean, qh, contract_lanes, preferred_element_type=F32)
            gm = jnp.where(past, gate[:MOBA_RANK_ROWS], MASK_VALUE)
            rank = jnp.zeros(gm.shape, jnp.int32)
            for jp in range(nb):
                cv = gm[jp:jp + 1, :]
                beats = jnp.logical_or(cv > gm, jnp.logical_and(cv == gm, jp < blk_id))
                rank = rank + jnp.where(beats, 1, 0)
            visible = jnp.logical_or(jnp.logical_and(rank < n_sel, past), blk_id >= q_blk)
            bias_t = jnp.concatenate([jnp.where(visible, 0.0, MASK_VALUE), no_bias], axis=0).astype(BF16)
            bias = lax.dot_general(bias_t, eye, (((0,), (0,)), ((), ())), preferred_element_type=F32)
            heads.append(jnp.concatenate([qh, bias.astype(BF16)], axis=1))
        return cols, heads, k_aug, v_aug

    def scores(i, pair):
        _, heads, k_aug, _ = pair
        rows = slice(i * blk, (i + 1) * blk)
        q_both = jnp.concatenate([heads[0][rows], heads[1][rows]], axis=0)
        return lax.dot_general(q_both, k_aug[:(i + 1) * blk], contract_lanes, preferred_element_type=F32)

    def softmax_pv(i, pair, sc):
        cols, _, _, v_aug = pair
        sc_own = sc[:, i * blk:] + causal_bias
        chunks = [sc_own[:, c:c + LANES] for c in range(0, blk, LANES)]
        chunks += [sc[:, c:c + LANES] for c in range(0, i * blk, LANES)]
        m = jnp.max(functools.reduce(jnp.maximum, chunks), axis=-1, keepdims=True)
        if i > 0:
            p = jnp.concatenate([jnp.exp2(sc[:, :i * blk] - m), jnp.exp2(sc_own - m)], axis=1)
        else:
            p = jnp.exp2(sc_own - m)
        o = jnp.dot(p.astype(BF16), v_aug[:(i + 1) * blk], preferred_element_type=F32)
        o = o[:, :LANES] / o[:, LANES:]
        o_ref[i * blk:(i + 1) * blk, cols] = jnp.where(head_of_lane == 0, o[:blk], o[blk:]).astype(o_ref.dtype)

    pairs = {}
    for i in reversed(range(nb)):
        for idx in range(q_ref.shape[1] // LANES):
            if idx not in pairs:
                pairs[idx] = prepare(idx)
            softmax_pv(i, pairs[idx], scores(i, pairs[idx]))


def _moba(proj, batch, seq, riders=()):
    n = proj.shape[0]
    width = MOBA_PAIRS_PER_STEP * LANES
    steps = ATT_WIDTH // width
    spec = lambda off: pl.BlockSpec((seq, width), lambda b, p: (b, off + p))
    in_specs = [spec(0), spec(steps), spec(2 * steps)]
    out_specs = [pl.BlockSpec((seq, width), lambda b, p: (b, p))]
    out_shape = [jax.ShapeDtypeStruct((n, ATT_WIDTH), BF16)]
    args = [proj, proj, proj]
    for rider in riders:
        r_in, r_out, r_shape = _cast_rider_specs(rider, batch * steps, lambda b, p: (b * steps + p, 0))
        in_specs.append(r_in)
        out_specs.append(r_out)
        out_shape.append(r_shape)
        args.append(rider)
    return pl.pallas_call(
        _moba_kernel,
        grid=(batch, steps),
        in_specs=in_specs,
        out_specs=out_specs,
        out_shape=out_shape,
        compiler_params=_params("arbitrary", "arbitrary"),
    )(*args)


def _retention_kernel(q_ref, k_ref, v_ref, g_ref, dm_ref, qd_ref, kd_ref, cd_ref, og_ref, o_ref):
    s = q_ref.shape[0]
    c = RET_BLOCK
    states = [jnp.zeros((RET_HEAD_DIM, RET_HEAD_DIM), F32) for _ in range(RET_HEADS)]
    for n in range(s // c):
        rows = slice(n * c, (n + 1) * c)
        for hd in range(RET_HEADS):
            cols = slice(hd * RET_HEAD_DIM, (hd + 1) * RET_HEAD_DIM)
            qc = q_ref[rows, cols]
            kc = k_ref[rows, cols]
            vc = v_ref[rows, cols]
            scores = lax.dot_general(qc, kc, (((1,), (1,)), ((), ())), preferred_element_type=F32) * dm_ref[hd]
            inner = jnp.dot(scores.astype(BF16), vc, preferred_element_type=F32)
            cross = jnp.dot(qc, states[hd].astype(BF16), preferred_element_type=F32) * qd_ref[hd]
            o = _rms(inner + cross) * og_ref[hd]
            gate = g_ref[rows, cols].astype(F32)
            o_ref[rows, cols] = (_silu(gate) * o).astype(o_ref.dtype)
            k_dec = (kc.astype(F32) * kd_ref[hd]).astype(BF16)
            kv = lax.dot_general(k_dec, vc, (((0,), (0,)), ((), ())), preferred_element_type=F32)
            states[hd] = states[hd] * cd_ref[hd] + kv


def _retention_tables():
    h, c = RET_HEADS, RET_BLOCK
    lg = jnp.log(1.0 - 2.0 ** (-5.0 - jnp.arange(h, dtype=F32)))
    idx = jnp.arange(c, dtype=F32)
    diff = idx[:, None] - idx[None, :]
    dmask = jnp.where(diff >= 0, jnp.exp(jnp.maximum(diff, 0.0)[None] * lg[:, None, None]), 0.0)
    rows = lambda v: jnp.broadcast_to(v[:, :, None], (h, c, RET_HEAD_DIM))
    qdec = rows(jnp.exp((idx + 1.0)[None, :] * lg[:, None]))
    kdec = rows(jnp.exp((c - 1 - idx)[None, :] * lg[:, None]))
    cdec = jnp.broadcast_to(jnp.exp(c * lg)[:, None, None], (h, RET_HEAD_DIM, RET_HEAD_DIM))
    return dmask, qdec, kdec, cdec


def _retention(proj, ret_out_g, batch, seq):
    n = proj.shape[0]
    base = 3 * ATT_WIDTH // RET_WIDTH
    spec = lambda off: pl.BlockSpec((seq, RET_WIDTH), lambda b: (b, base + off))
    tables = _retention_tables()
    whole = lambda t: pl.BlockSpec(t.shape, lambda b: (0, 0, 0))
    return pl.pallas_call(
        _retention_kernel,
        grid=(batch,),
        in_specs=[spec(0), spec(1), spec(2), spec(3)] + [whole(t) for t in tables]
                 + [pl.BlockSpec((RET_HEADS, 1, RET_HEAD_DIM), lambda b: (0, 0, 0))],
        out_specs=pl.BlockSpec((seq, RET_WIDTH), lambda b: (b, 0)),
        out_shape=jax.ShapeDtypeStruct((n, RET_WIDTH), BF16),
        compiler_params=_params("arbitrary"),
    )(proj, proj, proj, proj, *tables, ret_out_g.reshape(RET_HEADS, 1, RET_HEAD_DIM))


def _top2_route(logits):
    lane = lax.broadcasted_iota(jnp.int32, logits.shape, 1)
    lg = jnp.where(lane < N_EXPERTS, logits, -jnp.inf)
    m1 = jnp.max(lg, axis=-1, keepdims=True)
    i1 = jnp.min(jnp.where(lg == m1, lane, LANES), axis=-1, keepdims=True)
    lg2 = jnp.where(lane == i1, -jnp.inf, lg)
    m2 = jnp.max(lg2, axis=-1, keepdims=True)
    i2 = jnp.min(jnp.where(lg2 == m2, lane, LANES), axis=-1, keepdims=True)
    e = jnp.exp(m2 - m1)
    w1 = 1.0 / (1.0 + e)
    w2 = e / (1.0 + e)
    out = jnp.where(lane == 0, i1.astype(F32), 0.0)
    out = jnp.where(lane == 1, i2.astype(F32), out)
    out = jnp.where(lane == 2, w1, out)
    return jnp.where(lane == 3, w2, out)


def _mix_and_norm(oa_ref, or_ref, x_ref, w_ref, ag_ref, m, ng_ref):
    oa = (_rms(oa_ref[...].astype(F32)) * ag_ref[...]).astype(BF16)
    mix = (jnp.dot(oa, w_ref[:ATT_WIDTH, :].astype(BF16), preferred_element_type=F32)
           + jnp.dot(or_ref[...], w_ref[ATT_WIDTH:, :].astype(BF16), preferred_element_type=F32))
    x1 = x_ref[...] + m[2:3] * mix
    h = _rms(x1) * ng_ref[...] * (1.0 + m[4:5]) + m[3:4]
    return x1, h


def _mixer_specs(tm, d, per_seq, layer):
    row = lambda w: pl.BlockSpec((tm, w), lambda i: (i, 0))
    full = lambda a, b: pl.BlockSpec((a, b), lambda i: (0, 0))
    w_out = pl.BlockSpec((None, ATT_WIDTH + RET_WIDTH, d), lambda i: (layer, 0, 0))
    return [row(ATT_WIDTH), row(RET_WIDTH), row(d), w_out, full(1, ATT_WIDTH),
            pl.BlockSpec((1, 6, d), lambda i: (i // per_seq, 0, 0)), full(1, d)]


def _outproj_route_kernel(oa_ref, or_ref, x_ref, w_ref, ag_ref, mod_ref, ng_ref, rw_ref,
                          x1_ref, h_ref, rt_ref, rt_t_ref):
    x1, h = _mix_and_norm(oa_ref, or_ref, x_ref, w_ref, ag_ref, mod_ref[0], ng_ref)
    x1_ref[...] = x1
    _store_token_tiles(h_ref, h)
    route = _top2_route(jnp.dot(h.astype(BF16), rw_ref[...], preferred_element_type=F32))
    rt_ref[...] = route
    rt_t_ref[...] = route.T[:rt_t_ref.shape[0]]


def _outproj_route(o_a, o_r, x2d, w_all, layer, att_g, mod, norm_g, seq, router_w):
    n, d = x2d.shape
    tm = _tile(seq, ROUTE_ROWS, BF16_SUBLANES)
    sub = d // LANES
    rw = jnp.zeros((d, LANES), BF16).at[:, :N_EXPERTS].set(router_w.astype(BF16))
    return pl.pallas_call(
        _outproj_route_kernel,
        grid=(n // tm,),
        in_specs=_mixer_specs(tm, d, seq // tm, layer) + [pl.BlockSpec((d, LANES), lambda i: (0, 0))],
        out_specs=[pl.BlockSpec((tm, d), lambda i: (i, 0)),
                   pl.BlockSpec((tm * sub, LANES), lambda i: (i, 0)),
                   pl.BlockSpec((tm, LANES), lambda i: (i, 0)),
                   pl.BlockSpec((ROUTE_T_ROWS, tm), lambda i: (0, i))],
        out_shape=[jax.ShapeDtypeStruct((n, d), F32), jax.ShapeDtypeStruct((n * sub, LANES), F32),
                   jax.ShapeDtypeStruct((n, LANES), F32), jax.ShapeDtypeStruct((ROUTE_T_ROWS, n), F32)],
        compiler_params=_params("arbitrary"),
    )(o_a, o_r, x2d, w_all, att_g.reshape(1, ATT_WIDTH), mod, norm_g.reshape(1, d), rw)


def _swiglu(h, wg_ref, wu_ref, wd_ref, act_ref):
    ff = act_ref.shape[1]
    tf = _tile(ff, V7X_MXU_WIDTH, LANES)
    for c0 in range(0, ff, tf):
        g = jnp.dot(h, wg_ref[:, c0:c0 + tf], preferred_element_type=F32)
        u = jnp.dot(h, wu_ref[:, c0:c0 + tf], preferred_element_type=F32)
        act_ref[:, c0:c0 + tf] = (_silu(g) * u).astype(act_ref.dtype)
    return jnp.dot(act_ref[...], wd_ref[...], preferred_element_type=F32)


def _outproj_ffn_kernel(oa_ref, or_ref, x_ref, w_ref, ag_ref, mod_ref, ng_ref, wg_ref, wu_ref, wd_ref, fg_ref,
                        *rest, final):
    if len(rest) == 4:
        rider_ref, o_ref, rider_bf16_ref, act_ref = rest
        rider_bf16_ref[...] = rider_ref[...].astype(BF16)
    else:
        o_ref, act_ref = rest
    m = mod_ref[0]
    x1, h = _mix_and_norm(oa_ref, or_ref, x_ref, w_ref, ag_ref, m, ng_ref)
    y = _swiglu(h.astype(BF16), wg_ref, wu_ref, wd_ref, act_ref)
    x2 = x1 + m[5:6] * y
    o_ref[...] = _rms(x2) * fg_ref[...] if final else x2


def _outproj_ffn(o_a, o_r, x2d, w_all, layer, att_g, mod, norm_g, wg, wu, wd, final_g, seq, final, rider=None):
    n, d = x2d.shape
    ff = wg.shape[1]
    tm = _tile(seq, FFN_ROWS, BF16_SUBLANES)
    whole = lambda t: pl.BlockSpec(t.shape, lambda i: (0, 0))
    in_specs = _mixer_specs(tm, d, seq // tm, layer) + [whole(wg), whole(wu), whole(wd), pl.BlockSpec((1, d), lambda i: (0, 0))]
    out_specs = [pl.BlockSpec((tm, d), lambda i: (i, 0))]
    out_shape = [jax.ShapeDtypeStruct((n, d), F32)]
    args = [o_a, o_r, x2d, w_all, att_g.reshape(1, ATT_WIDTH), mod, norm_g.reshape(1, d), wg, wu, wd,
            final_g.reshape(1, d)]
    if rider is not None:
        r_in, r_out, r_shape = _cast_rider_specs(rider, n // tm, lambda i: (i, 0))
        in_specs.append(r_in)
        out_specs.append(r_out)
        out_shape.append(r_shape)
        args.append(rider)
    out = pl.pallas_call(
        functools.partial(_outproj_ffn_kernel, final=final),
        grid=(n // tm,),
        in_specs=in_specs,
        out_specs=out_specs,
        out_shape=out_shape,
        scratch_shapes=[pltpu.VMEM((tm, ff), BF16)],
        compiler_params=_params("arbitrary"),
    )(*args)
    return out if rider is not None else out[0]


def _moe_ffn_kernel(blk_e_ref, n_used_ref, x_ref, wg_ref, wu_ref, wd_ref, o_ref, xs_ref, act_ref):
    del blk_e_ref
    g, d = xs_ref.shape

    @pl.when(pl.program_id(0) < n_used_ref[0])
    def _():
        xs_ref[...] = _load_token_tiles(x_ref, g, d).astype(BF16)
        _store_token_tiles(o_ref, _swiglu(xs_ref[...], wg_ref.at[0], wu_ref.at[0], wd_ref.at[0], act_ref))

    @pl.when(pl.program_id(0) >= n_used_ref[0])
    def _():
        o_ref[...] = jnp.zeros_like(o_ref)


def _moe_grouped(xb, blk_e, n_used, wg, wu, wd):
    _, d, ff = wg.shape
    sub = d // LANES
    g = FFN_ROWS
    grid_spec = pltpu.PrefetchScalarGridSpec(
        num_scalar_prefetch=2,
        grid=(xb.shape[0] // (g * sub),),
        in_specs=[
            pl.BlockSpec((g * sub, LANES), lambda b, be, nu: (b, 0)),
            pl.BlockSpec((1, d, ff), lambda b, be, nu: (be[b], 0, 0)),
            pl.BlockSpec((1, d, ff), lambda b, be, nu: (be[b], 0, 0)),
            pl.BlockSpec((1, ff, d), lambda b, be, nu: (be[b], 0, 0)),
        ],
        out_specs=pl.BlockSpec((g * sub, LANES), lambda b, be, nu: (b, 0)),
        scratch_shapes=[pltpu.VMEM((g, d), BF16), pltpu.VMEM((g, ff), BF16)],
    )
    return pl.pallas_call(
        _moe_ffn_kernel,
        grid_spec=grid_spec,
        out_shape=jax.ShapeDtypeStruct(xb.shape, F32),
        compiler_params=_params("arbitrary"),
    )(blk_e, n_used, xb, wg, wu, wd)


def _start_run_copies(src_ref, dst_ref, src_row, dst_row, count, sub, sem, wait=False):
    for bit in range(MOE_RUN_BITS):
        size = 1 << bit
        off = jnp.left_shift(jnp.right_shift(count, bit + 1), bit + 1)

        @pl.when(jnp.bitwise_and(jnp.right_shift(count, bit), 1) == 1)
        def _():
            copy = pltpu.make_async_copy(
                src_ref.at[pl.ds(pl.multiple_of((src_row + off) * sub, sub), size * sub)],
                dst_ref.at[pl.ds(pl.multiple_of((dst_row + off) * sub, sub), size * sub)], sem)
            copy.wait() if wait else copy.start()


def _dispatch_kernel(pos_ref, cnt_ref, glob_ref, loc_ref, pad_row_ref, pad_len_ref, h_ref, xb_ref,
                     sbuf, zbuf, sems, *, sub):
    i = pl.program_id(0)
    last = pl.num_programs(0) - 1
    tm = h_ref.shape[0] // sub
    ta = tm * TOP_K
    slot = lax.rem(i, 2)
    g = zbuf.shape[0] // sub
    blocks = xb_ref.shape[0] // (g * sub)

    def zero_fill(wait):
        for e in range(N_EXPERTS):
            _start_run_copies(zbuf, xb_ref, 0, pad_row_ref[e], pad_len_ref[e], sub, sems.at[2], wait)
            unused = pad_len_ref[N_EXPERTS] + e

            @pl.when(unused < blocks)
            def _():
                copy = pltpu.make_async_copy(
                    zbuf, xb_ref.at[pl.ds(pl.multiple_of(unused * (g * sub), g * sub), g * sub)], sems.at[2])
                copy.wait() if wait else copy.start()

    @pl.when(i == 0)
    def _():
        zbuf[...] = jnp.zeros_like(zbuf)
        zero_fill(False)

    def wait_slot(s):
        pltpu.make_async_copy(sbuf.at[s], xb_ref.at[pl.ds(0, ta * sub)], sems.at[s]).wait()

    @pl.when(i >= 2)
    def _():
        wait_slot(slot)

    def place(t, carry):
        v = _token_tile(h_ref, t, sub)[...]
        for k in range(TOP_K):
            _token_tile(sbuf.at[slot], pos_ref[i * ta + k * tm + t], sub, scaled=True)[...] = v
        return carry

    lax.fori_loop(0, tm, place, 0, unroll=8)

    for e in range(N_EXPERTS):
        r = i * N_EXPERTS + e
        _start_run_copies(sbuf.at[slot], xb_ref, loc_ref[r], glob_ref[r], cnt_ref[r], sub, sems.at[slot])

    @pl.when(i == last)
    def _():
        @pl.when(i >= 1)
        def _():
            wait_slot(1 - slot)
        wait_slot(slot)
        zero_fill(True)


def _dispatch(h_tiles, layout, seq, d):
    sub = d // LANES
    n = h_tiles.shape[0] // sub
    tm = _tile(seq, MOE_TILE_TOKENS, F32_SUBLANES)
    grid_spec = pltpu.PrefetchScalarGridSpec(
        num_scalar_prefetch=6,
        grid=(n // tm,),
        in_specs=[pl.BlockSpec((tm * sub, LANES), lambda i, *_: (i, 0))],
        out_specs=pl.BlockSpec(memory_space=pl.ANY),
        scratch_shapes=[pltpu.VMEM((2, tm * TOP_K * sub, LANES), h_tiles.dtype),
                        pltpu.VMEM((FFN_ROWS * sub, LANES), h_tiles.dtype),
                        pltpu.SemaphoreType.DMA((3,))],
    )
    return pl.pallas_call(
        functools.partial(_dispatch_kernel, sub=sub),
        grid_spec=grid_spec,
        out_shape=jax.ShapeDtypeStruct((layout["p_total"] * sub, LANES), h_tiles.dtype),
        compiler_params=_params("arbitrary"),
    )(layout["pos"], layout["cnt"], layout["glob"], layout["loc"], layout["pad_row"], layout["pad_len"], h_tiles)


def _combine_kernel(pos_ref, cnt_ref, glob_ref, loc_ref, x1_ref, rt_ref, mod_ref, fg_ref, yb_ref, o_ref,
                    ybuf, tbuf, sems, *, final):
    i = pl.program_id(0)
    tm, d = x1_ref.shape
    sub = d // LANES
    ta = tm * TOP_K
    slot = lax.rem(i, 2)

    def fetch(tile, s):
        for e in range(N_EXPERTS):
            r = tile * N_EXPERTS + e
            _start_run_copies(yb_ref, ybuf.at[s], glob_ref[r], loc_ref[r], cnt_ref[r], sub, sems.at[s])

    @pl.when(i == 0)
    def _():
        fetch(0, 0)

    @pl.when(i + 1 < pl.num_programs(0))
    def _():
        fetch(i + 1, 1 - slot)

    pltpu.make_async_copy(yb_ref.at[pl.ds(0, ta * sub)], ybuf.at[slot], sems.at[slot]).wait()

    def unpermute(t, carry):
        for k in range(TOP_K):
            src = _token_tile(ybuf.at[slot], pos_ref[i * ta + k * tm + t], sub, scaled=True)
            _token_tile(tbuf.at[k], t, sub)[...] = src[...]
        return carry

    lax.fori_loop(0, tm, unpermute, 0, unroll=8)

    rt = rt_ref[...]
    y = (rt[:, 2:3] * _load_token_tiles(tbuf.at[0], tm, d) + rt[:, 3:4] * _load_token_tiles(tbuf.at[1], tm, d))
    x2 = x1_ref[...] + mod_ref[0][5:6] * y
    o_ref[...] = _rms(x2) * fg_ref[...] if final else x2


def _combine(x1, yb, layout, route, mod, final_g, seq, final):
    n, d = x1.shape
    sub = d // LANES
    tm = _tile(seq, MOE_TILE_TOKENS, F32_SUBLANES)
    per_seq = seq // tm
    grid_spec = pltpu.PrefetchScalarGridSpec(
        num_scalar_prefetch=4,
        grid=(n // tm,),
        in_specs=[
            pl.BlockSpec((tm, d), lambda i, *_: (i, 0)),
            pl.BlockSpec((tm, LANES), lambda i, *_: (i, 0)),
            pl.BlockSpec((1, 6, d), lambda i, *_: (i // per_seq, 0, 0)),
            pl.BlockSpec((1, d), lambda i, *_: (0, 0)),
            pl.BlockSpec(memory_space=pl.ANY),
        ],
        out_specs=pl.BlockSpec((tm, d), lambda i, *_: (i, 0)),
        scratch_shapes=[pltpu.VMEM((2, TOP_K * tm * sub, LANES), F32),
                        pltpu.VMEM((TOP_K, tm * sub, LANES), F32),
                        pltpu.SemaphoreType.DMA((2,))],
    )
    return pl.pallas_call(
        functools.partial(_combine_kernel, final=final),
        grid_spec=grid_spec,
        out_shape=jax.ShapeDtypeStruct((n, d), F32),
        compiler_params=_params("arbitrary"),
    )(layout["pos"], layout["cnt"], layout["glob"], layout["loc"], x1, route, mod, final_g.reshape(1, d), yb)


def _moe_layout(route_t, n, seq, sub):
    g = FFN_ROWS
    a = n * TOP_K
    ta = _tile(seq, MOE_TILE_TOKENS, F32_SUBLANES) * TOP_K
    tiles = a // ta
    assert ta // TOP_K < (1 << MOE_RUN_BITS)
    e_flat = route_t[:TOP_K].astype(jnp.int32).reshape(TOP_K, tiles, ta // TOP_K).transpose(1, 0, 2).reshape(-1)
    onehot = (e_flat[None, :] == jnp.arange(N_EXPERTS, dtype=jnp.int32)[:, None]).astype(jnp.int32)
    csum = jnp.cumsum(onehot, axis=1)
    rank = jnp.sum(csum * onehot, axis=0) - 1
    counts = csum[:, -1]
    padded = (counts + g - 1) // g * g
    pad_ends = jnp.cumsum(padded)
    pad_starts = pad_ends - padded
    dest = jnp.sum(pad_starts[:, None] * onehot, axis=0) + rank
    tile_end = csum[:, ta - 1::ta]
    cnt = tile_end - jnp.concatenate([jnp.zeros((N_EXPERTS, 1), jnp.int32), tile_end[:, :-1]], axis=1)
    glob = pad_starts[:, None] + tile_end - cnt
    loc = jnp.cumsum(cnt, axis=0) - cnt
    per_assignment = lambda v: jnp.sum(
        jnp.broadcast_to(v[:, :, None], (N_EXPERTS, tiles, ta)).reshape(N_EXPERTS, a) * onehot, axis=0)
    pos = per_assignment(loc - glob) + dest
    p_total = (-(-a // g) + N_EXPERTS) * g
    blk_start = jnp.arange(p_total // g, dtype=jnp.int32) * g
    blk_e = jnp.minimum(jnp.sum((pad_ends[:, None] <= blk_start[None, :]).astype(jnp.int32), axis=0),
                        N_EXPERTS - 1)
    i32 = lambda v: v.astype(jnp.int32).reshape(-1)
    n_used = pad_ends[-1:] // g
    cnt, glob, loc = cnt.T, glob.T, loc.T
    return dict(pos=i32(pos * sub), cnt=i32(cnt), glob=i32(glob), loc=i32(loc), blk_e=i32(blk_e), n_used=i32(n_used),
                pad_row=i32(pad_starts + counts), pad_len=i32(jnp.concatenate([padded - counts, n_used])),
                p_total=p_total)


def _moe_ffn(h_tiles, x1, route, route_t, mod, wg, wu, wd, final_g, seq, final):
    n, d = x1.shape
    layout = _moe_layout(route_t, n, seq, d // LANES)
    xb = _dispatch(h_tiles, layout, seq, d)
    yb = _moe_grouped(xb, layout["blk_e"], layout["n_used"], wg, wu, wd)
    return _combine(x1, yb, layout, route, mod, final_g, seq, final)


def kernel(x, c, norm_mix_g, norm_ffn_g, ada_w, ada_b, w_in, w_out, att_out_g, ret_out_g, ffn_w_gate,
           ffn_w_up, ffn_w_down, router_w, moe_w_gate, moe_w_up, moe_w_down, final_norm_g):
    batch, seq, d = x.shape
    depth = ada_w.shape[0]
    assert seq % MOBA_BLOCK == 0 and seq % RET_BLOCK == 0
    assert w_in.shape[2] == N_GROUPS * GROUP_WIDTH
    mods = _adaln(c, ada_w, ada_b).reshape(depth, batch, 6, d)
    tables = _rotary_tables(seq)
    xf = x.reshape(batch * seq, d)
    flat = lambda w: w.reshape(-1, w.shape[-1])
    moe_bf16 = {}
    for l in range(depth):
        mod = mods[l]
        final = l == depth - 1
        proj = _inproj(xf, mod, norm_mix_g[l], w_in, l, tables, seq)
        o_r = _retention(proj, ret_out_g[l], batch, seq)
        if l % 2 == 0:
            nxt = (l + 1) // 2
            ride = l + 1 < depth
            riders = [ffn_w_gate[l // 2], ffn_w_up[l // 2], ffn_w_down[l // 2]]
            o_a, wg, wu, wd, *rest = _moba(proj, batch, seq, riders + ([flat(moe_w_gate[nxt])] if ride else []))
            if ride:
                moe_bf16[nxt, "gate"] = rest[0].reshape(moe_w_gate[nxt].shape)
            xf = _outproj_ffn(o_a, o_r, xf, w_out, l, att_out_g[l], mod, norm_ffn_g[l], wg, wu, wd,
                              final_norm_g, seq, final, rider=flat(moe_w_down[nxt]) if ride else None)
            if ride:
                xf, down_bf16 = xf
                moe_bf16[nxt, "down"] = down_bf16.reshape(moe_w_down[nxt].shape)
        else:
            e = l // 2
            o_a, up_bf16 = _moba(proj, batch, seq, [flat(moe_w_up[e])])
            x1, h, route, route_t = _outproj_route(o_a, o_r, xf, w_out, l, att_out_g[l], mod, norm_ffn_g[l],
                                                   seq, router_w[e])
            xf = _moe_ffn(h, x1, route, route_t, mod, moe_bf16[e, "gate"], up_bf16.reshape(moe_w_up[e].shape),
                          moe_bf16[e, "down"], final_norm_g, seq, final)
    return xf.reshape(batch, seq, d)
```

```python
import functools

import jax
import jax.numpy as jnp
from jax import lax
from jax.experimental import pallas as pl
from jax.experimental.pallas import tpu as pltpu

F32 = jnp.float32
BF16 = jnp.bfloat16

LANES = 128
F32_SUBLANES = 8
BF16_SUBLANES = 16
V7X_MXU_WIDTH = 256
V7X_VMEM_LIMIT_BYTES = 56 * 1024 * 1024

ADALN_COLS = 3072
INPROJ_ROWS = 1024
ROUTE_ROWS = 1024

ATT_HEADS = 8
ATT_HEAD_DIM = 64
ATT_WIDTH = ATT_HEADS * ATT_HEAD_DIM
MOBA_BLOCK = 256
MOBA_TOPK = 3
MOBA_PAIRS_PER_STEP = 2
MOBA_RANK_ROWS = F32_SUBLANES
MOBA_GATE_ROWS = BF16_SUBLANES
MASK_VALUE = -1e30
MOBA_Q_SCALE = ATT_HEAD_DIM ** -0.5 * 1.4426950408889634
RET_HEADS = 4
RET_HEAD_DIM = 128
RET_WIDTH = RET_HEADS * RET_HEAD_DIM
RET_BLOCK = 256
ROPE_BASE = 10000.0
N_EXPERTS = 8
TOP_K = 2
EPS = 1e-6
GROUP_WIDTH = 512
N_GROUPS = 7
MOE_TILE_TOKENS = 1024
MOE_RUN_BITS = 11
FFN_ROWS = 512
ROUTE_T_ROWS = F32_SUBLANES


def _tile(total, target, mult):
    best = None
    t = mult
    while t <= min(total, target):
        if total % t == 0:
            best = t
        t += mult
    return best if best is not None else total


def _params(*sem):
    return pltpu.CompilerParams(dimension_semantics=sem, vmem_limit_bytes=V7X_VMEM_LIMIT_BYTES)


def _rms(x):
    return x * lax.rsqrt(jnp.mean(x * x, axis=-1, keepdims=True) + EPS)


def _silu(x):
    return x * jax.nn.sigmoid(x)


def _store_token_tiles(ref, x, row0=0):
    rows, d = x.shape
    sub = d // LANES
    for s in range(sub):
        ref[pl.ds(row0 * sub + s, rows, stride=sub), :] = x[:, s * LANES:(s + 1) * LANES].astype(ref.dtype)


def _load_token_tiles(ref, rows, d, row0=0):
    sub = d // LANES
    return jnp.concatenate([ref[pl.ds(row0 * sub + s, rows, stride=sub), :] for s in range(sub)], axis=1)


def _token_tile(ref, t, sub, scaled=False):
    return ref.at[pl.ds(pl.multiple_of(t if scaled else t * sub, sub), sub)]


def _adaln_kernel(c_ref, w_ref, b_ref, o_ref):
    cs = _silu(c_ref[...]).astype(BF16)
    o_ref[0] = jnp.dot(cs, w_ref[0].astype(BF16), preferred_element_type=F32) + b_ref[0]


def _adaln(c, ada_w, ada_b):
    depth, d, w = ada_w.shape
    b = c.shape[0]
    tn = _tile(w, ADALN_COLS, LANES)
    return pl.pallas_call(
        _adaln_kernel,
        grid=(depth, w // tn),
        in_specs=[
            pl.BlockSpec((b, d), lambda l, j: (0, 0)),
            pl.BlockSpec((1, d, tn), lambda l, j: (l, 0, j)),
            pl.BlockSpec((1, 1, tn), lambda l, j: (l, 0, j)),
        ],
        out_specs=pl.BlockSpec((1, b, tn), lambda l, j: (l, 0, j)),
        out_shape=jax.ShapeDtypeStruct((depth, b, w), F32),
        compiler_params=_params("arbitrary", "arbitrary"),
    )(c, ada_w, ada_b.reshape(depth, 1, w))


def _inproj_kernel(x_ref, mod_ref, g_ref, w_ref, cq_ref, sq_ref, ck_ref, sk_ref, o_ref):
    m = mod_ref[0]
    h = (_rms(x_ref[...]) * g_ref[...] * (1.0 + m[1:2]) + m[0:1]).astype(BF16)
    rotary = {3: (cq_ref, sq_ref), 4: (ck_ref, sk_ref)}
    for j in range(N_GROUPS):
        c0 = j * GROUP_WIDTH
        acc = jnp.dot(h, w_ref[:, c0:c0 + GROUP_WIDTH].astype(BF16), preferred_element_type=F32)
        if j in rotary:
            cos = rotary[j][0][...]
            sin = rotary[j][1][...]
            for hd in range(RET_HEADS):
                a = acc[:, hd * RET_HEAD_DIM:(hd + 1) * RET_HEAD_DIM]
                r = a * cos + pltpu.roll(a, RET_HEAD_DIM // 2, 1) * sin
                o_ref[:, c0 + hd * RET_HEAD_DIM:c0 + (hd + 1) * RET_HEAD_DIM] = r.astype(o_ref.dtype)
        elif j == 0:
            o_ref[:, c0:c0 + GROUP_WIDTH] = (acc * MOBA_Q_SCALE).astype(o_ref.dtype)
        else:
            o_ref[:, c0:c0 + GROUP_WIDTH] = acc.astype(o_ref.dtype)


def _rotary_tables(s):
    half = RET_HEAD_DIM // 2
    inv_freq = ROPE_BASE ** (-jnp.arange(half, dtype=F32) / half)
    ang = jnp.arange(s, dtype=F32)[:, None] * inv_freq[None, :]
    cos = jnp.cos(ang)
    sin = jnp.sin(ang)
    cos2 = jnp.concatenate([cos, cos], axis=-1)
    sin2 = jnp.concatenate([-sin, sin], axis=-1)
    k_scale = RET_HEAD_DIM ** -0.5
    return cos2, sin2, cos2 * k_scale, sin2 * k_scale


def _inproj(x2d, mod, g, w_all, layer, tables, seq):
    n, d = x2d.shape
    tm = _tile(seq, INPROJ_ROWS, BF16_SUBLANES)
    per_seq = seq // tm
    width = N_GROUPS * GROUP_WIDTH
    return pl.pallas_call(
        _inproj_kernel,
        grid=(n // tm,),
        in_specs=[
            pl.BlockSpec((tm, d), lambda i: (i, 0)),
            pl.BlockSpec((1, 6, d), lambda i: (i // per_seq, 0, 0)),
            pl.BlockSpec((1, d), lambda i: (0, 0)),
            pl.BlockSpec((None, d, width), lambda i: (layer, 0, 0)),
        ] + [pl.BlockSpec((tm, RET_HEAD_DIM), lambda i: (i % per_seq, 0))] * 4,
        out_specs=pl.BlockSpec((tm, width), lambda i: (i, 0)),
        out_shape=jax.ShapeDtypeStruct((n, width), BF16),
        compiler_params=_params("arbitrary"),
    )(x2d, mod, g.reshape(1, d), w_all, *tables)


def _cast_rider_specs(rider, steps, index):
    rows, cols = rider.shape
    assert rows % steps == 0 and (rows // steps) % BF16_SUBLANES == 0
    spec = pl.BlockSpec((rows // steps, cols), index)
    return spec, spec, jax.ShapeDtypeStruct(rider.shape, BF16)


def _moba_kernel(q_ref, k_ref, v_ref, *rest):
    riders = (len(rest) - 1) // 2
    o_ref = rest[riders]
    for w_ref, w_bf16_ref in zip(rest[:riders], rest[riders + 1:]):
        w_bf16_ref[...] = w_ref[...].astype(BF16)
    s = q_ref.shape[0]
    blk = MOBA_BLOCK
    nb = s // blk
    assert nb <= MOBA_RANK_ROWS
    n_sel = min(MOBA_TOPK, nb - 1)
    shift = blk.bit_length() - 1
    contract_lanes = (((1,), (1,)), ((), ()))
    lane = lax.broadcasted_iota(jnp.int32, (1, LANES), 1)
    head_of_lane = jnp.right_shift(lane, ATT_HEAD_DIM.bit_length() - 1)

    key_pos = lax.broadcasted_iota(jnp.int32, (s, LANES), 0)
    key_lane = lax.broadcasted_iota(jnp.int32, (s, LANES), 1)
    key_block_onehot = jnp.where(jnp.right_shift(key_pos, shift) == key_lane, 1.0, 0.0).astype(BF16)
    ones = jnp.ones((s, LANES), BF16)

    blk_id = lax.broadcasted_iota(jnp.int32, (MOBA_RANK_ROWS, s), 0)
    q_blk = jnp.right_shift(lax.broadcasted_iota(jnp.int32, (MOBA_RANK_ROWS, s), 1), shift)
    past = blk_id < q_blk
    block_mean = jnp.where(
        jnp.right_shift(lax.broadcasted_iota(jnp.int32, (MOBA_GATE_ROWS, s), 1), shift)
        == lax.broadcasted_iota(jnp.int32, (MOBA_GATE_ROWS, s), 0), 1.0 / blk, 0.0).astype(BF16)
    no_bias = jnp.zeros((MOBA_GATE_ROWS - MOBA_RANK_ROWS, s), F32)
    eye = jnp.where(lax.broadcasted_iota(jnp.int32, (MOBA_GATE_ROWS, LANES), 0)
                    == lax.broadcasted_iota(jnp.int32, (MOBA_GATE_ROWS, LANES), 1), 1.0, 0.0).astype(BF16)
    row = lax.broadcasted_iota(jnp.int32, (2 * blk, blk), 0)
    col = lax.broadcasted_iota(jnp.int32, (2 * blk, blk), 1)
    causal_bias = jnp.where(col <= jnp.bitwise_and(row, blk - 1), 0.0, MASK_VALUE)

    def prepare(pair):
        cols = slice(pair * LANES, (pair + 1) * LANES)
        q_all = q_ref[:, cols]
        k_all = k_ref[:, cols]
        k_aug = jnp.concatenate([k_all, key_block_onehot], axis=1)
        v_aug = jnp.concatenate([v_ref[:, cols], ones], axis=1)
        k_mean = jnp.dot(block_mean, k_all, preferred_element_type=F32).astype(BF16)

        heads = []
        for hh in range(2):
            qh = jnp.where(head_of_lane == hh, q_all, jnp.zeros_like(q_all))
            gate = lax.dot_general(k_mean, qh, contract_lanes, preferred_element_type=F32)
            gm = jnp.where(past, gate[:MOBA_RANK_ROWS], MASK_VALUE)
            rank = jnp.zeros(gm.shape, jnp.int32)
            for jp in range(nb):
                cv = gm[jp:jp + 1, :]
                beats = jnp.logical_or(cv > gm, jnp.logical_and(cv == gm, jp < blk_id))
                rank = rank + jnp.where(beats, 1, 0)
            visible = jnp.logical_or(jnp.logical_and(rank < n_sel, past), blk_id >= q_blk)
            bias_t = jnp.concatenate([jnp.where(visible, 0.0, MASK_VALUE), no_bias], axis=0).astype(BF16)
            bias = lax.dot_general(bias_t, eye, (((0,), (0,)), ((), ())), preferred_element_type=F32)
            heads.append(jnp.concatenate([qh, bias.astype(BF16)], axis=1))
        return cols, heads, k_aug, v_aug

    def scores(i, pair):
        _, heads, k_aug, _ = pair
        rows = slice(i * blk, (i + 1) * blk)
        q_both = jnp.concatenate([heads[0][rows], heads[1][rows]], axis=0)
        return lax.dot_general(q_both, k_aug[:(i + 1) * blk], contract_lanes, preferred_element_type=F32)

    def softmax_pv(i, pair, sc):
        cols, _, _, v_aug = pair
        sc_own = sc[:, i * blk:] + causal_bias
        chunks = [sc_own[:, c:c + LANES] for c in range(0, blk, LANES)]
        chunks += [sc[:, c:c + LANES] for c in range(0, i * blk, LANES)]
        m = jnp.max(functools.reduce(jnp.maximum, chunks), axis=-1, keepdims=True)
        if i > 0:
            p = jnp.concatenate([jnp.exp2(sc[:, :i * blk] - m), jnp.exp2(sc_own - m)], axis=1)
        else:
            p = jnp.exp2(sc_own - m)
        o = jnp.dot(p.astype(BF16), v_aug[:(i + 1) * blk], preferred_element_type=F32)
        o = o[:, :LANES] / o[:, LANES:]
        o_ref[i * blk:(i + 1) * blk, cols] = jnp.where(head_of_lane == 0, o[:blk], o[blk:]).astype(o_ref.dtype)

    pairs = {}
    for i in reversed(range(nb)):
        for idx in range(q_ref.shape[1] // LANES):
            if idx not in pairs:
                pairs[idx] = prepare(idx)
            softmax_pv(i, pairs[idx], scores(i, pairs[idx]))


def _moba(proj, batch, seq, riders=()):
    n = proj.shape[0]
    width = MOBA_PAIRS_PER_STEP * LANES
    steps = ATT_WIDTH // width
    spec = lambda off: pl.BlockSpec((seq, width), lambda b, p: (b, off + p))
    in_specs = [spec(0), spec(steps), spec(2 * steps)]
    out_specs = [pl.BlockSpec((seq, width), lambda b, p: (b, p))]
    out_shape = [jax.ShapeDtypeStruct((n, ATT_WIDTH), BF16)]
    args = [proj, proj, proj]
    for rider in riders:
        r_in, r_out, r_shape = _cast_rider_specs(rider, batch * steps, lambda b, p: (b * steps + p, 0))
        in_specs.append(r_in)
        out_specs.append(r_out)
        out_shape.append(r_shape)
        args.append(rider)
    return pl.pallas_call(
        _moba_kernel,
        grid=(batch, steps),
        in_specs=in_specs,
        out_specs=out_specs,
        out_shape=out_shape,
        compiler_params=_params("arbitrary", "arbitrary"),
    )(*args)


def _retention_kernel(q_ref, k_ref, v_ref, g_ref, dm_ref, qd_ref, kd_ref, cd_ref, og_ref, o_ref):
    s = q_ref.shape[0]
    c = RET_BLOCK
    states = [jnp.zeros((RET_HEAD_DIM, RET_HEAD_DIM), F32) for _ in range(RET_HEADS)]
    for n in range(s // c):
        rows = slice(n * c, (n + 1) * c)
        for hd in range(RET_HEADS):
            cols = slice(hd * RET_HEAD_DIM, (hd + 1) * RET_HEAD_DIM)
            qc = q_ref[rows, cols]
            kc = k_ref[rows, cols]
            vc = v_ref[rows, cols]
            scores = lax.dot_general(qc, kc, (((1,), (1,)), ((), ())), preferred_element_type=F32) * dm_ref[hd]
            inner = jnp.dot(scores.astype(BF16), vc, preferred_element_type=F32)
            cross = jnp.dot(qc, states[hd].astype(BF16), preferred_element_type=F32) * qd_ref[hd]
            o = _rms(inner + cross) * og_ref[hd]
            gate = g_ref[rows, cols].astype(F32)
            o_ref[rows, cols] = (_silu(gate) * o).astype(o_ref.dtype)
            k_dec = (kc.astype(F32) * kd_ref[hd]).astype(BF16)
            kv = lax.dot_general(k_dec, vc, (((0,), (0,)), ((), ())), preferred_element_type=F32)
            states[hd] = states[hd] * cd_ref[hd] + kv


def _retention_tables():
    h, c = RET_HEADS, RET_BLOCK
    lg = jnp.log(1.0 - 2.0 ** (-5.0 - jnp.arange(h, dtype=F32)))
    idx = jnp.arange(c, dtype=F32)
    diff = idx[:, None] - idx[None, :]
    dmask = jnp.where(diff >= 0, jnp.exp(jnp.maximum(diff, 0.0)[None] * lg[:, None, None]), 0.0)
    rows = lambda v: jnp.broadcast_to(v[:, :, None], (h, c, RET_HEAD_DIM))
    qdec = rows(jnp.exp((idx + 1.0)[None, :] * lg[:, None]))
    kdec = rows(jnp.exp((c - 1 - idx)[None, :] * lg[:, None]))
    cdec = jnp.broadcast_to(jnp.exp(c * lg)[:, None, None], (h, RET_HEAD_DIM, RET_HEAD_DIM))
    return dmask, qdec, kdec, cdec


def _retention(proj, ret_out_g, batch, seq):
    n = proj.shape[0]
    base = 3 * ATT_WIDTH // RET_WIDTH
    spec = lambda off: pl.BlockSpec((seq, RET_WIDTH), lambda b: (b, base + off))
    tables = _retention_tables()
    whole = lambda t: pl.BlockSpec(t.shape, lambda b: (0, 0, 0))
    return pl.pallas_call(
        _retention_kernel,
        grid=(batch,),
        in_specs=[spec(0), spec(1), spec(2), spec(3)] + [whole(t) for t in tables]
                 + [pl.BlockSpec((RET_HEADS, 1, RET_HEAD_DIM), lambda b: (0, 0, 0))],
        out_specs=pl.BlockSpec((seq, RET_WIDTH), lambda b: (b, 0)),
        out_shape=jax.ShapeDtypeStruct((n, RET_WIDTH), BF16),
        compiler_params=_params("arbitrary"),
    )(proj, proj, proj, proj, *tables, ret_out_g.reshape(RET_HEADS, 1, RET_HEAD_DIM))


def _top2_route(logits):
    lane = lax.broadcasted_iota(jnp.int32, logits.shape, 1)
    lg = jnp.where(lane < N_EXPERTS, logits, -jnp.inf)
    m1 = jnp.max(lg, axis=-1, keepdims=True)
    i1 = jnp.min(jnp.where(lg == m1, lane, LANES), axis=-1, keepdims=True)
    lg2 = jnp.where(lane == i1, -jnp.inf, lg)
    m2 = jnp.max(lg2, axis=-1, keepdims=True)
    i2 = jnp.min(jnp.where(lg2 == m2, lane, LANES), axis=-1, keepdims=True)
    e = jnp.exp(m2 - m1)
    w1 = 1.0 / (1.0 + e)
    w2 = e / (1.0 + e)
    out = jnp.where(lane == 0, i1.astype(F32), 0.0)
    out = jnp.where(lane == 1, i2.astype(F32), out)
    out = jnp.where(lane == 2, w1, out)
    return jnp.where(lane == 3, w2, out)


def _mix_and_norm(oa_ref, or_ref, x_ref, w_ref, ag_ref, m, ng_ref):
    oa = (_rms(oa_ref[...].astype(F32)) * ag_ref[...]).astype(BF16)
    mix = (jnp.dot(oa, w_ref[:ATT_WIDTH, :].astype(BF16), preferred_element_type=F32)
           + jnp.dot(or_ref[...], w_ref[ATT_WIDTH:, :].astype(BF16), preferred_element_type=F32))
    x1 = x_ref[...] + m[2:3] * mix
    h = _rms(x1) * ng_ref[...] * (1.0 + m[4:5]) + m[3:4]
    return x1, h


def _mixer_specs(tm, d, per_seq, layer):
    row = lambda w: pl.BlockSpec((tm, w), lambda i: (i, 0))
    full = lambda a, b: pl.BlockSpec((a, b), lambda i: (0, 0))
    w_out = pl.BlockSpec((None, ATT_WIDTH + RET_WIDTH, d), lambda i: (layer, 0, 0))
    return [row(ATT_WIDTH), row(RET_WIDTH), row(d), w_out, full(1, ATT_WIDTH),
            pl.BlockSpec((1, 6, d), lambda i: (i // per_seq, 0, 0)), full(1, d)]


def _outproj_route_kernel(oa_ref, or_ref, x_ref, w_ref, ag_ref, mod_ref, ng_ref, rw_ref,
                          x1_ref, h_ref, rt_ref, rt_t_ref):
    x1, h = _mix_and_norm(oa_ref, or_ref, x_ref, w_ref, ag_ref, mod_ref[0], ng_ref)
    x1_ref[...] = x1
    _store_token_tiles(h_ref, h)
    route = _top2_route(jnp.dot(h.astype(BF16), rw_ref[...], preferred_element_type=F32))
    rt_ref[...] = route
    rt_t_ref[...] = route.T[:rt_t_ref.shape[0]]


def _outproj_route(o_a, o_r, x2d, w_all, layer, att_g, mod, norm_g, seq, router_w):
    n, d = x2d.shape
    tm = _tile(seq, ROUTE_ROWS, BF16_SUBLANES)
    sub = d // LANES
    rw = jnp.zeros((d, LANES), BF16).at[:, :N_EXPERTS].set(router_w.astype(BF16))
    return pl.pallas_call(
        _outproj_route_kernel,
        grid=(n // tm,),
        in_specs=_mixer_specs(tm, d, seq // tm, layer) + [pl.BlockSpec((d, LANES), lambda i: (0, 0))],
        out_specs=[pl.BlockSpec((tm, d), lambda i: (i, 0)),
                   pl.BlockSpec((tm * sub, LANES), lambda i: (i, 0)),
                   pl.BlockSpec((tm, LANES), lambda i: (i, 0)),
                   pl.BlockSpec((ROUTE_T_ROWS, tm), lambda i: (0, i))],
        out_shape=[jax.ShapeDtypeStruct((n, d), F32), jax.ShapeDtypeStruct((n * sub, LANES), F32),
                   jax.ShapeDtypeStruct((n, LANES), F32), jax.ShapeDtypeStruct((ROUTE_T_ROWS, n), F32)],
        compiler_params=_params("arbitrary"),
    )(o_a, o_r, x2d, w_all, att_g.reshape(1, ATT_WIDTH), mod, norm_g.reshape(1, d), rw)


def _swiglu(h, wg_ref, wu_ref, wd_ref, act_ref):
    ff = act_ref.shape[1]
    tf = _tile(ff, V7X_MXU_WIDTH, LANES)
    for c0 in range(0, ff, tf):
        g = jnp.dot(h, wg_ref[:, c0:c0 + tf], preferred_element_type=F32)
        u = jnp.dot(h, wu_ref[:, c0:c0 + tf], preferred_element_type=F32)
        act_ref[:, c0:c0 + tf] = (_silu(g) * u).astype(act_ref.dtype)
    return jnp.dot(act_ref[...], wd_ref[...], preferred_element_type=F32)


def _outproj_ffn_kernel(oa_ref, or_ref, x_ref, w_ref, ag_ref, mod_ref, ng_ref, wg_ref, wu_ref, wd_ref, fg_ref,
                        *rest, final):
    if len(rest) == 4:
        rider_ref, o_ref, rider_bf16_ref, act_ref = rest
        rider_bf16_ref[...] = rider_ref[...].astype(BF16)
    else:
        o_ref, act_ref = rest
    m = mod_ref[0]
    x1, h = _mix_and_norm(oa_ref, or_ref, x_ref, w_ref, ag_ref, m, ng_ref)
    y = _swiglu(h.astype(BF16), wg_ref, wu_ref, wd_ref, act_ref)
    x2 = x1 + m[5:6] * y
    o_ref[...] = _rms(x2) * fg_ref[...] if final else x2


def _outproj_ffn(o_a, o_r, x2d, w_all, layer, att_g, mod, norm_g, wg, wu, wd, final_g, seq, final, rider=None):
    n, d = x2d.shape
    ff = wg.shape[1]
    tm = _tile(seq, FFN_ROWS, BF16_SUBLANES)
    whole = lambda t: pl.BlockSpec(t.shape, lambda i: (0, 0))
    in_specs = _mixer_specs(tm, d, seq // tm, layer) + [whole(wg), whole(wu), whole(wd), pl.BlockSpec((1, d), lambda i: (0, 0))]
    out_specs = [pl.BlockSpec((tm, d), lambda i: (i, 0))]
    out_shape = [jax.ShapeDtypeStruct((n, d), F32)]
    args = [o_a, o_r, x2d, w_all, att_g.reshape(1, ATT_WIDTH), mod, norm_g.reshape(1, d), wg, wu, wd,
            final_g.reshape(1, d)]
    if rider is not None:
        r_in, r_out, r_shape = _cast_rider_specs(rider, n // tm, lambda i: (i, 0))
        in_specs.append(r_in)
        out_specs.append(r_out)
        out_shape.append(r_shape)
        args.append(rider)
    out = pl.pallas_call(
        functools.partial(_outproj_ffn_kernel, final=final),
        grid=(n // tm,),
        in_specs=in_specs,
        out_specs=out_specs,
        out_shape=out_shape,
        scratch_shapes=[pltpu.VMEM((tm, ff), BF16)],
        compiler_params=_params("arbitrary"),
    )(*args)
    return out if rider is not None else out[0]


def _moe_ffn_kernel(blk_e_ref, blk_rows_ref, x_ref, wg_ref, wu_ref, wd_ref, o_ref, xs_ref, act_ref):
    del blk_e_ref
    g, d = xs_ref.shape
    sub = d // LANES
    half = g // 2
    rows = blk_rows_ref[pl.program_id(0)]

    def run(m):
        xs = xs_ref.at[pl.ds(0, m)]
        xs[...] = _load_token_tiles(x_ref, m, d).astype(BF16)
        y = _swiglu(xs[...], wg_ref.at[0], wu_ref.at[0], wd_ref.at[0], act_ref.at[pl.ds(0, m)])
        _store_token_tiles(o_ref, y)

    @pl.when(rows > half)
    def _():
        run(g)

    @pl.when(jnp.logical_and(rows > 0, rows <= half))
    def _():
        run(half)
        o_ref[pl.ds(half * sub, half * sub), :] = jnp.zeros((half * sub, LANES), o_ref.dtype)

    @pl.when(rows == 0)
    def _():
        o_ref[...] = jnp.zeros_like(o_ref)


def _moe_grouped(xb, blk_e, blk_rows, wg, wu, wd):
    _, d, ff = wg.shape
    sub = d // LANES
    g = FFN_ROWS
    grid_spec = pltpu.PrefetchScalarGridSpec(
        num_scalar_prefetch=2,
        grid=(xb.shape[0] // (g * sub),),
        in_specs=[
            pl.BlockSpec((g * sub, LANES), lambda b, be, nu: (b, 0)),
            pl.BlockSpec((1, d, ff), lambda b, be, nu: (be[b], 0, 0)),
            pl.BlockSpec((1, d, ff), lambda b, be, nu: (be[b], 0, 0)),
            pl.BlockSpec((1, ff, d), lambda b, be, nu: (be[b], 0, 0)),
        ],
        out_specs=pl.BlockSpec((g * sub, LANES), lambda b, be, nu: (b, 0)),
        scratch_shapes=[pltpu.VMEM((g, d), BF16), pltpu.VMEM((g, ff), BF16)],
    )
    return pl.pallas_call(
        _moe_ffn_kernel,
        grid_spec=grid_spec,
        out_shape=jax.ShapeDtypeStruct(xb.shape, F32),
        compiler_params=_params("arbitrary"),
    )(blk_e, blk_rows, xb, wg, wu, wd)


def _start_run_copies(src_ref, dst_ref, src_row, dst_row, count, sub, sem, wait=False):
    for bit in range(MOE_RUN_BITS):
        size = 1 << bit
        off = jnp.left_shift(jnp.right_shift(count, bit + 1), bit + 1)

        @pl.when(jnp.bitwise_and(jnp.right_shift(count, bit), 1) == 1)
        def _():
            copy = pltpu.make_async_copy(
                src_ref.at[pl.ds(pl.multiple_of((src_row + off) * sub, sub), size * sub)],
                dst_ref.at[pl.ds(pl.multiple_of((dst_row + off) * sub, sub), size * sub)], sem)
            copy.wait() if wait else copy.start(priority=bit % 2)


def _dispatch_kernel(pos_ref, cnt_ref, glob_ref, loc_ref, pad_row_ref, pad_len_ref, h_ref, xb_ref,
                     sbuf, zbuf, sems, *, sub):
    i = pl.program_id(0)
    last = pl.num_programs(0) - 1
    tm = h_ref.shape[0] // sub
    ta = tm * TOP_K
    slot = lax.rem(i, 2)
    g = zbuf.shape[0] // sub
    blocks = xb_ref.shape[0] // (g * sub)

    def zero_fill(wait):
        for e in range(N_EXPERTS):
            _start_run_copies(zbuf, xb_ref, 0, pad_row_ref[e], pad_len_ref[e], sub, sems.at[2], wait)
            unused = pad_len_ref[N_EXPERTS] + e

            @pl.when(unused < blocks)
            def _():
                copy = pltpu.make_async_copy(
                    zbuf, xb_ref.at[pl.ds(pl.multiple_of(unused * (g * sub), g * sub), g * sub)], sems.at[2])
                copy.wait() if wait else copy.start()

    @pl.when(i == 0)
    def _():
        zbuf[...] = jnp.zeros_like(zbuf)
        zero_fill(False)

    def wait_slot(s):
        pltpu.make_async_copy(sbuf.at[s], xb_ref.at[pl.ds(0, ta * sub)], sems.at[s]).wait()

    @pl.when(i >= 2)
    def _():
        wait_slot(slot)

    def place(t, carry):
        v = _token_tile(h_ref, t, sub)[...]
        for k in range(TOP_K):
            _token_tile(sbuf.at[slot], pos_ref[i * ta + k * tm + t], sub, scaled=True)[...] = v
        return carry

    lax.fori_loop(0, tm, place, 0, unroll=8)

    for e in range(N_EXPERTS):
        r = i * N_EXPERTS + e
        _start_run_copies(sbuf.at[slot], xb_ref, loc_ref[r], glob_ref[r], cnt_ref[r], sub, sems.at[slot])

    @pl.when(i == last)
    def _():
        @pl.when(i >= 1)
        def _():
            wait_slot(1 - slot)
        wait_slot(slot)
        zero_fill(True)


def _dispatch(h_tiles, layout, seq, d):
    sub = d // LANES
    n = h_tiles.shape[0] // sub
    tm = _tile(seq, MOE_TILE_TOKENS, F32_SUBLANES)
    grid_spec = pltpu.PrefetchScalarGridSpec(
        num_scalar_prefetch=6,
        grid=(n // tm,),
        in_specs=[pl.BlockSpec((tm * sub, LANES), lambda i, *_: (i, 0))],
        out_specs=pl.BlockSpec(memory_space=pl.ANY),
        scratch_shapes=[pltpu.VMEM((2, tm * TOP_K * sub, LANES), h_tiles.dtype),
                        pltpu.VMEM((FFN_ROWS * sub, LANES), h_tiles.dtype),
                        pltpu.SemaphoreType.DMA((3,))],
    )
    return pl.pallas_call(
        functools.partial(_dispatch_kernel, sub=sub),
        grid_spec=grid_spec,
        out_shape=jax.ShapeDtypeStruct((layout["p_total"] * sub, LANES), h_tiles.dtype),
        compiler_params=_params("arbitrary"),
    )(layout["pos"], layout["cnt"], layout["glob"], layout["loc"], layout["pad_row"], layout["pad_len"], h_tiles)


def _combine_kernel(pos_ref, cnt_ref, glob_ref, loc_ref, x1_ref, rt_ref, mod_ref, fg_ref, yb_ref, o_ref,
                    ybuf, tbuf, sems, *, final):
    i = pl.program_id(0)
    tm, d = x1_ref.shape
    sub = d // LANES
    ta = tm * TOP_K
    slot = lax.rem(i, 2)

    def fetch(tile, s):
        for e in range(N_EXPERTS):
            r = tile * N_EXPERTS + e
            _start_run_copies(yb_ref, ybuf.at[s], glob_ref[r], loc_ref[r], cnt_ref[r], sub, sems.at[s])

    @pl.when(i == 0)
    def _():
        fetch(0, 0)

    @pl.when(i + 1 < pl.num_programs(0))
    def _():
        fetch(i + 1, 1 - slot)

    pltpu.make_async_copy(yb_ref.at[pl.ds(0, ta * sub)], ybuf.at[slot], sems.at[slot]).wait()

    def unpermute(t, carry):
        for k in range(TOP_K):
            src = _token_tile(ybuf.at[slot], pos_ref[i * ta + k * tm + t], sub, scaled=True)
            _token_tile(tbuf.at[k], t, sub)[...] = src[...]
        return carry

    lax.fori_loop(0, tm, unpermute, 0, unroll=8)

    rt = rt_ref[...]
    y = (rt[:, 2:3] * _load_token_tiles(tbuf.at[0], tm, d) + rt[:, 3:4] * _load_token_tiles(tbuf.at[1], tm, d))
    x2 = x1_ref[...] + mod_ref[0][5:6] * y
    o_ref[...] = _rms(x2) * fg_ref[...] if final else x2


def _combine(x1, yb, layout, route, mod, final_g, seq, final):
    n, d = x1.shape
    sub = d // LANES
    tm = _tile(seq, MOE_TILE_TOKENS, F32_SUBLANES)
    per_seq = seq // tm
    grid_spec = pltpu.PrefetchScalarGridSpec(
        num_scalar_prefetch=4,
        grid=(n // tm,),
        in_specs=[
            pl.BlockSpec((tm, d), lambda i, *_: (i, 0)),
            pl.BlockSpec((tm, LANES), lambda i, *_: (i, 0)),
            pl.BlockSpec((1, 6, d), lambda i, *_: (i // per_seq, 0, 0)),
            pl.BlockSpec((1, d), lambda i, *_: (0, 0)),
            pl.BlockSpec(memory_space=pl.ANY),
        ],
        out_specs=pl.BlockSpec((tm, d), lambda i, *_: (i, 0)),
        scratch_shapes=[pltpu.VMEM((2, TOP_K * tm * sub, LANES), F32),
                        pltpu.VMEM((TOP_K, tm * sub, LANES), F32),
                        pltpu.SemaphoreType.DMA((2,))],
    )
    return pl.pallas_call(
        functools.partial(_combine_kernel, final=final),
        grid_spec=grid_spec,
        out_shape=jax.ShapeDtypeStruct((n, d), F32),
        compiler_params=_params("arbitrary"),
    )(layout["pos"], layout["cnt"], layout["glob"], layout["loc"], x1, route, mod, final_g.reshape(1, d), yb)


def _moe_layout(route_t, n, seq, sub):
    g = FFN_ROWS
    a = n * TOP_K
    ta = _tile(seq, MOE_TILE_TOKENS, F32_SUBLANES) * TOP_K
    tiles = a // ta
    assert ta // TOP_K < (1 << MOE_RUN_BITS)
    e_flat = route_t[:TOP_K].astype(jnp.int32).reshape(TOP_K, tiles, ta // TOP_K).transpose(1, 0, 2).reshape(-1)
    onehot = (e_flat[None, :] == jnp.arange(N_EXPERTS, dtype=jnp.int32)[:, None]).astype(jnp.int32)
    csum = jnp.cumsum(onehot, axis=1)
    rank = jnp.sum(csum * onehot, axis=0) - 1
    counts = csum[:, -1]
    padded = (counts + g - 1) // g * g
    pad_ends = jnp.cumsum(padded)
    pad_starts = pad_ends - padded
    dest = jnp.sum(pad_starts[:, None] * onehot, axis=0) + rank
    tile_end = csum[:, ta - 1::ta]
    cnt = tile_end - jnp.concatenate([jnp.zeros((N_EXPERTS, 1), jnp.int32), tile_end[:, :-1]], axis=1)
    glob = pad_starts[:, None] + tile_end - cnt
    loc = jnp.cumsum(cnt, axis=0) - cnt
    per_assignment = lambda v: jnp.sum(
        jnp.broadcast_to(v[:, :, None], (N_EXPERTS, tiles, ta)).reshape(N_EXPERTS, a) * onehot, axis=0)
    pos = per_assignment(loc - glob) + dest
    p_total = (-(-a // g) + N_EXPERTS) * g
    blk_start = jnp.arange(p_total // g, dtype=jnp.int32) * g
    blk_e = jnp.minimum(jnp.sum((pad_ends[:, None] <= blk_start[None, :]).astype(jnp.int32), axis=0),
                        N_EXPERTS - 1)
    i32 = lambda v: v.astype(jnp.int32).reshape(-1)
    n_used = pad_ends[-1:] // g
    cnt, glob, loc = cnt.T, glob.T, loc.T
    of_blk = (blk_e[None, :] == jnp.arange(N_EXPERTS, dtype=jnp.int32)[:, None]).astype(jnp.int32)
    blk_rows = jnp.clip(jnp.sum((pad_starts + counts)[:, None] * of_blk, axis=0) - blk_start, 0, g)
    return dict(pos=i32(pos * sub), cnt=i32(cnt), glob=i32(glob), loc=i32(loc), blk_e=i32(blk_e),
                blk_rows=i32(blk_rows),
                pad_row=i32(pad_starts + counts), pad_len=i32(jnp.concatenate([padded - counts, n_used])),
                p_total=p_total)


def _moe_ffn(h_tiles, x1, route, route_t, mod, wg, wu, wd, final_g, seq, final):
    n, d = x1.shape
    layout = _moe_layout(route_t, n, seq, d // LANES)
    xb = _dispatch(h_tiles, layout, seq, d)
    yb = _moe_grouped(xb, layout["blk_e"], layout["blk_rows"], wg, wu, wd)
    return _combine(x1, yb, layout, route, mod, final_g, seq, final)


def kernel(x, c, norm_mix_g, norm_ffn_g, ada_w, ada_b, w_in, w_out, att_out_g, ret_out_g, ffn_w_gate,
           ffn_w_up, ffn_w_down, router_w, moe_w_gate, moe_w_up, moe_w_down, final_norm_g):
    batch, seq, d = x.shape
    depth = ada_w.shape[0]
    assert seq % MOBA_BLOCK == 0 and seq % RET_BLOCK == 0
    assert w_in.shape[2] == N_GROUPS * GROUP_WIDTH
    mods = _adaln(c, ada_w, ada_b).reshape(depth, batch, 6, d)
    tables = _rotary_tables(seq)
    xf = x.reshape(batch * seq, d)
    flat = lambda w: w.reshape(-1, w.shape[-1])
    moe_bf16 = {}
    for l in range(depth):
        mod = mods[l]
        final = l == depth - 1
        proj = _inproj(xf, mod, norm_mix_g[l], w_in, l, tables, seq)
        o_r = _retention(proj, ret_out_g[l], batch, seq)
        if l % 2 == 0:
            nxt = (l + 1) // 2
            ride = l + 1 < depth
            riders = [ffn_w_gate[l // 2], ffn_w_up[l // 2], ffn_w_down[l // 2]]
            o_a, wg, wu, wd, *rest = _moba(proj, batch, seq, riders + ([flat(moe_w_gate[nxt])] if ride else []))
            if ride:
                moe_bf16[nxt, "gate"] = rest[0].reshape(moe_w_gate[nxt].shape)
            xf = _outproj_ffn(o_a, o_r, xf, w_out, l, att_out_g[l], mod, norm_ffn_g[l], wg, wu, wd,
                              final_norm_g, seq, final, rider=flat(moe_w_down[nxt]) if ride else None)
            if ride:
                xf, down_bf16 = xf
                moe_bf16[nxt, "down"] = down_bf16.reshape(moe_w_down[nxt].shape)
        else:
            e = l // 2
            o_a, up_bf16 = _moba(proj, batch, seq, [flat(moe_w_up[e])])
            x1, h, route, route_t = _outproj_route(o_a, o_r, xf, w_out, l, att_out_g[l], mod, norm_ffn_g[l],
                                                   seq, router_w[e])
            xf = _moe_ffn(h, x1, route, route_t, mod, moe_bf16[e, "gate"], up_bf16.reshape(moe_w_up[e].shape),
                          moe_bf16[e, "down"], final_norm_g, seq, final)
    return xf.reshape(batch, seq, d)
```
